```python
import jax, jax.numpy as jnp
from jax import lax
import numpy as np

D_MODEL = 2048
BATCH = 2
SEQ = 4096
DEPTH = 1

CTX_LEN = 256
GRID_W = 64
D_MIX = D_MODEL
D_LRU = D_MIX // 2
LRU_BLOCKS = 8
LRU_BS = D_LRU // LRU_BLOCKS
CONV_W = 4
LRU_C = 8.0
D_MV = D_MIX - D_LRU
M_HEADS = 8
M_DV = D_MV // M_HEADS
M_DK = M_DV // 2
D_MQK = M_HEADS * M_DK
CHUNK = 128
D_FF = ((8 * D_MODEL // 3 + 255) // 256) * 256
N_MOD = 9
EPS = 1e-6
HALF = 0.5
IN_WIDTHS = (D_LRU, D_LRU, D_MQK, D_MQK, D_MV, D_MV, 4 * M_HEADS)
D_IN = sum(IN_WIDTHS)
SPLITS = [int(s) for s in np.cumsum(IN_WIDTHS)[:-1]]

kernel_name = "hybrid_rglru_mlstm_macaron_dit_block"


def rmsnorm(x, g):
    xf = x.astype(jnp.float32)
    y = xf * lax.rsqrt(jnp.mean(xf * xf, axis=-1, keepdims=True) + EPS)
    return (y * g.astype(jnp.float32)).astype(x.dtype)


def modulate(h, shift, scale):
    return h * (1 + scale) + shift


def swiglu(h, w_up, w_down):
    gate, up = jnp.split(h @ w_up, 2, axis=-1)
    return (jax.nn.silu(gate) * up) @ w_down


def centred_conv(x, w, b):
    T = x.shape[1]
    lo = CONV_W // 2
    xp = jnp.pad(x, ((0, 0), (lo, CONV_W - 1 - lo), (0, 0)))
    return b + sum(xp[:, k:k + T] * w[k] for k in range(CONV_W))


def linear_scan(a, bx, h0):
    bx = bx.at[:, 0].add(a[:, 0] * h0)

    def combine(l, r):
        al, bl = l
        ar, br = r
        return al * ar, ar * bl + br

    _, h = lax.associative_scan(combine, (a, bx), axis=1)
    return h


def rglru_dir(xc, w_r, b_r, w_i, b_i, lam, h0):
    B, T, _ = xc.shape
    xb = xc.reshape(B, T, LRU_BLOCKS, LRU_BS)
    r = jax.nn.sigmoid(jnp.einsum('btnc,ncd->btnd', xb, w_r).reshape(B, T, D_LRU) + b_r)
    i = jax.nn.sigmoid(jnp.einsum('btnc,ncd->btnd', xb, w_i).reshape(B, T, D_LRU) + b_i)
    log_a = -LRU_C * r * jax.nn.softplus(-lam.astype(jnp.float32))
    a = jnp.exp(log_a)
    mult = jnp.sqrt(-jnp.expm1(2.0 * log_a))
    return linear_scan(a, mult * i * xc, h0)


def rglru_bidir(u_lat, u_ctx, conv_w, conv_b, w_r, b_r, w_i, b_i, lam):
    f32 = jnp.float32
    xl = centred_conv(u_lat.astype(f32), conv_w, conv_b)
    xc = centred_conv(u_ctx.astype(f32), conv_w, conv_b)
    z = jnp.zeros((xc.shape[0], D_LRU), f32)

    def run(d, seq, h0):
        return rglru_dir(seq, w_r[d], b_r[d], w_i[d], b_i[d], lam[d], h0)

    hc_f = run(0, xc, z)
    hl_f = run(0, xl, hc_f[:, -1])
    hc_b = run(1, xc[:, ::-1], z)
    hl_b = run(1, xl[:, ::-1], hc_b[:, -1])
    return hl_f + hl_b[:, ::-1], hc_f + hc_b[:, ::-1]


def mlstm_dir(q, k, v, ig, fg, state0):
    B, H, T, DK = q.shape
    DV = v.shape[-1]
    nc = T // CHUNK

    def chunks(t):
        t = t.reshape((B, H, nc, CHUNK) + t.shape[3:])
        return jnp.moveaxis(t, 2, 0)

    log_f = jax.nn.log_sigmoid(fg)
    lower = jnp.tril(jnp.ones((CHUNK, CHUNK), dtype=bool))

    def step(state, xs):
        C, n, m = state
        qc, kc, vc, ic, lfc = xs
        bcum = jnp.cumsum(lfc, axis=-1)
        dmat = bcum[..., :, None] - bcum[..., None, :] + ic[..., None, :]
        dmat = jnp.where(lower, dmat, -jnp.inf)
        inter = bcum + m[..., None]
        m_row = jnp.maximum(inter, jnp.max(dmat, axis=-1))
        w = jnp.einsum('bhtk,bhsk->bhts', qc, kc) * jnp.exp(dmat - m_row[..., None])
        s_inter = jnp.exp(inter - m_row)
        num = s_inter[..., None] * jnp.einsum('bhvk,bhtk->bhtv', C, qc) + jnp.einsum('bhts,bhsv->bhtv', w, vc)
        den = s_inter * jnp.einsum('bhk,bhtk->bht', n, qc) + jnp.sum(w, axis=-1)
        h = num / jnp.maximum(jnp.abs(den), jnp.exp(-m_row))[..., None]
        b_last = bcum[..., -1]
        g = b_last[..., None] - bcum + ic
        m_new = jnp.maximum(b_last + m, jnp.max(g, axis=-1))
        decay = jnp.exp(b_last + m - m_new)
        wg = jnp.exp(g - m_new[..., None])
        C_new = decay[..., None, None] * C + jnp.einsum('bhs,bhsv,bhsk->bhvk', wg, vc, kc)
        n_new = decay[..., None] * n + jnp.einsum('bhs,bhsk->bhk', wg, kc)
        return (C_new, n_new, m_new), h

    state, hs = lax.scan(step, state0, (chunks(q), chunks(k), chunks(v), chunks(ig), chunks(log_f)))
    h = jnp.moveaxis(hs, 0, 2).reshape(B, H, T, DV)
    return h, state


def mlstm_prep(q, k, v, gt):
    f32 = jnp.float32
    B, T, _ = q.shape

    def heads(t, d):
        return t.astype(f32).reshape(B, T, M_HEADS, d).transpose(0, 2, 1, 3)

    gates = gt.astype(f32).reshape(B, T, 4, M_HEADS).transpose(2, 0, 3, 1)
    return heads(q, M_DK), heads(k, M_DK) * (M_DK ** -0.5), heads(v, M_DV), gates


def mlstm_bidir(lat, ctx_in):
    ql, kl, vl, gl = lat
    qc, kc, vc, gc = ctx_in
    B = qc.shape[0]
    f32 = jnp.float32
    zero = (jnp.zeros((B, M_HEADS, M_DV, M_DK), f32), jnp.zeros((B, M_HEADS, M_DK), f32),
            jnp.zeros((B, M_HEADS), f32))

    def fl(t):
        return jnp.flip(t, axis=2)

    hc_f, st_f = mlstm_dir(qc, kc, vc, gc[0], gc[1], zero)
    hl_f, _ = mlstm_dir(ql, kl, vl, gl[0], gl[1], st_f)
    hc_b, st_b = mlstm_dir(fl(qc), fl(kc), fl(vc), jnp.flip(gc[2], -1), jnp.flip(gc[3], -1), zero)
    hl_b, _ = mlstm_dir(fl(ql), fl(kl), fl(vl), jnp.flip(gl[2], -1), jnp.flip(gl[3], -1), st_b)
    return hl_f + fl(hl_b), hc_f + fl(hc_b)


def mlstm_headnorm(hh, gain):
    B, H, T, DV = hh.shape
    hh = hh * lax.rsqrt(jnp.mean(hh * hh, axis=-1, keepdims=True) + EPS)
    return hh.transpose(0, 2, 1, 3).reshape(B, T, H * DV) * gain.astype(jnp.float32)


def to_colmajor(t, rows):
    B, T, F = t.shape
    return t.reshape(B, rows, GRID_W, F).transpose(0, 2, 1, 3).reshape(B, T, F)


def from_colmajor(t, rows):
    B, T, F = t.shape
    return t.reshape(B, GRID_W, rows, F).transpose(0, 2, 1, 3).reshape(B, T, F)


def token_mix(h, hc, rows, is_last, w_in, b_mgate, lru_conv_w, lru_conv_b, lru_w_r, lru_b_r,
              lru_w_i, lru_b_i, lru_lam, mlstm_norm, w_out):
    lx, lg, q, k, v, o, gt = jnp.split(h @ w_in, SPLITS, axis=-1)
    lxc, lgc, qc, kc, vc, oc, gtc = jnp.split(hc @ w_in, SPLITS, axis=-1)
    f32 = jnp.float32
    rl, rc = rglru_bidir(lx, lxc, lru_conv_w, lru_conv_b, lru_w_r, lru_b_r, lru_w_i, lru_b_i, lru_lam)
    lru_lat = jax.nn.gelu(lg.astype(f32)) * rl
    cm = lambda t: to_colmajor(t, rows)
    ml, mc = mlstm_bidir(mlstm_prep(cm(q), cm(k), cm(v), cm(gt) + b_mgate),
                         mlstm_prep(qc, kc, vc, gtc + b_mgate))
    mls_lat = from_colmajor(mlstm_headnorm(ml, mlstm_norm), rows) * jax.nn.sigmoid(o.astype(f32))
    y = jnp.concatenate([lru_lat, mls_lat], axis=-1).astype(h.dtype) @ w_out
    if is_last:
        return y, None
    lru_c = jax.nn.gelu(lgc.astype(f32)) * rc
    mls_c = mlstm_headnorm(mc, mlstm_norm) * jax.nn.sigmoid(oc.astype(f32))
    yc = jnp.concatenate([lru_c, mls_c], axis=-1).astype(hc.dtype) @ w_out
    return y, yc


def layer(x, ctx, mod, mod_c, rows, is_last, ffn1_norm, ffn1_w_up, ffn1_w_down, mix_norm, w_in,
          b_mgate, lru_conv_w, lru_conv_b, lru_w_r, lru_b_r, lru_w_i, lru_b_i, lru_lam, mlstm_norm,
          w_out, ffn2_norm, ffn2_w_up, ffn2_w_down):
    sh1, sc1, g1, sh2, sc2, g2, sh3, sc3, g3 = jnp.split(mod, N_MOD, axis=-1)
    csh1, csc1, cg1, csh2, csc2, cg2, csh3, csc3, cg3 = jnp.split(mod_c, N_MOD, axis=-1)
    x = x + HALF * g1 * swiglu(modulate(rmsnorm(x, ffn1_norm), sh1, sc1), ffn1_w_up, ffn1_w_down)
    ctx = ctx + HALF * cg1 * swiglu(modulate(rmsnorm(ctx, ffn1_norm), csh1, csc1), ffn1_w_up, ffn1_w_down)
    h = modulate(rmsnorm(x, mix_norm), sh2, sc2)
    hc = modulate(rmsnorm(ctx, mix_norm), csh2, csc2)
    y, yc = token_mix(h, hc, rows, is_last, w_in, b_mgate, lru_conv_w, lru_conv_b, lru_w_r, lru_b_r,
                      lru_w_i, lru_b_i, lru_lam, mlstm_norm, w_out)
    x = x + g2 * y
    x = x + HALF * g3 * swiglu(modulate(rmsnorm(x, ffn2_norm), sh3, sc3), ffn2_w_up, ffn2_w_down)
    if not is_last:
        ctx = ctx + cg2 * yc
        ctx = ctx + HALF * cg3 * swiglu(modulate(rmsnorm(ctx, ffn2_norm), csh3, csc3), ffn2_w_up, ffn2_w_down)
    return x, ctx


def setup_inputs(seed: int = 0) -> dict:
    key = jax.random.key(seed)
    ks = jax.random.split(key, 32)
    f32 = jnp.float32

    def nrm(k, shape, s):
        return jax.random.normal(k, shape, f32) * s

    x = nrm(ks[0], (BATCH, SEQ, D_MODEL), 1.0)
    c = nrm(ks[1], (BATCH, D_MODEL), 1.0)
    ctx = nrm(ks[2], (BATCH, CTX_LEN, D_MODEL), 1.0)
    c_ctx = nrm(ks[3], (D_MODEL,), 1.0)
    ada_w = nrm(ks[4], (DEPTH, D_MODEL, N_MOD * D_MODEL), 0.5 * D_MODEL ** -0.5)
    ada_b = nrm(ks[5], (DEPTH, N_MOD * D_MODEL), 0.01)
    ffn1_norm = 1.0 + nrm(ks[6], (DEPTH, D_MODEL), 0.01)
    ffn1_w_up = nrm(ks[7], (DEPTH, D_MODEL, 2 * D_FF), D_MODEL ** -0.5)
    ffn1_w_down = nrm(ks[8], (DEPTH, D_FF, D_MODEL), D_FF ** -0.5)
    mix_norm = 1.0 + nrm(ks[9], (DEPTH, D_MODEL), 0.01)
    w_in = nrm(ks[10], (DEPTH, D_MODEL, D_IN), D_MODEL ** -0.5)
    f_off = jnp.stack([jnp.zeros((M_HEADS,), f32), jnp.linspace(3.0, 6.0, M_HEADS, dtype=f32)])
    b_mgate = (nrm(ks[11], (DEPTH, 2, 2, M_HEADS), 0.1) + f_off).reshape(DEPTH, 4 * M_HEADS)
    lru_conv_w = nrm(ks[12], (DEPTH, CONV_W, D_LRU), CONV_W ** -0.5)
    lru_conv_b = nrm(ks[13], (DEPTH, D_LRU), 0.01)
    lru_w_r = nrm(ks[14], (DEPTH, 2, LRU_BLOCKS, LRU_BS, LRU_BS), LRU_BS ** -0.5)
    lru_b_r = nrm(ks[15], (DEPTH, 2, D_LRU), 0.01)
    lru_w_i = nrm(ks[16], (DEPTH, 2, LRU_BLOCKS, LRU_BS, LRU_BS), LRU_BS ** -0.5)
    lru_b_i = nrm(ks[17], (DEPTH, 2, D_LRU), 0.01)
    a0 = jax.random.uniform(ks[18], (DEPTH, 2, D_LRU), f32, minval=0.9, maxval=0.999)
    lru_lam = jnp.log(a0) - jnp.log1p(-a0)
    mlstm_norm = 1.0 + nrm(ks[19], (DEPTH, D_MV), 0.01)
    w_out = nrm(ks[20], (DEPTH, D_MIX, D_MODEL), D_MIX ** -0.5)
    ffn2_norm = 1.0 + nrm(ks[21], (DEPTH, D_MODEL), 0.01)
    ffn2_w_up = nrm(ks[22], (DEPTH, D_MODEL, 2 * D_FF), D_MODEL ** -0.5)
    ffn2_w_down = nrm(ks[23], (DEPTH, D_FF, D_MODEL), D_FF ** -0.5)
    final_norm = 1.0 + nrm(ks[24], (D_MODEL,), 0.01)
    return {"x": x, "c": c, "ctx": ctx, "c_ctx": c_ctx, "ada_w": ada_w, "ada_b": ada_b,
            "ffn1_norm": ffn1_norm, "ffn1_w_up": ffn1_w_up, "ffn1_w_down": ffn1_w_down,
            "mix_norm": mix_norm, "w_in": w_in, "b_mgate": b_mgate, "lru_conv_w": lru_conv_w,
            "lru_conv_b": lru_conv_b, "lru_w_r": lru_w_r, "lru_b_r": lru_b_r, "lru_w_i": lru_w_i,
            "lru_b_i": lru_b_i, "lru_lam": lru_lam, "mlstm_norm": mlstm_norm, "w_out": w_out,
            "ffn2_norm": ffn2_norm, "ffn2_w_up": ffn2_w_up, "ffn2_w_down": ffn2_w_down,
            "final_norm": final_norm}


def reference(x, c, ctx, c_ctx, ada_w, ada_b, ffn1_norm, ffn1_w_up, ffn1_w_down, mix_norm, w_in,
              b_mgate, lru_conv_w, lru_conv_b, lru_w_r, lru_b_r, lru_w_i, lru_b_i, lru_lam,
              mlstm_norm, w_out, ffn2_norm, ffn2_w_up, ffn2_w_down, final_norm):
    rows = x.shape[1] // GRID_W
    sc = jax.nn.silu(c)
    sc_ctx = jax.nn.silu(c_ctx)
    for l in range(DEPTH):
        mod = (sc @ ada_w[l] + ada_b[l])[:, None, :]
        mod_c = (sc_ctx @ ada_w[l] + ada_b[l])[None, None, :]
        x, ctx = layer(x, ctx, mod, mod_c, rows, l == DEPTH - 1, ffn1_norm[l], ffn1_w_up[l],
                       ffn1_w_down[l], mix_norm[l], w_in[l], b_mgate[l], lru_conv_w[l], lru_conv_b[l],
                       lru_w_r[l], lru_b_r[l], lru_w_i[l], lru_b_i[l], lru_lam[l], mlstm_norm[l],
                       w_out[l], ffn2_norm[l], ffn2_w_up[l], ffn2_w_down[l])
    return rmsnorm(x, final_norm)
```

```python
import functools

import jax
import jax.numpy as jnp
from jax import lax
from jax.experimental import pallas as pl
from jax.experimental.pallas import tpu as pltpu

F32 = jnp.float32
BF16 = jnp.bfloat16

GRID_W = 64
LRU_BLOCKS = 8
CONV_W = 4
LRU_C = 8.0
M_HEADS = 8
CHUNK = 128
N_MOD = 9
EPS = 1e-6
HALF = 0.5

LANES = 128
SUBLANES = 8
VMEM_LIMIT_BYTES = 56 * 1024 * 1024

FFN_TM = 512
FFN_TF = 512
PROJ_TM = 512
LRU_GATE_ROWS = 512
ADA_TN = 1024


def _cparams(sem):
    return pltpu.CompilerParams(dimension_semantics=sem, vmem_limit_bytes=VMEM_LIMIT_BYTES)


def _rms(x, g):
    return x * lax.rsqrt(jnp.mean(x * x, axis=-1, keepdims=True) + EPS) * g


def _ada_body(c_ref, w_ref, b_ref, o_ref):
    s = c_ref[...]
    s = s * jax.nn.sigmoid(s)
    o_ref[...] = jnp.dot(s, w_ref[...], preferred_element_type=F32) + b_ref[...]


def _ada_call(cc, w, b):
    rows, d = cc.shape
    n = w.shape[1]
    return pl.pallas_call(
        _ada_body,
        grid=(n // ADA_TN,),
        in_specs=[pl.BlockSpec((rows, d), lambda j: (0, 0)),
                  pl.BlockSpec((d, ADA_TN), lambda j: (0, j)),
                  pl.BlockSpec((1, ADA_TN), lambda j: (0, j))],
        out_specs=pl.BlockSpec((rows, ADA_TN), lambda j: (0, j)),
        out_shape=jax.ShapeDtypeStruct((rows, n), F32),
        compiler_params=_cparams(("arbitrary",)),
        name="ada_mod",
    )(cc, w, b)


def _ffn_body(x_ref, g_ref, sh_ref, sc_ref, gate_ref, wg_ref, wu_ref, wd_ref, fin_ref, o_ref,
              h_scr, acc_scr, *, final_norm):
    j = pl.program_id(1)

    @pl.when(j == 0)
    def _():
        h = _rms(x_ref[...], g_ref[...]) * (1.0 + sc_ref[0]) + sh_ref[0]
        h_scr[...] = h.astype(BF16)
        acc_scr[...] = jnp.zeros_like(acc_scr)

    h = h_scr[...]
    g = jnp.dot(h, wg_ref[...], preferred_element_type=F32)
    u = jnp.dot(h, wu_ref[...], preferred_element_type=F32)
    a = (g * jax.nn.sigmoid(g) * u).astype(BF16)
    acc_scr[...] += jnp.dot(a, wd_ref[...], preferred_element_type=F32)

    @pl.when(j == pl.num_programs(1) - 1)
    def _():
        y = x_ref[...] + HALF * gate_ref[0] * acc_scr[...]
        if final_norm:
            y = _rms(y, fin_ref[...])
        o_ref[...] = y


def _ffn_call(x2d, norm_g, shift, scale, gate, w_up, w_down, fin_g, *, rows_per_mod, final_norm):
    m, d = x2d.shape
    f = w_down.shape[0]
    tm = min(FFN_TM, m)
    tiles_per_mod = rows_per_mod // tm
    nf = f // FFN_TF
    mod_spec = pl.BlockSpec((1, 1, d), lambda i, j: (i // tiles_per_mod, 0, 0))
    vec_spec = pl.BlockSpec((1, d), lambda i, j: (0, 0))
    return pl.pallas_call(
        functools.partial(_ffn_body, final_norm=final_norm),
        grid=(m // tm, nf),
        in_specs=[pl.BlockSpec((tm, d), lambda i, j: (i, 0)),
                  vec_spec, mod_spec, mod_spec, mod_spec,
                  pl.BlockSpec((d, FFN_TF), lambda i, j: (0, j)),
                  pl.BlockSpec((d, FFN_TF), lambda i, j: (0, j + nf)),
                  pl.BlockSpec((FFN_TF, d), lambda i, j: (j, 0)),
                  vec_spec],
        out_specs=pl.BlockSpec((tm, d), lambda i, j: (i, 0)),
        out_shape=jax.ShapeDtypeStruct((m, d), F32),
        scratch_shapes=[pltpu.VMEM((tm, d), BF16), pltpu.VMEM((tm, d), F32)],
        compiler_params=_cparams(("parallel", "arbitrary")),
        name="ffn_final" if final_norm else "ffn",
    )(x2d, norm_g, shift, scale, gate, w_up, w_up, w_down, fin_g)


def _proj_body(x_ref, g_ref, sh_ref, sc_ref, w_ref, *refs, n_sub, widths):
    out_refs, h_scr = refs[:-1], refs[-1]
    r = x_ref.shape[1]
    d = g_ref.shape[1]
    for s in range(n_sub):
        h = _rms(x_ref[0, :, s * d:(s + 1) * d], g_ref[...]) * (1.0 + sc_ref[0]) + sh_ref[0]
        h_scr[s * r:(s + 1) * r, :] = h.astype(BF16)
    h = h_scr[...]
    off = 0
    for o_ref, wdt in zip(out_refs, widths):
        o_ref[0] = jnp.dot(h, w_ref[:, off:off + wdt], preferred_element_type=F32).astype(o_ref.dtype)
        off += wdt


def _proj_call(x3d, norm_g, shift, scale, w, widths, *, n_sub, tm, name):
    b, r_total, cols = x3d.shape
    d = norm_g.shape[1]
    r = tm // n_sub
    n_row_tiles = r_total // r
    n_col_tiles = cols // (n_sub * d)
    assert n_row_tiles == 1 or n_col_tiles == 1
    n_tiles = n_row_tiles * n_col_tiles
    t_out = n_tiles * tm
    if n_col_tiles == 1:
        x_map = lambda bi, i: (bi, i, 0)
    else:
        x_map = lambda bi, i: (bi, 0, i)
    mod_spec = pl.BlockSpec((1, 1, d), lambda bi, i: (bi, 0, 0))
    return pl.pallas_call(
        functools.partial(_proj_body, n_sub=n_sub, widths=tuple(widths)),
        grid=(b, n_tiles),
        in_specs=[pl.BlockSpec((1, r, n_sub * d), x_map),
                  pl.BlockSpec((1, d), lambda bi, i: (0, 0)),
                  mod_spec, mod_spec,
                  pl.BlockSpec(w.shape, lambda bi, i: (0, 0))],
        out_specs=[pl.BlockSpec((1, tm, wdt), lambda bi, i: (bi, i, 0)) for wdt in widths],
        out_shape=[jax.ShapeDtypeStruct((b, t_out, wdt), F32) for wdt in widths],
        scratch_shapes=[pltpu.VMEM((tm, d), BF16)],
        compiler_params=_cparams(("parallel", "parallel")),
        name=name,
    )(x3d, norm_g, shift, scale, w)


def _lru_conv(x_ref, cw_ref, cb_ref, pad_scr, xc_scr, t):
    zeros = jnp.zeros((SUBLANES, LANES), F32)
    pad_scr[0:SUBLANES, :] = zeros
    pad_scr[SUBLANES:SUBLANES + t, :] = x_ref[0]
    pad_scr[SUBLANES + t:2 * SUBLANES + t, :] = zeros
    ch = min(LRU_GATE_ROWS, t)
    win_rows = ch + 2 * SUBLANES

    def body(c, carry):
        off = pl.multiple_of(c * ch, SUBLANES)
        win = pad_scr[pl.ds(off, win_rows), :]
        acc = cb_ref[...] + cw_ref[2:3, :] * win[SUBLANES:SUBLANES + ch]
        for k, shift in ((0, 2), (1, 1), (3, win_rows - 1)):
            acc = acc + cw_ref[k:k + 1, :] * pltpu.roll(win, shift, 0)[SUBLANES:SUBLANES + ch]
        xc_scr[pl.ds(off, ch), :] = acc
        return carry

    lax.fori_loop(0, t // ch, body, 0)


def _lru_gates(xc_scr, t, d, wr_ref, br_ref, wi_ref, bi_ref, lam_ref, a_scr, b_scr):
    ch = min(LRU_GATE_ROWS, t)
    log_a_unit = -LRU_C * jax.nn.softplus(-lam_ref[d:d + 1, :])
    wr = wr_ref[d, 0].astype(BF16)
    wi = wi_ref[d, 0].astype(BF16)

    def body(c, carry):
        off = pl.multiple_of(c * ch, SUBLANES)
        xc = xc_scr[pl.ds(off, ch), :]
        xb = xc.astype(BF16)
        r = jax.nn.sigmoid(jnp.dot(xb, wr, preferred_element_type=F32) + br_ref[d:d + 1, :])
        i = jax.nn.sigmoid(jnp.dot(xb, wi, preferred_element_type=F32) + bi_ref[d:d + 1, :])
        log_a = log_a_unit * r
        a = jnp.exp(log_a)
        a_scr[pl.ds(off, ch), :] = a
        b_scr[pl.ds(off, ch), :] = jnp.sqrt(-jnp.tanh(log_a) * (a * a + 1.0)) * i * xc
        return carry

    lax.fori_loop(0, t // ch, body, 0)


def _lru_scan(a_scr, b_scr, t, d, h0, h_scr):
    groups = t // SUBLANES
    row = lax.broadcasted_iota(jnp.int32, (SUBLANES, LANES), 0)

    def body(g, carry):
        gi = g if d == 0 else groups - 1 - g
        off = pl.multiple_of(gi * SUBLANES, SUBLANES)
        a = a_scr[pl.ds(off, SUBLANES), :]
        bv = b_scr[pl.ds(off, SUBLANES), :]
        for k in (1, 2, 4):
            shift = k if d == 0 else SUBLANES - k
            valid = (row >= k) if d == 0 else (row < SUBLANES - k)
            a_prev = pltpu.roll(a, shift, 0)
            b_prev = pltpu.roll(bv, shift, 0)
            bv = jnp.where(valid, a * b_prev + bv, bv)
            a = jnp.where(valid, a * a_prev, a)
        h = a * carry + bv
        if h_scr is not None:
            if d == 0:
                h_scr[pl.ds(off, SUBLANES), :] = h
            else:
                h_scr[pl.ds(off, SUBLANES), :] += h
        return h[SUBLANES - 1:SUBLANES, :] if d == 0 else h[0:1, :]

    return lax.fori_loop(0, groups, body, h0, unroll=4)


def _lru_body(lx_ref, lg_ref, lxc_ref, cw_ref, cb_ref, wr_ref, br_ref, wi_ref, bi_ref, lam_ref, o_ref,
              pad_scr, xc_scr, xcc_scr, a_scr, b_scr, h_scr):
    t = lx_ref.shape[1]
    tc = lxc_ref.shape[1]
    _lru_conv(lxc_ref, cw_ref, cb_ref, pad_scr, xcc_scr, tc)
    _lru_conv(lx_ref, cw_ref, cb_ref, pad_scr, xc_scr, t)
    gate_refs = (wr_ref, br_ref, wi_ref, bi_ref, lam_ref)
    for d in (0, 1):
        _lru_gates(xcc_scr, tc, d, *gate_refs, a_scr, b_scr)
        h0 = _lru_scan(a_scr, b_scr, tc, d, jnp.zeros((1, LANES), F32), None)
        _lru_gates(xc_scr, t, d, *gate_refs, a_scr, b_scr)
        _lru_scan(a_scr, b_scr, t, d, h0, h_scr)
    o_ref[0] = (jax.nn.gelu(lg_ref[0]) * h_scr[...]).astype(o_ref.dtype)


def _lru_call(lx, lg, lxc, conv_w, conv_b, w_r, b_r, w_i, b_i, lam):
    b, t, d_lru = lx.shape
    tc = lxc.shape[1]
    nb = d_lru // LANES
    assert w_r.shape == (2, nb, LANES, LANES)
    seq = lambda n: pl.BlockSpec((1, n, LANES), lambda bi, j: (bi, 0, j))
    vec = lambda n: pl.BlockSpec((n, LANES), lambda bi, j: (0, j))
    wblk = pl.BlockSpec((2, 1, LANES, LANES), lambda bi, j: (0, j, 0, 0))
    return pl.pallas_call(
        _lru_body,
        grid=(b, nb),
        in_specs=[seq(t), seq(t), seq(tc), vec(CONV_W), vec(1), wblk, vec(2), wblk, vec(2), vec(2)],
        out_specs=seq(t),
        out_shape=jax.ShapeDtypeStruct((b, t, d_lru), BF16),
        scratch_shapes=[pltpu.VMEM((t + 2 * SUBLANES, LANES), F32),
                        pltpu.VMEM((t, LANES), F32), pltpu.VMEM((tc, LANES), F32),
                        pltpu.VMEM((t, LANES), F32), pltpu.VMEM((t, LANES), F32),
                        pltpu.VMEM((t, LANES), F32)],
        compiler_params=_cparams(("parallel", "parallel")),
        name="rglru",
    )(lx, lg, lxc, conv_w, conv_b, w_r, b_r, w_i, b_i, lam)


def _split3_dot(tri, x):
    hi = x.astype(BF16)
    r1 = x - hi.astype(F32)
    mid = r1.astype(BF16)
    lo = (r1 - mid.astype(F32)).astype(BF16)
    return (jnp.dot(tri, hi, preferred_element_type=F32) + jnp.dot(tri, mid, preferred_element_type=F32)
            + jnp.dot(tri, lo, preferred_element_type=F32))


def _mlstm_chunk(q, k, v, gates, d, s_scr, m_scr, need_h, dk):
    L = q.shape[0]
    ti = lax.broadcasted_iota(jnp.int32, (L, L), 0)
    si = lax.broadcasted_iota(jnp.int32, (L, L), 1)
    causal = (si <= ti) if d == 0 else (si >= ti)
    tri = causal.astype(BF16)
    lane_q = lax.broadcasted_iota(jnp.int32, (L, LANES), 1)
    krow = lax.broadcasted_iota(jnp.int32, (LANES, L), 0)
    lane_v = lax.broadcasted_iota(jnp.int32, (L, LANES), 1)

    bc = _split3_dot(tri, jax.nn.log_sigmoid(gates))
    gates_t = gates.T
    bc_t = bc.T
    tot = bc[L - 1:L, :] if d == 0 else bc[0:1, :]
    k_t = k.T * (dk ** -0.5)

    hs = []
    for e in (0, 1):
        ci = 16 * d + e
        cf = 16 * d + 8 + e
        bcum_col = bc[:, cf:cf + 1]
        bcum_row = bc_t[cf:cf + 1, :]
        ic_row = gates_t[ci:ci + 1, :]
        ic_col = gates[:, ci:ci + 1]
        b_last = tot[:, cf:cf + 1]
        m = m_scr[e][0:1, 0:1]
        s_old = s_scr[e]
        k_te = jnp.where((krow >= e * dk) & (krow < (e + 1) * dk), k_t, 0.0).astype(BF16)
        v_e = v[:, e * LANES:(e + 1) * LANES]
        if need_h:
            q_e = jnp.where((lane_q >= e * dk) & (lane_q < (e + 1) * dk), q, 0.0).astype(BF16)
            dmat = jnp.where(causal, bcum_col - bcum_row + ic_row, -jnp.inf)
            inter = bcum_col + m
            m_row = jnp.maximum(inter, jnp.max(dmat, axis=-1, keepdims=True))
            w = jnp.dot(q_e, k_te, preferred_element_type=F32) * jnp.exp(dmat - m_row)
            s_inter = jnp.exp(inter - m_row)
            qs = jnp.dot(q_e, s_old.astype(BF16), preferred_element_type=F32)
            num = s_inter * qs[:, :LANES] + jnp.dot(w.astype(BF16), v_e.astype(BF16),
                                                    preferred_element_type=F32)
            den = s_inter * qs[:, LANES:LANES + 1] + jnp.sum(w, axis=-1, keepdims=True)
            hs.append(num / jnp.maximum(jnp.abs(den), jnp.exp(-m_row)))
        g_row = b_last - bcum_row + ic_row
        g_col = b_last - bcum_col + ic_col
        m_new = jnp.maximum(b_last + m, jnp.max(g_row, axis=-1, keepdims=True))
        decay = jnp.exp(b_last + m - m_new)
        wg = jnp.exp(g_col - m_new)
        wv = jnp.concatenate([wg * v_e, jnp.where(lane_v == 0, wg, 0.0)], axis=1).astype(BF16)
        s_scr[e] = decay * s_old + jnp.dot(k_te, wv, preferred_element_type=F32)
        m_scr[e] = jnp.broadcast_to(m_new, (SUBLANES, LANES))
    return hs


def _mlstm_body(q_ref, k_ref, v_ref, o_ref, gt_ref, qc_ref, kc_ref, vc_ref, gtc_ref, bias_ref, gain_ref,
                out_ref, hs_scr, s_scr, m_scr, *, dk):
    t = q_ref.shape[1]
    tc = qc_ref.shape[1]
    L = CHUNK
    nc, ncc = t // L, tc // L
    shift = (LANES - 2 * pl.program_id(1)) % LANES
    bias = pltpu.roll(jnp.broadcast_to(bias_ref[...], (SUBLANES, LANES)), shift, 1)[0:1, :]

    def gates_of(ref, off):
        return pltpu.roll(ref[0, pl.ds(off, L), :], shift, 1) + bias

    for d in (0, 1):
        s_scr[...] = jnp.zeros_like(s_scr)
        m_scr[...] = jnp.zeros_like(m_scr)
        for c in range(ncc):
            off = (c if d == 0 else ncc - 1 - c) * L
            _mlstm_chunk(qc_ref[0, off:off + L, :], kc_ref[0, off:off + L, :], vc_ref[0, off:off + L, :],
                         gates_of(gtc_ref, off), d, s_scr, m_scr, False, dk)

        def body(c, carry):
            ci = c if d == 0 else nc - 1 - c
            off = pl.multiple_of(ci * L, L)
            rows = pl.ds(off, L)
            hs = _mlstm_chunk(q_ref[0, rows, :], k_ref[0, rows, :], v_ref[0, rows, :],
                              gates_of(gt_ref, off), d, s_scr, m_scr, True, dk)
            h = jnp.concatenate(hs, axis=1)
            if d == 0:
                hs_scr[rows, :] = h
            else:
                hs_scr[rows, :] += h
            return carry

        lax.fori_loop(0, nc, body, 0)

    def fin(c, carry):
        rows = pl.ds(pl.multiple_of(c * L, L), L)
        hh = hs_scr[rows, :]
        outs = []
        for e in (0, 1):
            x = hh[:, e * LANES:(e + 1) * LANES]
            outs.append(x * lax.rsqrt(jnp.mean(x * x, axis=-1, keepdims=True) + EPS))
        y = jnp.concatenate(outs, axis=1) * gain_ref[...] * jax.nn.sigmoid(o_ref[0, rows, :])
        out_ref[0, rows, :] = y.astype(out_ref.dtype)
        return carry

    lax.fori_loop(0, nc, fin, 0)


def _mlstm_call(q, k, v, o, gt, qc, kc, vc, gtc, bias, gain):
    b, t, d_qk = q.shape
    tc = qc.shape[1]
    d_v = v.shape[2]
    pairs = M_HEADS // 2
    dk = d_qk // M_HEADS
    assert d_qk // pairs == LANES and d_v // pairs == 2 * LANES
    seq = lambda n, w: pl.BlockSpec((1, n, w), lambda bi, p: (bi, 0, p))
    allg = lambda n: pl.BlockSpec((1, n, LANES), lambda bi, p: (bi, 0, 0))
    return pl.pallas_call(
        functools.partial(_mlstm_body, dk=dk),
        grid=(b, pairs),
        in_specs=[seq(t, LANES), seq(t, LANES), seq(t, 2 * LANES), seq(t, 2 * LANES), allg(t),
                  seq(tc, LANES), seq(tc, LANES), seq(tc, 2 * LANES), allg(tc),
                  pl.BlockSpec((1, LANES), lambda bi, p: (0, 0)),
                  pl.BlockSpec((1, 2 * LANES), lambda bi, p: (0, p))],
        out_specs=seq(t, 2 * LANES),
        out_shape=jax.ShapeDtypeStruct((b, t, d_v), BF16),
        scratch_shapes=[pltpu.VMEM((t, 2 * LANES), F32),
                        pltpu.VMEM((2, LANES, 2 * LANES), F32),
                        pltpu.VMEM((2, SUBLANES, LANES), F32)],
        compiler_params=_cparams(("parallel", "parallel")),
        name="mlstm",
    )(q, k, v, o, gt, qc, kc, vc, gtc, bias, gain)


def _outproj_body(x_ref, lru_ref, mls_ref, wa_ref, wb_ref, g_ref, o_ref, m_scr, *, n_sub):
    r = mls_ref.shape[1]
    dm = m_scr.shape[1]
    for s in range(n_sub):
        m_scr[s * r:(s + 1) * r, :] = mls_ref[0, :, s * dm:(s + 1) * dm]
    y = (jnp.dot(lru_ref[0], wa_ref[...], preferred_element_type=F32)
         + jnp.dot(m_scr[...], wb_ref[...], preferred_element_type=F32))
    o_ref[0] = x_ref[0] + g_ref[0] * y


def _outproj_call(x3d, lru, mls_cm, w_a, w_b, gate):
    b, t, d = x3d.shape
    da, dm = lru.shape[2], mls_cm.shape[2]
    rows = t // GRID_W
    n_sub = PROJ_TM // GRID_W
    mls_view = mls_cm.reshape(b, GRID_W, rows * dm)
    return pl.pallas_call(
        functools.partial(_outproj_body, n_sub=n_sub),
        grid=(b, t // PROJ_TM),
        in_specs=[pl.BlockSpec((1, PROJ_TM, d), lambda bi, i: (bi, i, 0)),
                  pl.BlockSpec((1, PROJ_TM, da), lambda bi, i: (bi, i, 0)),
                  pl.BlockSpec((1, GRID_W, n_sub * dm), lambda bi, i: (bi, 0, i)),
                  pl.BlockSpec((da, d), lambda bi, i: (0, 0)),
                  pl.BlockSpec((dm, d), lambda bi, i: (0, 0)),
                  pl.BlockSpec((1, 1, d), lambda bi, i: (bi, 0, 0))],
        out_specs=pl.BlockSpec((1, PROJ_TM, d), lambda bi, i: (bi, i, 0)),
        out_shape=jax.ShapeDtypeStruct((b, t, d), F32),
        scratch_shapes=[pltpu.VMEM((PROJ_TM, dm), BF16)],
        compiler_params=_cparams(("parallel", "parallel")),
        name="outproj",
    )(x3d, lru, mls_view, w_a, w_b, gate)


def kernel(x, c, ctx, c_ctx, ada_w, ada_b, ffn1_norm, ffn1_w_up, ffn1_w_down, mix_norm, w_in, b_mgate, lru_conv_w, lru_conv_b, lru_w_r, lru_b_r, lru_w_i, lru_b_i, lru_lam, mlstm_norm, w_out, ffn2_norm, ffn2_w_up, ffn2_w_down, final_norm):
    b, t, d = x.shape
    tc = ctx.shape[1]
    assert ada_w.shape[0] == 1, "single-layer block only"
    assert t % GRID_W == 0 and PROJ_TM % (t // GRID_W) == 0 and t % PROJ_TM == 0
    rows = t // GRID_W
    d_lru = lru_conv_w.shape[2]
    d_mv = mlstm_norm.shape[1]
    d_mqk = (w_in.shape[2] - 2 * d_lru - 2 * d_mv - 4 * M_HEADS) // 2

    pad = SUBLANES - b - 1
    cc = jnp.concatenate([c, c_ctx[None, :], jnp.zeros((pad, d), F32)], axis=0)
    mod = _ada_call(cc, ada_w[0], ada_b[0][None, :]).reshape(SUBLANES, N_MOD, d)
    lat = lambda i: mod[:b, i][:, None, :]
    cxt = lambda i: mod[b:b + 1, i][:, None, :]
    row = lambda v: v[0][None, :]

    up1, down1 = ffn1_w_up[0].astype(BF16), ffn1_w_down[0].astype(BF16)
    x1 = _ffn_call(x.reshape(b * t, d), row(ffn1_norm), lat(0), lat(1), lat(2), up1, down1,
                   row(ffn1_norm), rows_per_mod=t, final_norm=False)
    ctx1 = _ffn_call(ctx.reshape(b * tc, d), row(ffn1_norm), cxt(0), cxt(1), cxt(2), up1, down1,
                     row(ffn1_norm), rows_per_mod=b * tc, final_norm=False)
    x1 = x1.reshape(b, t, d)
    ctx1 = ctx1.reshape(b, tc, d)

    w_in0 = w_in[0]
    w_a = w_in0[:, :2 * d_lru].astype(BF16)
    n_gate = 4 * M_HEADS
    w_b = jnp.concatenate([w_in0[:, 2 * d_lru:], jnp.zeros((d, LANES - n_gate), F32)], axis=1).astype(BF16)
    widths_a = (d_lru, d_lru)
    widths_b = (d_mqk, d_mqk, d_mv, d_mv, LANES)
    mixn = row(mix_norm)
    lx, lg = _proj_call(x1, mixn, lat(3), lat(4), w_a, widths_a, n_sub=1, tm=PROJ_TM, name="proj_lru")
    n_sub = PROJ_TM // rows
    q, k, v, o, gt = _proj_call(x1.reshape(b, rows, GRID_W * d), mixn, lat(3), lat(4), w_b, widths_b,
                                n_sub=n_sub, tm=PROJ_TM, name="proj_mlstm")
    ctx_mod = lambda i: jnp.broadcast_to(cxt(i), (b, 1, d))
    lxc, _ = _proj_call(ctx1, mixn, ctx_mod(3), ctx_mod(4), w_a, widths_a, n_sub=1, tm=tc,
                        name="proj_lru_ctx")
    qc, kc, vc, _, gtc = _proj_call(ctx1, mixn, ctx_mod(3), ctx_mod(4), w_b, widths_b, n_sub=1, tm=tc,
                                    name="proj_mlstm_ctx")

    lru_lat = _lru_call(lx, lg, lxc, lru_conv_w[0], row(lru_conv_b), lru_w_r[0], lru_b_r[0], lru_w_i[0],
                        lru_b_i[0], lru_lam[0])
    bias = jnp.concatenate([b_mgate[0], jnp.zeros((LANES - n_gate,), F32)])[None, :]
    mls_cm = _mlstm_call(q, k, v, o, gt, qc, kc, vc, gtc, bias, row(mlstm_norm))

    w_out0 = w_out[0].astype(BF16)
    x2 = _outproj_call(x1, lru_lat, mls_cm, w_out0[:d_lru], w_out0[d_lru:], lat(5))

    up2, down2 = ffn2_w_up[0].astype(BF16), ffn2_w_down[0].astype(BF16)
    out = _ffn_call(x2.reshape(b * t, d), row(ffn2_norm), lat(6), lat(7), lat(8), up2, down2,
                    final_norm[None, :], rows_per_mod=t, final_norm=True)
    return out.reshape(b, t, d)
```

```python
import functools

import jax
import jax.numpy as jnp
from jax import lax
from jax.experimental import pallas as pl
from jax.experimental.pallas import tpu as pltpu

F32 = jnp.float32
BF16 = jnp.bfloat16

GRID_W = 64
LRU_BLOCKS = 8
CONV_W = 4
LRU_C = 8.0
M_HEADS = 8
CHUNK = 128
N_MOD = 9
EPS = 1e-6
HALF = 0.5

LANES = 128
SUBLANES = 8
VMEM_LIMIT_BYTES = 56 * 1024 * 1024

FFN_TM = 512
FFN_TF = 512
PROJ_TM = 512
LRU_GATE_ROWS = 512
ADA_TN = 1024


def _cparams(sem):
    return pltpu.CompilerParams(dimension_semantics=sem, vmem_limit_bytes=VMEM_LIMIT_BYTES)


def _sigmoid(z):
    return 0.5 * jnp.tanh(0.5 * z) + 0.5


def _rms(x, g):
    return x * lax.rsqrt(jnp.mean(x * x, axis=-1, keepdims=True) + EPS) * g


def _ada_body(c_ref, w_ref, b_ref, o_ref):
    s = c_ref[...]
    s = s * jax.nn.sigmoid(s)
    o_ref[...] = jnp.dot(s, w_ref[...], preferred_element_type=F32) + b_ref[...]


def _ada_call(cc, w, b):
    rows, d = cc.shape
    n = w.shape[1]
    return pl.pallas_call(
        _ada_body,
        grid=(n // ADA_TN,),
        in_specs=[pl.BlockSpec((rows, d), lambda j: (0, 0)),
                  pl.BlockSpec((d, ADA_TN), lambda j: (0, j)),
                  pl.BlockSpec((1, ADA_TN), lambda j: (0, j))],
        out_specs=pl.BlockSpec((rows, ADA_TN), lambda j: (0, j)),
        out_shape=jax.ShapeDtypeStruct((rows, n), F32),
        compiler_params=_cparams(("arbitrary",)),
        name="ada_mod",
    )(cc, w, b)


def _ffn_body(x_ref, g_ref, sh_ref, sc_ref, gate_ref, wg_ref, wu_ref, wd_ref, fin_ref, o_ref,
              h_scr, acc_scr, *, final_norm):
    j = pl.program_id(1)

    @pl.when(j == 0)
    def _():
        h = _rms(x_ref[...], g_ref[...]) * (1.0 + sc_ref[0]) + sh_ref[0]
        h_scr[...] = h.astype(BF16)
        acc_scr[...] = jnp.zeros_like(acc_scr)

    h = h_scr[...]
    g = jnp.dot(h, wg_ref[...], preferred_element_type=F32)
    u = jnp.dot(h, wu_ref[...], preferred_element_type=F32)
    a = (g * jax.nn.sigmoid(g) * u).astype(BF16)
    acc_scr[...] += jnp.dot(a, wd_ref[...], preferred_element_type=F32)

    @pl.when(j == pl.num_programs(1) - 1)
    def _():
        y = x_ref[...] + HALF * gate_ref[0] * acc_scr[...]
        if final_norm:
            y = _rms(y, fin_ref[...])
        o_ref[...] = y


def _ffn_call(x2d, norm_g, shift, scale, gate, w_up, w_down, fin_g, *, rows_per_mod, final_norm, name):
    m, d = x2d.shape
    f = w_down.shape[0]
    tm = min(FFN_TM, m)
    tiles_per_mod = rows_per_mod // tm
    nf = f // FFN_TF
    mod_spec = pl.BlockSpec((1, 1, d), lambda i, j: (i // tiles_per_mod, 0, 0))
    vec_spec = pl.BlockSpec((1, d), lambda i, j: (0, 0))
    return pl.pallas_call(
        functools.partial(_ffn_body, final_norm=final_norm),
        grid=(m // tm, nf),
        in_specs=[pl.BlockSpec((tm, d), lambda i, j: (i, 0)),
                  vec_spec, mod_spec, mod_spec, mod_spec,
                  pl.BlockSpec((d, FFN_TF), lambda i, j: (0, j)),
                  pl.BlockSpec((d, FFN_TF), lambda i, j: (0, j + nf)),
                  pl.BlockSpec((FFN_TF, d), lambda i, j: (j, 0)),
                  vec_spec],
        out_specs=pl.BlockSpec((tm, d), lambda i, j: (i, 0)),
        out_shape=jax.ShapeDtypeStruct((m, d), F32),
        scratch_shapes=[pltpu.VMEM((tm, d), BF16), pltpu.VMEM((tm, d), F32)],
        compiler_params=_cparams(("parallel", "arbitrary")),
        name=name,
    )(x2d, norm_g, shift, scale, gate, w_up, w_up, w_down, fin_g)


def _colmajor_perm(n_rows, n_cols):
    assert n_rows & (n_rows - 1) == 0
    m = n_rows * n_cols
    j = lax.broadcasted_iota(jnp.int32, (m, m), 0)
    i = lax.broadcasted_iota(jnp.int32, (m, m), 1)
    src = (j & (n_rows - 1)) * n_cols + (j >> (n_rows.bit_length() - 1))
    return (i == src).astype(BF16)


def _proj_body(x_ref, g_ref, sh_ref, sc_ref, w_ref, *out_refs, widths, colmajor):
    if colmajor:
        r, c, d = x_ref.shape[1:]
        x = x_ref[0].reshape(r * c, d)
    else:
        x = x_ref[0]
    h = (_rms(x, g_ref[...]) * (1.0 + sc_ref[0]) + sh_ref[0]).astype(BF16)
    if colmajor:
        h = jnp.dot(_colmajor_perm(r, c), h, preferred_element_type=F32).astype(BF16)
    off = 0
    for o_ref, wdt in zip(out_refs, widths):
        o_ref[0] = jnp.dot(h, w_ref[:, off:off + wdt], preferred_element_type=F32).astype(o_ref.dtype)
        off += wdt


def _proj_call(x, norm_g, shift, scale, w, widths, dtypes, *, tm, colmajor, name):
    b = x.shape[0]
    d = norm_g.shape[1]
    if colmajor:
        rows, gw = x.shape[1:3]
        cols = tm // rows
        assert cols == SUBLANES and gw % cols == 0
        n_tiles = gw // cols
        x_spec = pl.BlockSpec((1, rows, cols, d), lambda bi, i: (bi, 0, i, 0))
    else:
        n_tiles = x.shape[1] // tm
        x_spec = pl.BlockSpec((1, tm, d), lambda bi, i: (bi, i, 0))
    mod_spec = pl.BlockSpec((1, 1, d), lambda bi, i: (bi, 0, 0))
    return pl.pallas_call(
        functools.partial(_proj_body, widths=tuple(widths), colmajor=colmajor),
        grid=(b, n_tiles),
        in_specs=[x_spec,
                  pl.BlockSpec((1, d), lambda bi, i: (0, 0)),
                  mod_spec, mod_spec,
                  pl.BlockSpec(w.shape, lambda bi, i: (0, 0))],
        out_specs=[pl.BlockSpec((1, tm, wdt), lambda bi, i: (bi, i, 0)) for wdt in widths],
        out_shape=[jax.ShapeDtypeStruct((b, n_tiles * tm, wdt), dt) for wdt, dt in zip(widths, dtypes)],
        compiler_params=_cparams(("parallel", "parallel")),
        name=name,
    )(x, norm_g, shift, scale, w)


def _lru_conv(x_ref, cw_ref, cb_ref, pad_scr, xc_scr, t):
    zeros = jnp.zeros((SUBLANES, LANES), F32)
    pad_scr[0:SUBLANES, :] = zeros
    pad_scr[SUBLANES:SUBLANES + t, :] = x_ref[0]
    pad_scr[SUBLANES + t:2 * SUBLANES + t, :] = zeros
    ch = min(LRU_GATE_ROWS, t)
    win_rows = ch + 2 * SUBLANES

    def body(c, carry):
        off = pl.multiple_of(c * ch, SUBLANES)
        win = pad_scr[pl.ds(off, win_rows), :]
        acc = cb_ref[...] + cw_ref[2:3, :] * win[SUBLANES:SUBLANES + ch]
        for k, shift in ((0, 2), (1, 1), (3, win_rows - 1)):
            acc = acc + cw_ref[k:k + 1, :] * pltpu.roll(win, shift, 0)[SUBLANES:SUBLANES + ch]
        xc_scr[pl.ds(off, ch), :] = acc
        return carry

    lax.fori_loop(0, t // ch, body, 0)


def _lru_gates(xc_scr, t, d, wr_ref, br_ref, wi_ref, bi_ref, lam_ref, a_scr, b_scr):
    ch = min(LRU_GATE_ROWS, t)
    log_a_unit = -LRU_C * jax.nn.softplus(-lam_ref[d:d + 1, :])
    wr = wr_ref[d, 0].astype(BF16)
    wi = wi_ref[d, 0].astype(BF16)

    def body(c, carry):
        off = pl.multiple_of(c * ch, SUBLANES)
        xc = xc_scr[pl.ds(off, ch), :]
        xb = xc.astype(BF16)
        r = _sigmoid(jnp.dot(xb, wr, preferred_element_type=F32) + br_ref[d:d + 1, :])
        i = _sigmoid(jnp.dot(xb, wi, preferred_element_type=F32) + bi_ref[d:d + 1, :])
        log_a = log_a_unit * r
        a = jnp.exp(log_a)
        a_scr[pl.ds(off, ch), :] = a
        b_scr[pl.ds(off, ch), :] = jnp.sqrt(-jnp.tanh(log_a) * (a * a + 1.0)) * i * xc
        return carry

    lax.fori_loop(0, t // ch, body, 0)


def _lru_scan(a_scr, b_scr, t, d, h0, h_scr):
    groups = t // SUBLANES
    row = lax.broadcasted_iota(jnp.int32, (SUBLANES, LANES), 0)

    def body(g, carry):
        gi = g if d == 0 else groups - 1 - g
        off = pl.multiple_of(gi * SUBLANES, SUBLANES)
        a = a_scr[pl.ds(off, SUBLANES), :]
        bv = b_scr[pl.ds(off, SUBLANES), :]
        for k in (1, 2, 4):
            shift = k if d == 0 else SUBLANES - k
            valid = (row >= k) if d == 0 else (row < SUBLANES - k)
            a_prev = pltpu.roll(a, shift, 0)
            b_prev = pltpu.roll(bv, shift, 0)
            bv = jnp.where(valid, a * b_prev + bv, bv)
            a = jnp.where(valid, a * a_prev, a)
        h = a * carry + bv
        if h_scr is not None:
            if d == 0:
                h_scr[pl.ds(off, SUBLANES), :] = h
            else:
                h_scr[pl.ds(off, SUBLANES), :] += h
        return h[SUBLANES - 1:SUBLANES, :] if d == 0 else h[0:1, :]

    return lax.fori_loop(0, groups, body, h0, unroll=4)


def _lru_body(lx_ref, lg_ref, lxc_ref, cw_ref, cb_ref, wr_ref, br_ref, wi_ref, bi_ref, lam_ref, o_ref,
              pad_scr, xc_scr, xcc_scr, a_scr, b_scr, h_scr):
    t = lx_ref.shape[1]
    tc = lxc_ref.shape[1]
    _lru_conv(lxc_ref, cw_ref, cb_ref, pad_scr, xcc_scr, tc)
    _lru_conv(lx_ref, cw_ref, cb_ref, pad_scr, xc_scr, t)
    gate_refs = (wr_ref, br_ref, wi_ref, bi_ref, lam_ref)
    for d in (0, 1):
        _lru_gates(xcc_scr, tc, d, *gate_refs, a_scr, b_scr)
        h0 = _lru_scan(a_scr, b_scr, tc, d, jnp.zeros((1, LANES), F32), None)
        _lru_gates(xc_scr, t, d, *gate_refs, a_scr, b_scr)
        _lru_scan(a_scr, b_scr, t, d, h0, h_scr)
    o_ref[0] = (jax.nn.gelu(lg_ref[0]) * h_scr[...]).astype(o_ref.dtype)


def _lru_call(lx, lg, lxc, conv_w, conv_b, w_r, b_r, w_i, b_i, lam):
    b, t, d_lru = lx.shape
    tc = lxc.shape[1]
    nb = d_lru // LANES
    assert w_r.shape == (2, nb, LANES, LANES)
    seq = lambda n: pl.BlockSpec((1, n, LANES), lambda bi, j: (bi, 0, j))
    vec = lambda n: pl.BlockSpec((n, LANES), lambda bi, j: (0, j))
    wblk = pl.BlockSpec((2, 1, LANES, LANES), lambda bi, j: (0, j, 0, 0))
    return pl.pallas_call(
        _lru_body,
        grid=(b, nb),
        in_specs=[seq(t), seq(t), seq(tc), vec(CONV_W), vec(1), wblk, vec(2), wblk, vec(2), vec(2)],
        out_specs=seq(t),
        out_shape=jax.ShapeDtypeStruct((b, t, d_lru), BF16),
        scratch_shapes=[pltpu.VMEM((t + 2 * SUBLANES, LANES), F32),
                        pltpu.VMEM((t, LANES), F32), pltpu.VMEM((tc, LANES), F32),
                        pltpu.VMEM((t, LANES), F32), pltpu.VMEM((t, LANES), F32),
                        pltpu.VMEM((t, LANES), F32)],
        compiler_params=_cparams(("parallel", "parallel")),
        name="rglru",
    )(lx, lg, lxc, conv_w, conv_b, w_r, b_r, w_i, b_i, lam)


FWD_GATE_LANES = 2 * M_HEADS


def _split3_dot(tri, x):
    hi = x.astype(BF16)
    r1 = x - hi.astype(F32)
    mid = r1.astype(BF16)
    lo = (r1 - mid.astype(F32)).astype(BF16)
    return (jnp.dot(tri, hi, preferred_element_type=F32) + jnp.dot(tri, mid, preferred_element_type=F32)
            + jnp.dot(tri, lo, preferred_element_type=F32))


def _cummax_rows(x, row, reverse):
    n = x.shape[0]
    k = 1
    while k < n:
        if reverse:
            x = jnp.where(row < n - k, jnp.maximum(x, pltpu.roll(x, n - k, 0)), x)
        else:
            x = jnp.where(row >= k, jnp.maximum(x, pltpu.roll(x, k, 0)), x)
        k *= 2
    return x


def _round_up_bf16(x):
    return (x + jnp.abs(x) * (2.0 ** -7)).astype(BF16)


def _mlstm_gate_prep(gates, kidx, bc_scr, cm_scr, xt_scr, tot_scr, gmax_scr):
    L = gates.shape[0]
    ti = lax.broadcasted_iota(jnp.int32, (L, L), 0)
    si = lax.broadcasted_iota(jnp.int32, (L, L), 1)
    row = lax.broadcasted_iota(jnp.int32, (L, LANES), 0)
    fwd = lax.broadcasted_iota(jnp.int32, (L, LANES), 1) < FWD_GATE_LANES
    lf = jax.nn.log_sigmoid(gates)
    bc_f = _split3_dot((si <= ti).astype(BF16), lf)
    bc_b = _split3_dot((si >= ti).astype(BF16), lf)
    bc = jnp.where(fwd, bc_f, bc_b)
    tot = jnp.where(fwd[0:1], bc_f[L - 1:L], bc_b[0:1])
    x = pltpu.roll(gates, SUBLANES, 1) - bc
    cm = jnp.where(fwd, _cummax_rows(x, row, False), _cummax_rows(x, row, True))
    gmax = jnp.max(tot + x, axis=0, keepdims=True)
    xt = x.T
    bc_scr[kidx] = bc
    cm_scr[kidx] = cm
    xt_scr[kidx, 0:SUBLANES, :] = xt[SUBLANES:2 * SUBLANES]
    xt_scr[kidx, SUBLANES:2 * SUBLANES, :] = xt[3 * SUBLANES:4 * SUBLANES]
    tot_scr[kidx] = jnp.broadcast_to(tot, (SUBLANES, LANES))
    gmax_scr[kidx] = jnp.broadcast_to(gmax, (SUBLANES, LANES))


def _mlstm_chunk(q, k, v, kidx, step, d, scr, need_h, dk):
    bc_scr, cm_scr, xt_scr, tot_scr, m_in_scr, m_out_scr, s_scr = scr
    L = q.shape[0]
    ti = lax.broadcasted_iota(jnp.int32, (L, L), 0)
    si = lax.broadcasted_iota(jnp.int32, (L, L), 1)
    causal = (si <= ti) if d == 0 else (si >= ti)
    lane = lax.broadcasted_iota(jnp.int32, (L, LANES), 1)
    krow = lax.broadcasted_iota(jnp.int32, (LANES, L), 0)
    k_t = k.astype(F32).T * (dk ** -0.5)
    tot_v, m_in_v, m_out_v = tot_scr[kidx], m_in_scr[step], m_out_scr[step]
    ones = jnp.ones((L, LANES), BF16)
    cf0 = FWD_GATE_LANES * d + SUBLANES
    if need_h:
        bc = bc_scr[kidx]
        sel_lane = lax.broadcasted_iota(jnp.int32, (LANES, 2 * LANES), 0)
        sel_col = lax.broadcasted_iota(jnp.int32, (LANES, 2 * LANES), 1)
        sel = (sel_lane == jnp.where(sel_col < LANES, cf0, cf0 + 1)).astype(BF16)
        mx_all = _round_up_bf16(jnp.maximum(m_in_v[0:1, :], cm_scr[kidx]))
        mx_tiles = jnp.dot(mx_all, sel, preferred_element_type=F32)
    hs = []
    for e in (0, 1):
        cf = cf0 + e
        tot = tot_v[0:1, cf:cf + 1]
        m_in = m_in_v[0:1, cf:cf + 1]
        m_out = m_out_v[0:1, cf:cf + 1]
        x_row = xt_scr[kidx, SUBLANES * d + e:SUBLANES * d + e + 1, :]
        s_old = s_scr[2 * d + e]
        k_te = jnp.where((krow >= e * dk) & (krow < (e + 1) * dk), k_t, 0.0)
        v1 = jnp.concatenate([v[:, e * LANES:(e + 1) * LANES], ones], axis=1)
        if need_h:
            mx = mx_tiles[:, e * LANES:(e + 1) * LANES]
            q_e = jnp.where((lane >= e * dk) & (lane < (e + 1) * dk), q, jnp.zeros_like(q))
            w = (jnp.dot(q_e, k_te.astype(BF16), preferred_element_type=F32)
                 * jnp.exp(jnp.where(causal, x_row - mx, -jnp.inf)))
            s_inter = jnp.exp(m_in - mx)
            qs = jnp.dot(q_e, s_old.astype(BF16), preferred_element_type=F32)
            wv = jnp.dot(w.astype(BF16), v1, preferred_element_type=F32)
            num = s_inter * qs[:, :LANES] + wv[:, :LANES]
            den = s_inter * qs[:, LANES:] + wv[:, LANES:]
            m_row = bc[:, cf:cf + 1] + mx[:, cf:cf + 1]
            hs.append(num * (1.0 / jnp.maximum(jnp.abs(den[:, cf:cf + 1]), jnp.exp(-m_row))))
        wg_row = jnp.exp(tot + x_row - m_out)
        decay = jnp.exp(tot + m_in - m_out)
        s_scr[2 * d + e] = decay * s_old + jnp.dot((k_te * wg_row).astype(BF16), v1,
                                                   preferred_element_type=F32)
    return hs


def _mlstm_body(q_ref, k_ref, v_ref, o_ref, gt_ref, qc_ref, kc_ref, vc_ref, gtc_ref, bias_ref, gain_ref,
                out_ref, hs_scr, bc_scr, cm_scr, xt_scr, tot_scr, gmax_scr, m_in_scr, m_out_scr,
                s_scr, *, dk):
    t = q_ref.shape[1]
    tc = qc_ref.shape[1]
    L = CHUNK
    nc, ncc = t // L, tc // L
    n_steps = nc + ncc
    shift = (LANES - 2 * pl.program_id(1)) % LANES
    bias = pltpu.roll(jnp.broadcast_to(bias_ref[...], (SUBLANES, LANES)), shift, 1)[0:1, :]
    prep_scr = (bc_scr, cm_scr, xt_scr, tot_scr, gmax_scr)
    state_scr = (bc_scr, cm_scr, xt_scr, tot_scr, m_in_scr, m_out_scr, s_scr)

    for c in range(ncc):
        _mlstm_gate_prep(pltpu.roll(gtc_ref[0, c * L:(c + 1) * L, :], shift, 1) + bias, c, *prep_scr)

    def prep(c, carry):
        rows = pl.ds(pl.multiple_of(c * L, L), L)
        _mlstm_gate_prep(pltpu.roll(gt_ref[0, rows, :], shift, 1) + bias, c + ncc, *prep_scr)
        return carry

    lax.fori_loop(0, nc, prep, 0)

    fwd_id = lambda i: i
    bwd_id = lambda i: (ncc - 1 - i) if i < ncc else (n_steps - 1 - (i - ncc))
    fwd8 = lax.broadcasted_iota(jnp.int32, (SUBLANES, LANES), 1) < FWD_GATE_LANES
    m = jnp.zeros((SUBLANES, LANES), F32)
    for i in range(n_steps):
        tot = jnp.where(fwd8, tot_scr[fwd_id(i)], tot_scr[bwd_id(i)])
        gmax = jnp.where(fwd8, gmax_scr[fwd_id(i)], gmax_scr[bwd_id(i)])
        m_in_scr[i] = m
        m = jnp.maximum(tot + m, gmax)
        m_out_scr[i] = m

    s_scr[...] = jnp.zeros_like(s_scr)
    hs_scr[...] = jnp.zeros_like(hs_scr)
    for i in range(ncc):
        for d, cid in ((0, fwd_id(i)), (1, bwd_id(i))):
            rows = slice(cid * L, (cid + 1) * L)
            _mlstm_chunk(qc_ref[0, rows, :], kc_ref[0, rows, :], vc_ref[0, rows, :], cid, i, d,
                         state_scr, False, dk)

    def body(j, carry):
        for d in (0, 1):
            cj = j if d == 0 else nc - 1 - j
            rows = pl.ds(pl.multiple_of(cj * L, L), L)
            hs = _mlstm_chunk(q_ref[0, rows, :], k_ref[0, rows, :], v_ref[0, rows, :], cj + ncc, j + ncc, d,
                              state_scr, True, dk)
            hs_scr[rows, :] += jnp.concatenate(hs, axis=1)
        return carry

    lax.fori_loop(0, nc, body, 0)

    def fin(c, carry):
        rows = pl.ds(pl.multiple_of(c * L, L), L)
        hh = hs_scr[rows, :]
        outs = []
        for e in (0, 1):
            x = hh[:, e * LANES:(e + 1) * LANES]
            outs.append(x * lax.rsqrt(jnp.mean(x * x, axis=-1, keepdims=True) + EPS))
        y = jnp.concatenate(outs, axis=1) * gain_ref[...] * jax.nn.sigmoid(o_ref[0, rows, :])
        out_ref[0, rows, :] = y.astype(out_ref.dtype)
        return carry

    lax.fori_loop(0, nc, fin, 0)


def _mlstm_call(q, k, v, o, gt, qc, kc, vc, gtc, bias, gain):
    b, t, d_qk = q.shape
    tc = qc.shape[1]
    d_v = v.shape[2]
    pairs = M_HEADS // 2
    dk = d_qk // M_HEADS
    assert d_qk // pairs == LANES and d_v // pairs == 2 * LANES and CHUNK == LANES
    n_chunks = (t + tc) // CHUNK
    seq = lambda n, w: pl.BlockSpec((1, n, w), lambda bi, p: (bi, 0, p))
    allg = lambda n: pl.BlockSpec((1, n, LANES), lambda bi, p: (bi, 0, 0))
    chunk_f32 = lambda rows, cols: pltpu.VMEM((n_chunks, rows, cols), F32)
    return pl.pallas_call(
        functools.partial(_mlstm_body, dk=dk),
        grid=(b, pairs),
        in_specs=[seq(t, LANES), seq(t, LANES), seq(t, 2 * LANES), seq(t, 2 * LANES), allg(t),
                  seq(tc, LANES), seq(tc, LANES), seq(tc, 2 * LANES), allg(tc),
                  pl.BlockSpec((1, LANES), lambda bi, p: (0, 0)),
                  pl.BlockSpec((1, 2 * LANES), lambda bi, p: (0, p))],
        out_specs=seq(t, 2 * LANES),
        out_shape=jax.ShapeDtypeStruct((b, t, d_v), F32),
        scratch_shapes=[pltpu.VMEM((t, 2 * LANES), F32),
                        chunk_f32(CHUNK, LANES), chunk_f32(CHUNK, LANES),
                        chunk_f32(2 * SUBLANES, CHUNK),
                        chunk_f32(SUBLANES, LANES), chunk_f32(SUBLANES, LANES),
                        chunk_f32(SUBLANES, LANES), chunk_f32(SUBLANES, LANES),
                        pltpu.VMEM((4, LANES, 2 * LANES), F32)],
        compiler_params=_cparams(("parallel", "parallel")),
        name="mlstm",
    )(q, k, v, o, gt, qc, kc, vc, gtc, bias, gain)


def _outproj_body(x_ref, lru_ref, mls_ref, wa_ref, wb_ref, g_ref, o_ref):
    w_cols, r, dm = mls_ref.shape[1:]
    m = mls_ref[0].reshape(w_cols * r, dm).astype(BF16)
    m = jnp.dot(_colmajor_perm(w_cols, r), m, preferred_element_type=F32).astype(BF16)
    y = (jnp.dot(lru_ref[0], wa_ref[...], preferred_element_type=F32)
         + jnp.dot(m, wb_ref[...], preferred_element_type=F32))
    o_ref[0] = x_ref[0] + g_ref[0] * y


def _outproj_call(x3d, lru, mls_cm, w_a, w_b, gate):
    b, t, d = x3d.shape
    da, dm = lru.shape[2], mls_cm.shape[2]
    rows = t // GRID_W
    r_tile = PROJ_TM // GRID_W
    assert r_tile == SUBLANES
    mls_view = mls_cm.reshape(b, GRID_W, rows, dm)
    return pl.pallas_call(
        _outproj_body,
        grid=(b, t // PROJ_TM),
        in_specs=[pl.BlockSpec((1, PROJ_TM, d), lambda bi, i: (bi, i, 0)),
                  pl.BlockSpec((1, PROJ_TM, da), lambda bi, i: (bi, i, 0)),
                  pl.BlockSpec((1, GRID_W, r_tile, dm), lambda bi, i: (bi, 0, i, 0)),
                  pl.BlockSpec((da, d), lambda bi, i: (0, 0)),
                  pl.BlockSpec((dm, d), lambda bi, i: (0, 0)),
                  pl.BlockSpec((1, 1, d), lambda bi, i: (bi, 0, 0))],
        out_specs=pl.BlockSpec((1, PROJ_TM, d), lambda bi, i: (bi, i, 0)),
        out_shape=jax.ShapeDtypeStruct((b, t, d), F32),
        compiler_params=_cparams(("parallel", "parallel")),
        name="outproj",
    )(x3d, lru, mls_view, w_a, w_b, gate)


def kernel(x, c, ctx, c_ctx, ada_w, ada_b, ffn1_norm, ffn1_w_up, ffn1_w_down, mix_norm, w_in, b_mgate, lru_conv_w, lru_conv_b, lru_w_r, lru_b_r, lru_w_i, lru_b_i, lru_lam, mlstm_norm, w_out, ffn2_norm, ffn2_w_up, ffn2_w_down, final_norm):
    b, t, d = x.shape
    tc = ctx.shape[1]
    assert ada_w.shape[0] == 1, "single-layer block only"
    assert t % GRID_W == 0 and t % PROJ_TM == 0
    rows = t // GRID_W
    d_lru = lru_conv_w.shape[2]
    d_mv = mlstm_norm.shape[1]
    d_mqk = (w_in.shape[2] - 2 * d_lru - 2 * d_mv - 4 * M_HEADS) // 2

    pad = SUBLANES - b - 1
    cc = jnp.concatenate([c, c_ctx[None, :], jnp.zeros((pad, d), F32)], axis=0)
    mod = _ada_call(cc, ada_w[0], ada_b[0][None, :]).reshape(SUBLANES, N_MOD, d)
    lat = lambda i: mod[:b, i][:, None, :]
    cxt = lambda i: mod[b:b + 1, i][:, None, :]
    row = lambda v: v[0][None, :]

    up1, down1 = ffn1_w_up[0].astype(BF16), ffn1_w_down[0].astype(BF16)
    x1 = _ffn_call(x.reshape(b * t, d), row(ffn1_norm), lat(0), lat(1), lat(2), up1, down1,
                   row(ffn1_norm), rows_per_mod=t, final_norm=False, name="ffn_pre")
    ctx1 = _ffn_call(ctx.reshape(b * tc, d), row(ffn1_norm), cxt(0), cxt(1), cxt(2), up1, down1,
                     row(ffn1_norm), rows_per_mod=b * tc, final_norm=False, name="ffn_pre_ctx")
    x1 = x1.reshape(b, t, d)
    ctx1 = ctx1.reshape(b, tc, d)

    w_in0 = w_in[0]
    w_a = w_in0[:, :2 * d_lru].astype(BF16)
    n_gate = 4 * M_HEADS
    w_b = jnp.concatenate([w_in0[:, 2 * d_lru:], jnp.zeros((d, LANES - n_gate), F32)], axis=1).astype(BF16)
    widths_a, dtypes_a = (d_lru, d_lru), (F32, F32)
    widths_b, dtypes_b = (d_mqk, d_mqk, d_mv, d_mv, LANES), (BF16, BF16, BF16, F32, F32)
    mixn = row(mix_norm)
    lx, lg = _proj_call(x1, mixn, lat(3), lat(4), w_a, widths_a, dtypes_a, tm=PROJ_TM, colmajor=False,
                        name="proj_lru")
    q, k, v, o, gt = _proj_call(x1.reshape(b, rows, GRID_W, d), mixn, lat(3), lat(4), w_b, widths_b,
                                dtypes_b, tm=PROJ_TM, colmajor=True, name="proj_mlstm")
    ctx_mod = lambda i: jnp.broadcast_to(cxt(i), (b, 1, d))
    lxc, _ = _proj_call(ctx1, mixn, ctx_mod(3), ctx_mod(4), w_a, widths_a, dtypes_a, tm=tc, colmajor=False,
                        name="proj_lru_ctx")
    qc, kc, vc, _, gtc = _proj_call(ctx1, mixn, ctx_mod(3), ctx_mod(4), w_b, widths_b, dtypes_b, tm=tc,
                                    colmajor=False, name="proj_mlstm_ctx")

    lru_lat = _lru_call(lx, lg, lxc, lru_conv_w[0], row(lru_conv_b), lru_w_r[0], lru_b_r[0], lru_w_i[0],
                        lru_b_i[0], lru_lam[0])
    bias = jnp.concatenate([b_mgate[0], jnp.zeros((LANES - n_gate,), F32)])[None, :]
    mls_cm = _mlstm_call(q, k, v, o, gt, qc, kc, vc, gtc, bias, row(mlstm_norm))

    w_out0 = w_out[0].astype(BF16)
    x2 = _outproj_call(x1, lru_lat, mls_cm, w_out0[:d_lru], w_out0[d_lru:], lat(5))

    up2, down2 = ffn2_w_up[0].astype(BF16), ffn2_w_down[0].astype(BF16)
    out = _ffn_call(x2.reshape(b * t, d), row(ffn2_norm), lat(6), lat(7), lat(8), up2, down2,
                    final_norm[None, :], rows_per_mod=t, final_norm=True, name="ffn_post")
    return out.reshape(b, t, d)
```

```python
import functools

import jax
import jax.numpy as jnp
from jax import lax
from jax.experimental import pallas as pl
from jax.experimental.pallas import tpu as pltpu

F32 = jnp.float32
BF16 = jnp.bfloat16

GRID_W = 64
LRU_BLOCKS = 8
CONV_W = 4
LRU_C = 8.0
M_HEADS = 8
CHUNK = 128
N_MOD = 9
EPS = 1e-6
HALF = 0.5

LANES = 128
SUBLANES = 8
VMEM_LIMIT_BYTES = 58 * 1024 * 1024

FFN_TM = 1024
FFN_TF = 256
PROJ_TM = 512
LRU_GATE_ROWS = 512
ADA_TN = 1024


def _cparams(sem):
    return pltpu.CompilerParams(dimension_semantics=sem, vmem_limit_bytes=VMEM_LIMIT_BYTES)


def _sigmoid(z):
    return 0.5 * jnp.tanh(0.5 * z) + 0.5


def _rms(x, g):
    return x * lax.rsqrt(jnp.mean(x * x, axis=-1, keepdims=True) + EPS) * g


def _ada_body(c_ref, w_ref, b_ref, o_ref):
    s = c_ref[...]
    s = s * jax.nn.sigmoid(s)
    o_ref[...] = jnp.dot(s, w_ref[...], preferred_element_type=F32) + b_ref[...]


def _ada_call(cc, w, b):
    rows, d = cc.shape
    n = w.shape[1]
    return pl.pallas_call(
        _ada_body,
        grid=(n // ADA_TN,),
        in_specs=[pl.BlockSpec((rows, d), lambda j: (0, 0)),
                  pl.BlockSpec((d, ADA_TN), lambda j: (0, j)),
                  pl.BlockSpec((1, ADA_TN), lambda j: (0, j))],
        out_specs=pl.BlockSpec((rows, ADA_TN), lambda j: (0, j)),
        out_shape=jax.ShapeDtypeStruct((rows, n), F32),
        compiler_params=_cparams(("arbitrary",)),
        name="ada_mod",
    )(cc, w, b)


def _ffn_body(x_ref, g_ref, sh_ref, sc_ref, gate_ref, wg_ref, wu_ref, wd_ref, fin_ref, o_ref,
              h_scr, *, final_norm):
    j = pl.program_id(1)

    @pl.when(j == 0)
    def _():
        h = _rms(x_ref[...], g_ref[...]) * (1.0 + sc_ref[0]) + sh_ref[0]
        h_scr[...] = h.astype(BF16)
        o_ref[...] = jnp.zeros_like(o_ref)

    h = h_scr[...]
    g = jnp.dot(h, wg_ref[...].astype(BF16), preferred_element_type=F32)
    u = jnp.dot(h, wu_ref[...].astype(BF16), preferred_element_type=F32)
    a = (g * jax.nn.sigmoid(g) * u).astype(BF16)
    o_ref[...] += jnp.dot(a, wd_ref[...].astype(BF16), preferred_element_type=F32)

    @pl.when(j == pl.num_programs(1) - 1)
    def _():
        y = x_ref[...] + HALF * gate_ref[0] * o_ref[...]
        if final_norm:
            y = _rms(y, fin_ref[...])
        o_ref[...] = y


def _ffn_call(x2d, norm_g, shift, scale, gate, w_up, w_down, fin_g, *, rows_per_mod, final_norm, name):
    m, d = x2d.shape
    f = w_down.shape[0]
    tm = min(FFN_TM, m)
    tiles_per_mod = rows_per_mod // tm
    nf = f // FFN_TF
    mod_spec = pl.BlockSpec((1, 1, d), lambda i, j: (i // tiles_per_mod, 0, 0))
    vec_spec = pl.BlockSpec((1, d), lambda i, j: (0, 0))
    return pl.pallas_call(
        functools.partial(_ffn_body, final_norm=final_norm),
        grid=(m // tm, nf),
        in_specs=[pl.BlockSpec((tm, d), lambda i, j: (i, 0)),
                  vec_spec, mod_spec, mod_spec, mod_spec,
                  pl.BlockSpec((d, FFN_TF), lambda i, j: (0, j)),
                  pl.BlockSpec((d, FFN_TF), lambda i, j: (0, j + nf)),
                  pl.BlockSpec((FFN_TF, d), lambda i, j: (j, 0)),
                  vec_spec],
        out_specs=pl.BlockSpec((tm, d), lambda i, j: (i, 0)),
        out_shape=jax.ShapeDtypeStruct((m, d), F32),
        scratch_shapes=[pltpu.VMEM((tm, d), BF16)],
        compiler_params=_cparams(("parallel", "arbitrary")),
        name=name,
    )(x2d, norm_g, shift, scale, gate, w_up, w_up, w_down, fin_g)


def _colmajor_perm(n_rows, n_cols):
    assert n_rows & (n_rows - 1) == 0
    m = n_rows * n_cols
    j = lax.broadcasted_iota(jnp.int32, (m, m), 0)
    i = lax.broadcasted_iota(jnp.int32, (m, m), 1)
    src = (j & (n_rows - 1)) * n_cols + (j >> (n_rows.bit_length() - 1))
    return (i == src).astype(BF16)


def _proj_body(x_ref, g_ref, sh_ref, sc_ref, w_ref, *out_refs, widths, colmajor):
    if colmajor:
        r, c, d = x_ref.shape[1:]
        x = x_ref[0].reshape(r * c, d)
    else:
        x = x_ref[0]
    h = (_rms(x, g_ref[...]) * (1.0 + sc_ref[0]) + sh_ref[0]).astype(BF16)
    if colmajor:
        h = jnp.dot(_colmajor_perm(r, c), h, preferred_element_type=F32).astype(BF16)
    off = 0
    for o_ref, wdt in zip(out_refs, widths):
        o_ref[0] = jnp.dot(h, w_ref[:, off:off + wdt], preferred_element_type=F32).astype(o_ref.dtype)
        off += wdt


def _proj_call(x, norm_g, shift, scale, w, widths, dtypes, *, tm, colmajor, name):
    b = x.shape[0]
    d = norm_g.shape[1]
    if colmajor:
        rows, gw = x.shape[1:3]
        cols = tm // rows
        assert cols == SUBLANES and gw % cols == 0
        n_tiles = gw // cols
        x_spec = pl.BlockSpec((1, rows, cols, d), lambda bi, i: (bi, 0, i, 0))
    else:
        n_tiles = x.shape[1] // tm
        x_spec = pl.BlockSpec((1, tm, d), lambda bi, i: (bi, i, 0))
    mod_spec = pl.BlockSpec((1, 1, d), lambda bi, i: (bi, 0, 0))
    return pl.pallas_call(
        functools.partial(_proj_body, widths=tuple(widths), colmajor=colmajor),
        grid=(b, n_tiles),
        in_specs=[x_spec,
                  pl.BlockSpec((1, d), lambda bi, i: (0, 0)),
                  mod_spec, mod_spec,
                  pl.BlockSpec(w.shape, lambda bi, i: (0, 0))],
        out_specs=[pl.BlockSpec((1, tm, wdt), lambda bi, i: (bi, i, 0)) for wdt in widths],
        out_shape=[jax.ShapeDtypeStruct((b, n_tiles * tm, wdt), dt) for wdt, dt in zip(widths, dtypes)],
        compiler_params=_cparams(("parallel", "parallel")),
        name=name,
    )(x, norm_g, shift, scale, w)


def _lru_conv(x_ref, cw_ref, cb_ref, pad_scr, xc_scr, t):
    zeros = jnp.zeros((SUBLANES, LANES), F32)
    pad_scr[0:SUBLANES, :] = zeros
    pad_scr[SUBLANES:SUBLANES + t, :] = x_ref[0]
    pad_scr[SUBLANES + t:2 * SUBLANES + t, :] = zeros
    ch = min(LRU_GATE_ROWS, t)
    win_rows = ch + 2 * SUBLANES

    def body(c, carry):
        off = pl.multiple_of(c * ch, SUBLANES)
        win = pad_scr[pl.ds(off, win_rows), :]
        acc = cb_ref[...] + cw_ref[2:3, :] * win[SUBLANES:SUBLANES + ch]
        for k, shift in ((0, 2), (1, 1), (3, win_rows - 1)):
            acc = acc + cw_ref[k:k + 1, :] * pltpu.roll(win, shift, 0)[SUBLANES:SUBLANES + ch]
        xc_scr[pl.ds(off, ch), :] = acc
        return carry

    lax.fori_loop(0, t // ch, body, 0)


def _lru_gates(xc_scr, t, d, wr_ref, br_ref, wi_ref, bi_ref, lam_ref, a_scr, b_scr):
    ch = min(LRU_GATE_ROWS, t)
    log_a_unit = -LRU_C * jax.nn.softplus(-lam_ref[d:d + 1, :])
    wr = wr_ref[d, 0].astype(BF16)
    wi = wi_ref[d, 0].astype(BF16)

    def body(c, carry):
        off = pl.multiple_of(c * ch, SUBLANES)
        xc = xc_scr[pl.ds(off, ch), :]
        xb = xc.astype(BF16)
        r = _sigmoid(jnp.dot(xb, wr, preferred_element_type=F32) + br_ref[d:d + 1, :])
        i = _sigmoid(jnp.dot(xb, wi, preferred_element_type=F32) + bi_ref[d:d + 1, :])
        log_a = log_a_unit * r
        a = jnp.exp(log_a)
        a_scr[pl.ds(off, ch), :] = a
        b_scr[pl.ds(off, ch), :] = jnp.sqrt(-jnp.tanh(log_a) * (a * a + 1.0)) * i * xc
        return carry

    lax.fori_loop(0, t // ch, body, 0)


def _lru_scan(a_scr, b_scr, t, d, h0, h_scr):
    groups = t // SUBLANES
    row = lax.broadcasted_iota(jnp.int32, (SUBLANES, LANES), 0)

    def body(g, carry):
        gi = g if d == 0 else groups - 1 - g
        off = pl.multiple_of(gi * SUBLANES, SUBLANES)
        a = a_scr[pl.ds(off, SUBLANES), :]
        bv = b_scr[pl.ds(off, SUBLANES), :]
        for k in (1, 2, 4):
            shift = k if d == 0 else SUBLANES - k
            valid = (row >= k) if d == 0 else (row < SUBLANES - k)
            a_prev = pltpu.roll(a, shift, 0)
            b_prev = pltpu.roll(bv, shift, 0)
            bv = jnp.where(valid, a * b_prev + bv, bv)
            a = jnp.where(valid, a * a_prev, a)
        h = a * carry + bv
        if h_scr is not None:
            if d == 0:
                h_scr[pl.ds(off, SUBLANES), :] = h
            else:
                h_scr[pl.ds(off, SUBLANES), :] += h
        return h[SUBLANES - 1:SUBLANES, :] if d == 0 else h[0:1, :]

    return lax.fori_loop(0, groups, body, h0, unroll=4)


def _lru_body(lx_ref, lg_ref, lxc_ref, cw_ref, cb_ref, wr_ref, br_ref, wi_ref, bi_ref, lam_ref, o_ref,
              pad_scr, xc_scr, xcc_scr, a_scr, b_scr, h_scr):
    t = lx_ref.shape[1]
    tc = lxc_ref.shape[1]
    _lru_conv(lxc_ref, cw_ref, cb_ref, pad_scr, xcc_scr, tc)
    _lru_conv(lx_ref, cw_ref, cb_ref, pad_scr, xc_scr, t)
    gate_refs = (wr_ref, br_ref, wi_ref, bi_ref, lam_ref)
    for d in (0, 1):
        _lru_gates(xcc_scr, tc, d, *gate_refs, a_scr, b_scr)
        h0 = _lru_scan(a_scr, b_scr, tc, d, jnp.zeros((1, LANES), F32), None)
        _lru_gates(xc_scr, t, d, *gate_refs, a_scr, b_scr)
        _lru_scan(a_scr, b_scr, t, d, h0, h_scr)
    o_ref[0] = (jax.nn.gelu(lg_ref[0]) * h_scr[...]).astype(o_ref.dtype)


def _lru_call(lx, lg, lxc, conv_w, conv_b, w_r, b_r, w_i, b_i, lam):
    b, t, d_lru = lx.shape
    tc = lxc.shape[1]
    nb = d_lru // LANES
    assert w_r.shape == (2, nb, LANES, LANES)
    seq = lambda n: pl.BlockSpec((1, n, LANES), lambda bi, j: (bi, 0, j))
    vec = lambda n: pl.BlockSpec((n, LANES), lambda bi, j: (0, j))
    wblk = pl.BlockSpec((2, 1, LANES, LANES), lambda bi, j: (0, j, 0, 0))
    return pl.pallas_call(
        _lru_body,
        grid=(b, nb),
        in_specs=[seq(t), seq(t), seq(tc), vec(CONV_W), vec(1), wblk, vec(2), wblk, vec(2), vec(2)],
        out_specs=seq(t),
        out_shape=jax.ShapeDtypeStruct((b, t, d_lru), BF16),
        scratch_shapes=[pltpu.VMEM((t + 2 * SUBLANES, LANES), F32),
                        pltpu.VMEM((t, LANES), F32), pltpu.VMEM((tc, LANES), F32),
                        pltpu.VMEM((t, LANES), F32), pltpu.VMEM((t, LANES), F32),
                        pltpu.VMEM((t, LANES), F32)],
        compiler_params=_cparams(("parallel", "parallel")),
        name="rglru",
    )(lx, lg, lxc, conv_w, conv_b, w_r, b_r, w_i, b_i, lam)


FWD_GATE_LANES = 2 * M_HEADS


def _split3_dot(tri, x):
    hi = x.astype(BF16)
    r1 = x - hi.astype(F32)
    mid = r1.astype(BF16)
    lo = (r1 - mid.astype(F32)).astype(BF16)
    return (jnp.dot(tri, hi, preferred_element_type=F32) + jnp.dot(tri, mid, preferred_element_type=F32)
            + jnp.dot(tri, lo, preferred_element_type=F32))


def _cummax_rows(x, row, reverse):
    n = x.shape[0]
    k = 1
    while k < n:
        if reverse:
            x = jnp.where(row < n - k, jnp.maximum(x, pltpu.roll(x, n - k, 0)), x)
        else:
            x = jnp.where(row >= k, jnp.maximum(x, pltpu.roll(x, k, 0)), x)
        k *= 2
    return x


def _round_up_bf16(x):
    return (x + jnp.abs(x) * (2.0 ** -7)).astype(BF16)


def _mlstm_gate_prep(gates, kidx, bc_scr, cm_scr, xt_scr, tot_scr, gmax_scr):
    L = gates.shape[0]
    ti = lax.broadcasted_iota(jnp.int32, (L, L), 0)
    si = lax.broadcasted_iota(jnp.int32, (L, L), 1)
    row = lax.broadcasted_iota(jnp.int32, (L, LANES), 0)
    fwd = lax.broadcasted_iota(jnp.int32, (L, LANES), 1) < FWD_GATE_LANES
    lf = jax.nn.log_sigmoid(gates)
    bc_f = _split3_dot((si <= ti).astype(BF16), lf)
    bc_b = _split3_dot((si >= ti).astype(BF16), lf)
    bc = jnp.where(fwd, bc_f, bc_b)
    tot = jnp.where(fwd[0:1], bc_f[L - 1:L], bc_b[0:1])
    x = pltpu.roll(gates, SUBLANES, 1) - bc
    cm = jnp.where(fwd, _cummax_rows(x, row, False), _cummax_rows(x, row, True))
    gmax = jnp.max(tot + x, axis=0, keepdims=True)
    xt = x.T
    bc_scr[kidx] = bc
    cm_scr[kidx] = cm
    xt_scr[kidx, 0:SUBLANES, :] = xt[SUBLANES:2 * SUBLANES]
    xt_scr[kidx, SUBLANES:2 * SUBLANES, :] = xt[3 * SUBLANES:4 * SUBLANES]
    tot_scr[kidx] = jnp.broadcast_to(tot, (SUBLANES, LANES))
    gmax_scr[kidx] = jnp.broadcast_to(gmax, (SUBLANES, LANES))


def _mlstm_chunk(q, k, v, kidx, step, d, scr, need_h, dk):
    bc_scr, cm_scr, xt_scr, tot_scr, m_in_scr, m_out_scr, s_scr = scr
    L = q.shape[0]
    ti = lax.broadcasted_iota(jnp.int32, (L, L), 0)
    si = lax.broadcasted_iota(jnp.int32, (L, L), 1)
    causal = (si <= ti) if d == 0 else (si >= ti)
    lane = lax.broadcasted_iota(jnp.int32, (L, LANES), 1)
    krow = lax.broadcasted_iota(jnp.int32, (LANES, L), 0)
    k_t = k.astype(F32).T * (dk ** -0.5)
    tot_v, m_in_v, m_out_v = tot_scr[kidx], m_in_scr[step], m_out_scr[step]
    ones = jnp.ones((L, LANES), BF16)
    cf0 = FWD_GATE_LANES * d + SUBLANES
    if need_h:
        bc = bc_scr[kidx]
        sel_lane = lax.broadcasted_iota(jnp.int32, (LANES, 2 * LANES), 0)
        sel_col = lax.broadcasted_iota(jnp.int32, (LANES, 2 * LANES), 1)
        sel = (sel_lane == jnp.where(sel_col < LANES, cf0, cf0 + 1)).astype(BF16)
        mx_all = _round_up_bf16(jnp.maximum(m_in_v[0:1, :], cm_scr[kidx]))
        mx_tiles = jnp.dot(mx_all, sel, preferred_element_type=F32)
    hs = []
    for e in (0, 1):
        cf = cf0 + e
        tot = tot_v[0:1, cf:cf + 1]
        m_in = m_in_v[0:1, cf:cf + 1]
        m_out = m_out_v[0:1, cf:cf + 1]
        x_row = xt_scr[kidx, SUBLANES * d + e:SUBLANES * d + e + 1, :]
        s_old = s_scr[2 * d + e]
        k_te = jnp.where((krow >= e * dk) & (krow < (e + 1) * dk), k_t, 0.0)
        v1 = jnp.concatenate([v[:, e * LANES:(e + 1) * LANES], ones], axis=1)
        if need_h:
            mx = mx_tiles[:, e * LANES:(e + 1) * LANES]
            q_e = jnp.where((lane >= e * dk) & (lane < (e + 1) * dk), q, jnp.zeros_like(q))
            w = (jnp.dot(q_e, k_te.astype(BF16), preferred_element_type=F32)
                 * jnp.exp(jnp.where(causal, x_row - mx, -jnp.inf)))
            s_inter = jnp.exp(m_in - mx)
            qs = jnp.dot(q_e, s_old.astype(BF16), preferred_element_type=F32)
            wv = jnp.dot(w.astype(BF16), v1, preferred_element_type=F32)
            num = s_inter * qs[:, :LANES] + wv[:, :LANES]
            den = s_inter * qs[:, LANES:] + wv[:, LANES:]
            m_row = bc[:, cf:cf + 1] + mx[:, cf:cf + 1]
            hs.append(num * (1.0 / jnp.maximum(jnp.abs(den[:, cf:cf + 1]), jnp.exp(-m_row))))
        wg_row = jnp.exp(tot + x_row - m_out)
        decay = jnp.exp(tot + m_in - m_out)
        s_scr[2 * d + e] = decay * s_old + jnp.dot((k_te * wg_row).astype(BF16), v1,
                                                   preferred_element_type=F32)
    return hs


def _mlstm_body(q_ref, k_ref, v_ref, o_ref, gt_ref, qc_ref, kc_ref, vc_ref, gtc_ref, bias_ref, gain_ref,
                out_ref, hs_scr, bc_scr, cm_scr, xt_scr, tot_scr, gmax_scr, m_in_scr, m_out_scr,
                s_scr, *, dk):
    t = q_ref.shape[1]
    tc = qc_ref.shape[1]
    L = CHUNK
    nc, ncc = t // L, tc // L
    n_steps = nc + ncc
    shift = (LANES - 2 * pl.program_id(1)) % LANES
    bias = pltpu.roll(jnp.broadcast_to(bias_ref[...], (SUBLANES, LANES)), shift, 1)[0:1, :]
    prep_scr = (bc_scr, cm_scr, xt_scr, tot_scr, gmax_scr)
    state_scr = (bc_scr, cm_scr, xt_scr, tot_scr, m_in_scr, m_out_scr, s_scr)

    for c in range(ncc):
        _mlstm_gate_prep(pltpu.roll(gtc_ref[0, c * L:(c + 1) * L, :], shift, 1) + bias, c, *prep_scr)

    def prep(c, carry):
        rows = pl.ds(pl.multiple_of(c * L, L), L)
        _mlstm_gate_prep(pltpu.roll(gt_ref[0, rows, :], shift, 1) + bias, c + ncc, *prep_scr)
        return carry

    lax.fori_loop(0, nc, prep, 0, unroll=2)

    fwd_id = lambda i: i
    bwd_id = lambda i: (ncc - 1 - i) if i < ncc else (n_steps - 1 - (i - ncc))
    fwd8 = lax.broadcasted_iota(jnp.int32, (SUBLANES, LANES), 1) < FWD_GATE_LANES
    m = jnp.zeros((SUBLANES, LANES), F32)
    for i in range(n_steps):
        tot = jnp.where(fwd8, tot_scr[fwd_id(i)], tot_scr[bwd_id(i)])
        gmax = jnp.where(fwd8, gmax_scr[fwd_id(i)], gmax_scr[bwd_id(i)])
        m_in_scr[i] = m
        m = jnp.maximum(tot + m, gmax)
        m_out_scr[i] = m

    s_scr[...] = jnp.zeros_like(s_scr)
    hs_scr[...] = jnp.zeros_like(hs_scr)
    for i in range(ncc):
        for d, cid in ((0, fwd_id(i)), (1, bwd_id(i))):
            rows = slice(cid * L, (cid + 1) * L)
            _mlstm_chunk(qc_ref[0, rows, :], kc_ref[0, rows, :], vc_ref[0, rows, :], cid, i, d,
                         state_scr, False, dk)

    def body(j, carry):
        for d in (0, 1):
            cj = j if d == 0 else nc - 1 - j
            rows = pl.ds(pl.multiple_of(cj * L, L), L)
            hs = _mlstm_chunk(q_ref[0, rows, :], k_ref[0, rows, :], v_ref[0, rows, :], cj + ncc, j + ncc, d,
                              state_scr, True, dk)
            hs_scr[rows, :] += jnp.concatenate(hs, axis=1)
        return carry

    lax.fori_loop(0, nc, body, 0, unroll=2)

    def fin(c, carry):
        rows = pl.ds(pl.multiple_of(c * L, L), L)
        hh = hs_scr[rows, :]
        outs = []
        for e in (0, 1):
            x = hh[:, e * LANES:(e + 1) * LANES]
            outs.append(x * lax.rsqrt(jnp.mean(x * x, axis=-1, keepdims=True) + EPS))
        y = jnp.concatenate(outs, axis=1) * gain_ref[...] * jax.nn.sigmoid(o_ref[0, rows, :])
        out_ref[0, rows, :] = y.astype(out_ref.dtype)
        return carry

    lax.fori_loop(0, nc, fin, 0)


def _mlstm_call(q, k, v, o, gt, qc, kc, vc, gtc, bias, gain):
    b, t, d_qk = q.shape
    tc = qc.shape[1]
    d_v = v.shape[2]
    pairs = M_HEADS // 2
    dk = d_qk // M_HEADS
    assert d_qk // pairs == LANES and d_v // pairs == 2 * LANES and CHUNK == LANES
    n_chunks = (t + tc) // CHUNK
    seq = lambda n, w: pl.BlockSpec((1, n, w), lambda bi, p: (bi, 0, p))
    allg = lambda n: pl.BlockSpec((1, n, LANES), lambda bi, p: (bi, 0, 0))
    chunk_f32 = lambda rows, cols: pltpu.VMEM((n_chunks, rows, cols), F32)
    return pl.pallas_call(
        functools.partial(_mlstm_body, dk=dk),
        grid=(b, pairs),
        in_specs=[seq(t, LANES), seq(t, LANES), seq(t, 2 * LANES), seq(t, 2 * LANES), allg(t),
                  seq(tc, LANES), seq(tc, LANES), seq(tc, 2 * LANES), allg(tc),
                  pl.BlockSpec((1, LANES), lambda bi, p: (0, 0)),
                  pl.BlockSpec((1, 2 * LANES), lambda bi, p: (0, p))],
        out_specs=seq(t, 2 * LANES),
        out_shape=jax.ShapeDtypeStruct((b, t, d_v), F32),
        scratch_shapes=[pltpu.VMEM((t, 2 * LANES), F32),
                        chunk_f32(CHUNK, LANES), chunk_f32(CHUNK, LANES),
                        chunk_f32(2 * SUBLANES, CHUNK),
                        chunk_f32(SUBLANES, LANES), chunk_f32(SUBLANES, LANES),
                        chunk_f32(SUBLANES, LANES), chunk_f32(SUBLANES, LANES),
                        pltpu.VMEM((4, LANES, 2 * LANES), F32)],
        compiler_params=_cparams(("parallel", "parallel")),
        name="mlstm",
    )(q, k, v, o, gt, qc, kc, vc, gtc, bias, gain)


def _outproj_body(x_ref, lru_ref, mls_ref, wa_ref, wb_ref, g_ref, o_ref):
    w_cols, r, dm = mls_ref.shape[1:]
    m = mls_ref[0].reshape(w_cols * r, dm).astype(BF16)
    m = jnp.dot(_colmajor_perm(w_cols, r), m, preferred_element_type=F32).astype(BF16)
    y = (jnp.dot(lru_ref[0], wa_ref[...], preferred_element_type=F32)
         + jnp.dot(m, wb_ref[...], preferred_element_type=F32))
    o_ref[0] = x_ref[0] + g_ref[0] * y


def _outproj_call(x3d, lru, mls_cm, w_a, w_b, gate):
    b, t, d = x3d.shape
    da, dm = lru.shape[2], mls_cm.shape[2]
    rows = t // GRID_W
    r_tile = PROJ_TM // GRID_W
    assert r_tile == SUBLANES
    mls_view = mls_cm.reshape(b, GRID_W, rows, dm)
    return pl.pallas_call(
        _outproj_body,
        grid=(b, t // PROJ_TM),
        in_specs=[pl.BlockSpec((1, PROJ_TM, d), lambda bi, i: (bi, i, 0)),
                  pl.BlockSpec((1, PROJ_TM, da), lambda bi, i: (bi, i, 0)),
                  pl.BlockSpec((1, GRID_W, r_tile, dm), lambda bi, i: (bi, 0, i, 0)),
                  pl.BlockSpec((da, d), lambda bi, i: (0, 0)),
                  pl.BlockSpec((dm, d), lambda bi, i: (0, 0)),
                  pl.BlockSpec((1, 1, d), lambda bi, i: (bi, 0, 0))],
        out_specs=pl.BlockSpec((1, PROJ_TM, d), lambda bi, i: (bi, i, 0)),
        out_shape=jax.ShapeDtypeStruct((b, t, d), F32),
        compiler_params=_cparams(("parallel", "parallel")),
        name="outproj",
    )(x3d, lru, mls_view, w_a, w_b, gate)


def kernel(x, c, ctx, c_ctx, ada_w, ada_b, ffn1_norm, ffn1_w_up, ffn1_w_down, mix_norm, w_in, b_mgate, lru_conv_w, lru_conv_b, lru_w_r, lru_b_r, lru_w_i, lru_b_i, lru_lam, mlstm_norm, w_out, ffn2_norm, ffn2_w_up, ffn2_w_down, final_norm):
    b, t, d = x.shape
    tc = ctx.shape[1]
    assert ada_w.shape[0] == 1, "single-layer block only"
    assert t % GRID_W == 0 and t % PROJ_TM == 0
    rows = t // GRID_W
    d_lru = lru_conv_w.shape[2]
    d_mv = mlstm_norm.shape[1]
    d_mqk = (w_in.shape[2] - 2 * d_lru - 2 * d_mv - 4 * M_HEADS) // 2

    pad = SUBLANES - b - 1
    cc = jnp.concatenate([c, c_ctx[None, :], jnp.zeros((pad, d), F32)], axis=0)
    mod = _ada_call(cc, ada_w[0], ada_b[0][None, :]).reshape(SUBLANES, N_MOD, d)
    lat = lambda i: mod[:b, i][:, None, :]
    cxt = lambda i: mod[b:b + 1, i][:, None, :]
    row = lambda v: v[0][None, :]

    up1, down1 = ffn1_w_up[0], ffn1_w_down[0]
    x1 = _ffn_call(x.reshape(b * t, d), row(ffn1_norm), lat(0), lat(1), lat(2), up1, down1,
                   row(ffn1_norm), rows_per_mod=t, final_norm=False, name="ffn_pre")
    ctx1 = _ffn_call(ctx.reshape(b * tc, d), row(ffn1_norm), cxt(0), cxt(1), cxt(2), up1, down1,
                     row(ffn1_norm), rows_per_mod=b * tc, final_norm=False, name="ffn_pre_ctx")
    x1 = x1.reshape(b, t, d)
    ctx1 = ctx1.reshape(b, tc, d)

    w_in0 = w_in[0]
    w_a = w_in0[:, :2 * d_lru].astype(BF16)
    n_gate = 4 * M_HEADS
    w_b = jnp.concatenate([w_in0[:, 2 * d_lru:], jnp.zeros((d, LANES - n_gate), F32)], axis=1).astype(BF16)
    widths_a, dtypes_a = (d_lru, d_lru), (F32, F32)
    widths_b, dtypes_b = (d_mqk, d_mqk, d_mv, d_mv, LANES), (BF16, BF16, BF16, F32, F32)
    mixn = row(mix_norm)
    lx, lg = _proj_call(x1, mixn, lat(3), lat(4), w_a, widths_a, dtypes_a, tm=PROJ_TM, colmajor=False,
                        name="proj_lru")
    q, k, v, o, gt = _proj_call(x1.reshape(b, rows, GRID_W, d), mixn, lat(3), lat(4), w_b, widths_b,
                                dtypes_b, tm=PROJ_TM, colmajor=True, name="proj_mlstm")
    ctx_mod = lambda i: jnp.broadcast_to(cxt(i), (b, 1, d))
    lxc, _ = _proj_call(ctx1, mixn, ctx_mod(3), ctx_mod(4), w_a, widths_a, dtypes_a, tm=tc, colmajor=False,
                        name="proj_lru_ctx")
    qc, kc, vc, _, gtc = _proj_call(ctx1, mixn, ctx_mod(3), ctx_mod(4), w_b, widths_b, dtypes_b, tm=tc,
                                    colmajor=False, name="proj_mlstm_ctx")

    lru_lat = _lru_call(lx, lg, lxc, lru_conv_w[0], row(lru_conv_b), lru_w_r[0], lru_b_r[0], lru_w_i[0],
                        lru_b_i[0], lru_lam[0])
    bias = jnp.concatenate([b_mgate[0], jnp.zeros((LANES - n_gate,), F32)])[None, :]
    mls_cm = _mlstm_call(q, k, v, o, gt, qc, kc, vc, gtc, bias, row(mlstm_norm))

    w_out0 = w_out[0].astype(BF16)
    x2 = _outproj_call(x1, lru_lat, mls_cm, w_out0[:d_lru], w_out0[d_lru:], lat(5))

    up2, down2 = ffn2_w_up[0], ffn2_w_down[0]
    out = _ffn_call(x2.reshape(b * t, d), row(ffn2_norm), lat(6), lat(7), lat(8), up2, down2,
                    final_norm[None, :], rows_per_mod=t, final_norm=True, name="ffn_post")
    return out.reshape(b, t, d)
```

```python
import functools

import jax
import jax.numpy as jnp
from jax import lax
from jax.experimental import pallas as pl
from jax.experimental.pallas import tpu as pltpu

F32 = jnp.float32
BF16 = jnp.bfloat16

GRID_W = 64
LRU_BLOCKS = 8
CONV_W = 4
LRU_C = 8.0
M_HEADS = 8
CHUNK = 128
N_MOD = 9
EPS = 1e-6
HALF = 0.5

LANES = 128
SUBLANES = 8
VMEM_LIMIT_BYTES = 58 * 1024 * 1024

FFN_TM = 1024
FFN_TF = 256
FFN_EDGE_ROWS = 256
PROJ_ROWS = 256
PROJ_TM = 512
LRU_GATE_ROWS = 512
ADA_TN = 1024


def _cparams(sem):
    return pltpu.CompilerParams(dimension_semantics=sem, vmem_limit_bytes=VMEM_LIMIT_BYTES)


def _sigmoid(z):
    return 0.5 * jnp.tanh(0.5 * z) + 0.5


def _rms(x, g):
    return x * lax.rsqrt(jnp.mean(x * x, axis=-1, keepdims=True) + EPS) * g


def _ada_body(c_ref, w_ref, b_ref, o_ref):
    s = c_ref[...]
    s = s * jax.nn.sigmoid(s)
    o_ref[...] = jnp.dot(s, w_ref[...], preferred_element_type=F32) + b_ref[...]


def _ada_call(cc, w, b):
    rows, d = cc.shape
    n = w.shape[1]
    return pl.pallas_call(
        _ada_body,
        grid=(n // ADA_TN,),
        in_specs=[pl.BlockSpec((rows, d), lambda j: (0, 0)),
                  pl.BlockSpec((d, ADA_TN), lambda j: (0, j)),
                  pl.BlockSpec((1, ADA_TN), lambda j: (0, j))],
        out_specs=pl.BlockSpec((rows, ADA_TN), lambda j: (0, j)),
        out_shape=jax.ShapeDtypeStruct((rows, n), F32),
        compiler_params=_cparams(("arbitrary",)),
        name="ada_mod",
    )(cc, w, b)


def _ffn_body(x_ref, g_ref, sh_ref, sc_ref, gate_ref, wg_ref, wu_ref, wd_ref, fin_ref, o_ref,
              h_scr, *, final_norm):
    j = pl.program_id(1)
    last = pl.num_programs(1) - 1
    tm = x_ref.shape[0]
    row_blocks = [slice(r0, r0 + min(FFN_EDGE_ROWS, tm)) for r0 in range(0, tm, min(FFN_EDGE_ROWS, tm))]

    def contrib(h):
        g = jnp.dot(h, wg_ref[...].astype(BF16), preferred_element_type=F32)
        u = jnp.dot(h, wu_ref[...].astype(BF16), preferred_element_type=F32)
        a = (g * jax.nn.sigmoid(g) * u).astype(BF16)
        return jnp.dot(a, wd_ref[...].astype(BF16), preferred_element_type=F32)

    @pl.when(j == 0)
    def _():
        for rows in row_blocks:
            h = (_rms(x_ref[rows, :], g_ref[...]) * (1.0 + sc_ref[0]) + sh_ref[0]).astype(BF16)
            h_scr[rows, :] = h
            o_ref[rows, :] = contrib(h)

    @pl.when((j > 0) & (j < last))
    def _():
        o_ref[...] += contrib(h_scr[...])

    @pl.when(j == last)
    def _():
        for rows in row_blocks:
            y = x_ref[rows, :] + HALF * gate_ref[0] * (o_ref[rows, :] + contrib(h_scr[rows, :]))
            if final_norm:
                y = _rms(y, fin_ref[...])
            o_ref[rows, :] = y


def _ffn_call(x2d, norm_g, shift, scale, gate, w_up, w_down, fin_g, *, rows_per_mod, final_norm, name):
    m, d = x2d.shape
    f = w_down.shape[0]
    tm = min(FFN_TM, m)
    tiles_per_mod = rows_per_mod // tm
    nf = f // FFN_TF
    mod_spec = pl.BlockSpec((1, 1, d), lambda i, j: (i // tiles_per_mod, 0, 0))
    vec_spec = pl.BlockSpec((1, d), lambda i, j: (0, 0))
    return pl.pallas_call(
        functools.partial(_ffn_body, final_norm=final_norm),
        grid=(m // tm, nf),
        in_specs=[pl.BlockSpec((tm, d), lambda i, j: (i, 0)),
                  vec_spec, mod_spec, mod_spec, mod_spec,
                  pl.BlockSpec((d, FFN_TF), lambda i, j: (0, j)),
                  pl.BlockSpec((d, FFN_TF), lambda i, j: (0, j + nf)),
                  pl.BlockSpec((FFN_TF, d), lambda i, j: (j, 0)),
                  vec_spec],
        out_specs=pl.BlockSpec((tm, d), lambda i, j: (i, 0)),
        out_shape=jax.ShapeDtypeStruct((m, d), F32),
        scratch_shapes=[pltpu.VMEM((tm, d), BF16)],
        compiler_params=_cparams(("parallel", "arbitrary")),
        name=name,
    )(x2d, norm_g, shift, scale, gate, w_up, w_up, w_down, fin_g)


def _colmajor_perm(n_rows, n_cols, src0=0, n_src=None):
    assert n_rows & (n_rows - 1) == 0
    m = n_rows * n_cols
    n_src = m if n_src is None else n_src
    j = lax.broadcasted_iota(jnp.int32, (m, n_src), 0)
    i = lax.broadcasted_iota(jnp.int32, (m, n_src), 1) + src0
    src = (j & (n_rows - 1)) * n_cols + (j >> (n_rows.bit_length() - 1))
    return (i == src).astype(BF16)


def _proj_body(x_ref, g_ref, sh_ref, sc_ref, w_ref, *refs, widths, colmajor):
    out_refs, h_scr = refs[:-1], refs[-1]
    tm = h_scr.shape[0]
    rb = min(PROJ_ROWS, tm)
    norm = lambda x: (_rms(x, g_ref[...]) * (1.0 + sc_ref[0]) + sh_ref[0]).astype(BF16)
    if colmajor:
        r, c, d = x_ref.shape[1:]
        rows_per_block = rb // c
        hp = None
        for r0 in range(0, r, rows_per_block):
            h = norm(x_ref[0, r0:r0 + rows_per_block].reshape(rb, d))
            part = jnp.dot(_colmajor_perm(r, c, r0 * c, rb), h, preferred_element_type=F32)
            hp = part if hp is None else hp + part
        h_scr[...] = hp.astype(BF16)
    for r0 in range(0, tm, rb):
        rows = slice(r0, r0 + rb)
        h = h_scr[rows, :] if colmajor else norm(x_ref[0, rows, :])
        off = 0
        for o_ref, wdt in zip(out_refs, widths):
            o_ref[0, rows, :] = jnp.dot(h, w_ref[:, off:off + wdt],
                                        preferred_element_type=F32).astype(o_ref.dtype)
            off += wdt


def _proj_call(x, norm_g, shift, scale, w, widths, dtypes, *, tm, colmajor, name):
    b = x.shape[0]
    d = norm_g.shape[1]
    if colmajor:
        rows, gw = x.shape[1:3]
        cols = tm // rows
        assert cols == SUBLANES and gw % cols == 0
        n_tiles = gw // cols
        x_spec = pl.BlockSpec((1, rows, cols, d), lambda bi, i: (bi, 0, i, 0))
    else:
        n_tiles = x.shape[1] // tm
        x_spec = pl.BlockSpec((1, tm, d), lambda bi, i: (bi, i, 0))
    mod_spec = pl.BlockSpec((1, 1, d), lambda bi, i: (bi, 0, 0))
    return pl.pallas_call(
        functools.partial(_proj_body, widths=tuple(widths), colmajor=colmajor),
        grid=(b, n_tiles),
        in_specs=[x_spec,
                  pl.BlockSpec((1, d), lambda bi, i: (0, 0)),
                  mod_spec, mod_spec,
                  pl.BlockSpec(w.shape, lambda bi, i: (0, 0))],
        out_specs=[pl.BlockSpec((1, tm, wdt), lambda bi, i: (bi, i, 0)) for wdt in widths],
        out_shape=[jax.ShapeDtypeStruct((b, n_tiles * tm, wdt), dt) for wdt, dt in zip(widths, dtypes)],
        scratch_shapes=[pltpu.VMEM((tm, d), BF16)],
        compiler_params=_cparams(("parallel", "parallel")),
        name=name,
    )(x, norm_g, shift, scale, w)


def _lru_conv(x_ref, cw_ref, cb_ref, pad_scr, xc_scr, t):
    zeros = jnp.zeros((SUBLANES, LANES), F32)
    pad_scr[0:SUBLANES, :] = zeros
    pad_scr[SUBLANES:SUBLANES + t, :] = x_ref[0]
    pad_scr[SUBLANES + t:2 * SUBLANES + t, :] = zeros
    ch = min(LRU_GATE_ROWS, t)
    win_rows = ch + 2 * SUBLANES

    def body(c, carry):
        off = pl.multiple_of(c * ch, SUBLANES)
        win = pad_scr[pl.ds(off, win_rows), :]
        acc = cb_ref[...] + cw_ref[2:3, :] * win[SUBLANES:SUBLANES + ch]
        for k, shift in ((0, 2), (1, 1), (3, win_rows - 1)):
            acc = acc + cw_ref[k:k + 1, :] * pltpu.roll(win, shift, 0)[SUBLANES:SUBLANES + ch]
        xc_scr[pl.ds(off, ch), :] = acc
        return carry

    lax.fori_loop(0, t // ch, body, 0)


def _lru_gates(xc_scr, t, d, wr_ref, br_ref, wi_ref, bi_ref, lam_ref, a_scr, b_scr):
    ch = min(LRU_GATE_ROWS, t)
    log_a_unit = -LRU_C * jax.nn.softplus(-lam_ref[d:d + 1, :])
    wr = wr_ref[d, 0].astype(BF16)
    wi = wi_ref[d, 0].astype(BF16)

    def body(c, carry):
        off = pl.multiple_of(c * ch, SUBLANES)
        xc = xc_scr[pl.ds(off, ch), :]
        xb = xc.astype(BF16)
        r = _sigmoid(jnp.dot(xb, wr, preferred_element_type=F32) + br_ref[d:d + 1, :])
        i = _sigmoid(jnp.dot(xb, wi, preferred_element_type=F32) + bi_ref[d:d + 1, :])
        log_a = log_a_unit * r
        a = jnp.exp(log_a)
        a_scr[pl.ds(off, ch), :] = a
        b_scr[pl.ds(off, ch), :] = jnp.sqrt(-jnp.tanh(log_a) * (a * a + 1.0)) * i * xc
        return carry

    lax.fori_loop(0, t // ch, body, 0)


def _lru_scan(a_scr, b_scr, t, d, h0, h_scr):
    groups = t // SUBLANES
    row = lax.broadcasted_iota(jnp.int32, (SUBLANES, LANES), 0)

    def body(g, carry):
        gi = g if d == 0 else groups - 1 - g
        off = pl.multiple_of(gi * SUBLANES, SUBLANES)
        a = a_scr[pl.ds(off, SUBLANES), :]
        bv = b_scr[pl.ds(off, SUBLANES), :]
        for k in (1, 2, 4):
            shift = k if d == 0 else SUBLANES - k
            valid = (row >= k) if d == 0 else (row < SUBLANES - k)
            a_prev = pltpu.roll(a, shift, 0)
            b_prev = pltpu.roll(bv, shift, 0)
            bv = jnp.where(valid, a * b_prev + bv, bv)
            a = jnp.where(valid, a * a_prev, a)
        h = a * carry + bv
        if h_scr is not None:
            if d == 0:
                h_scr[pl.ds(off, SUBLANES), :] = h
            else:
                h_scr[pl.ds(off, SUBLANES), :] += h
        return h[SUBLANES - 1:SUBLANES, :] if d == 0 else h[0:1, :]

    return lax.fori_loop(0, groups, body, h0, unroll=4)


def _lru_body(lx_ref, lg_ref, lxc_ref, cw_ref, cb_ref, wr_ref, br_ref, wi_ref, bi_ref, lam_ref, o_ref,
              pad_scr, xc_scr, xcc_scr, a_scr, b_scr, h_scr):
    t = lx_ref.shape[1]
    tc = lxc_ref.shape[1]
    _lru_conv(lxc_ref, cw_ref, cb_ref, pad_scr, xcc_scr, tc)
    _lru_conv(lx_ref, cw_ref, cb_ref, pad_scr, xc_scr, t)
    gate_refs = (wr_ref, br_ref, wi_ref, bi_ref, lam_ref)
    for d in (0, 1):
        _lru_gates(xcc_scr, tc, d, *gate_refs, a_scr, b_scr)
        h0 = _lru_scan(a_scr, b_scr, tc, d, jnp.zeros((1, LANES), F32), None)
        _lru_gates(xc_scr, t, d, *gate_refs, a_scr, b_scr)
        _lru_scan(a_scr, b_scr, t, d, h0, h_scr)
    o_ref[0] = (jax.nn.gelu(lg_ref[0]) * h_scr[...]).astype(o_ref.dtype)


def _lru_call(lx, lg, lxc, conv_w, conv_b, w_r, b_r, w_i, b_i, lam):
    b, t, d_lru = lx.shape
    tc = lxc.shape[1]
    nb = d_lru // LANES
    assert w_r.shape == (2, nb, LANES, LANES)
    seq = lambda n: pl.BlockSpec((1, n, LANES), lambda bi, j: (bi, 0, j))
    vec = lambda n: pl.BlockSpec((n, LANES), lambda bi, j: (0, j))
    wblk = pl.BlockSpec((2, 1, LANES, LANES), lambda bi, j: (0, j, 0, 0))
    return pl.pallas_call(
        _lru_body,
        grid=(b, nb),
        in_specs=[seq(t), seq(t), seq(tc), vec(CONV_W), vec(1), wblk, vec(2), wblk, vec(2), vec(2)],
        out_specs=seq(t),
        out_shape=jax.ShapeDtypeStruct((b, t, d_lru), BF16),
        scratch_shapes=[pltpu.VMEM((t + 2 * SUBLANES, LANES), F32),
                        pltpu.VMEM((t, LANES), F32), pltpu.VMEM((tc, LANES), F32),
                        pltpu.VMEM((t, LANES), F32), pltpu.VMEM((t, LANES), F32),
                        pltpu.VMEM((t, LANES), F32)],
        compiler_params=_cparams(("parallel", "parallel")),
        name="rglru",
    )(lx, lg, lxc, conv_w, conv_b, w_r, b_r, w_i, b_i, lam)


FWD_GATE_LANES = 2 * M_HEADS


def _split3_dot(tri, x):
    hi = x.astype(BF16)
    r1 = x - hi.astype(F32)
    mid = r1.astype(BF16)
    lo = (r1 - mid.astype(F32)).astype(BF16)
    return (jnp.dot(tri, hi, preferred_element_type=F32) + jnp.dot(tri, mid, preferred_element_type=F32)
            + jnp.dot(tri, lo, preferred_element_type=F32))


def _cummax_rows(x, row, reverse):
    n = x.shape[0]
    k = 1
    while k < n:
        if reverse:
            x = jnp.where(row < n - k, jnp.maximum(x, pltpu.roll(x, n - k, 0)), x)
        else:
            x = jnp.where(row >= k, jnp.maximum(x, pltpu.roll(x, k, 0)), x)
        k *= 2
    return x


def _round_up_bf16(x):
    return (x + jnp.abs(x) * (2.0 ** -7)).astype(BF16)


def _mlstm_gate_prep(gates, kidx, bc_scr, cm_scr, xt_scr, tot_scr, gmax_scr):
    L = gates.shape[0]
    ti = lax.broadcasted_iota(jnp.int32, (L, L), 0)
    si = lax.broadcasted_iota(jnp.int32, (L, L), 1)
    row = lax.broadcasted_iota(jnp.int32, (L, LANES), 0)
    fwd = lax.broadcasted_iota(jnp.int32, (L, LANES), 1) < FWD_GATE_LANES
    lf = jax.nn.log_sigmoid(gates)
    bc_f = _split3_dot((si <= ti).astype(BF16), lf)
    bc_b = _split3_dot((si >= ti).astype(BF16), lf)
    bc = jnp.where(fwd, bc_f, bc_b)
    tot = jnp.where(fwd[0:1], bc_f[L - 1:L], bc_b[0:1])
    x = pltpu.roll(gates, SUBLANES, 1) - bc
    cm = jnp.where(fwd, _cummax_rows(x, row, False), _cummax_rows(x, row, True))
    gmax = jnp.max(tot + x, axis=0, keepdims=True)
    xt = x.T
    bc_scr[kidx] = bc
    cm_scr[kidx] = cm
    xt_scr[kidx, 0:SUBLANES, :] = xt[SUBLANES:2 * SUBLANES]
    xt_scr[kidx, SUBLANES:2 * SUBLANES, :] = xt[3 * SUBLANES:4 * SUBLANES]
    tot_scr[kidx] = jnp.broadcast_to(tot, (SUBLANES, LANES))
    gmax_scr[kidx] = jnp.broadcast_to(gmax, (SUBLANES, LANES))


def _mlstm_chunk(q, k, v, kidx, step, d, scr, need_h, dk):
    bc_scr, cm_scr, xt_scr, tot_scr, m_in_scr, m_out_scr, s_scr = scr
    L = q.shape[0]
    ti = lax.broadcasted_iota(jnp.int32, (L, L), 0)
    si = lax.broadcasted_iota(jnp.int32, (L, L), 1)
    causal = (si <= ti) if d == 0 else (si >= ti)
    lane = lax.broadcasted_iota(jnp.int32, (L, LANES), 1)
    krow = lax.broadcasted_iota(jnp.int32, (LANES, L), 0)
    k_t = k.astype(F32).T * (dk ** -0.5)
    tot_v, m_in_v, m_out_v = tot_scr[kidx], m_in_scr[step], m_out_scr[step]
    ones = jnp.ones((L, LANES), BF16)
    cf0 = FWD_GATE_LANES * d + SUBLANES
    if need_h:
        bc = bc_scr[kidx]
        sel_lane = lax.broadcasted_iota(jnp.int32, (LANES, 2 * LANES), 0)
        sel_col = lax.broadcasted_iota(jnp.int32, (LANES, 2 * LANES), 1)
        sel = (sel_lane == jnp.where(sel_col < LANES, cf0, cf0 + 1)).astype(BF16)
        mx_all = _round_up_bf16(jnp.maximum(m_in_v[0:1, :], cm_scr[kidx]))
        mx_tiles = jnp.dot(mx_all, sel, preferred_element_type=F32)
    hs = []
    for e in (0, 1):
        cf = cf0 + e
        tot = tot_v[0:1, cf:cf + 1]
        m_in = m_in_v[0:1, cf:cf + 1]
        m_out = m_out_v[0:1, cf:cf + 1]
        x_row = xt_scr[kidx, SUBLANES * d + e:SUBLANES * d + e + 1, :]
        s_old = s_scr[2 * d + e]
        k_te = jnp.where((krow >= e * dk) & (krow < (e + 1) * dk), k_t, 0.0)
        v1 = jnp.concatenate([v[:, e * LANES:(e + 1) * LANES], ones], axis=1)
        if need_h:
            mx = mx_tiles[:, e * LANES:(e + 1) * LANES]
            q_e = jnp.where((lane >= e * dk) & (lane < (e + 1) * dk), q, jnp.zeros_like(q))
            w = (jnp.dot(q_e, k_te.astype(BF16), preferred_element_type=F32)
                 * jnp.exp(jnp.where(causal, x_row - mx, -jnp.inf)))
            s_inter = jnp.exp(m_in - mx)
            qs = jnp.dot(q_e, s_old.astype(BF16), preferred_element_type=F32)
            wv = jnp.dot(w.astype(BF16), v1, preferred_element_type=F32)
            num = s_inter * qs[:, :LANES] + wv[:, :LANES]
            den = s_inter * qs[:, LANES:] + wv[:, LANES:]
            m_row = bc[:, cf:cf + 1] + mx[:, cf:cf + 1]
            hs.append(num * (1.0 / jnp.maximum(jnp.abs(den[:, cf:cf + 1]), jnp.exp(-m_row))))
        wg_row = jnp.exp(tot + x_row - m_out)
        decay = jnp.exp(tot + m_in - m_out)
        s_scr[2 * d + e] = decay * s_old + jnp.dot((k_te * wg_row).astype(BF16), v1,
                                                   preferred_element_type=F32)
    return hs


def _mlstm_body(q_ref, k_ref, v_ref, o_ref, gt_ref, qc_ref, kc_ref, vc_ref, gtc_ref, bias_ref, gain_ref,
                out_ref, hs_scr, bc_scr, cm_scr, xt_scr, tot_scr, gmax_scr, m_in_scr, m_out_scr,
                s_scr, *, dk):
    t = q_ref.shape[1]
    tc = qc_ref.shape[1]
    L = CHUNK
    nc, ncc = t // L, tc // L
    n_steps = nc + ncc
    shift = (LANES - 2 * pl.program_id(1)) % LANES
    bias = pltpu.roll(jnp.broadcast_to(bias_ref[...], (SUBLANES, LANES)), shift, 1)[0:1, :]
    prep_scr = (bc_scr, cm_scr, xt_scr, tot_scr, gmax_scr)
    state_scr = (bc_scr, cm_scr, xt_scr, tot_scr, m_in_scr, m_out_scr, s_scr)

    for c in range(ncc):
        _mlstm_gate_prep(pltpu.roll(gtc_ref[0, c * L:(c + 1) * L, :], shift, 1) + bias, c, *prep_scr)

    def prep(c, carry):
        rows = pl.ds(pl.multiple_of(c * L, L), L)
        _mlstm_gate_prep(pltpu.roll(gt_ref[0, rows, :], shift, 1) + bias, c + ncc, *prep_scr)
        return carry

    lax.fori_loop(0, nc, prep, 0, unroll=2)

    fwd_id = lambda i: i
    bwd_id = lambda i: (ncc - 1 - i) if i < ncc else (n_steps - 1 - (i - ncc))
    fwd8 = lax.broadcasted_iota(jnp.int32, (SUBLANES, LANES), 1) < FWD_GATE_LANES
    m = jnp.zeros((SUBLANES, LANES), F32)
    for i in range(n_steps):
        tot = jnp.where(fwd8, tot_scr[fwd_id(i)], tot_scr[bwd_id(i)])
        gmax = jnp.where(fwd8, gmax_scr[fwd_id(i)], gmax_scr[bwd_id(i)])
        m_in_scr[i] = m
        m = jnp.maximum(tot + m, gmax)
        m_out_scr[i] = m

    s_scr[...] = jnp.zeros_like(s_scr)
    hs_scr[...] = jnp.zeros_like(hs_scr)
    for i in range(ncc):
        for d, cid in ((0, fwd_id(i)), (1, bwd_id(i))):
            rows = slice(cid * L, (cid + 1) * L)
            _mlstm_chunk(qc_ref[0, rows, :], kc_ref[0, rows, :], vc_ref[0, rows, :], cid, i, d,
                         state_scr, False, dk)

    def body(j, carry):
        for d in (0, 1):
            cj = j if d == 0 else nc - 1 - j
            rows = pl.ds(pl.multiple_of(cj * L, L), L)
            hs = _mlstm_chunk(q_ref[0, rows, :], k_ref[0, rows, :], v_ref[0, rows, :], cj + ncc, j + ncc, d,
                              state_scr, True, dk)
            hs_scr[rows, :] += jnp.concatenate(hs, axis=1)
        return carry

    lax.fori_loop(0, nc, body, 0, unroll=2)

    def fin(c, carry):
        rows = pl.ds(pl.multiple_of(c * L, L), L)
        hh = hs_scr[rows, :]
        outs = []
        for e in (0, 1):
            x = hh[:, e * LANES:(e + 1) * LANES]
            outs.append(x * lax.rsqrt(jnp.mean(x * x, axis=-1, keepdims=True) + EPS))
        y = jnp.concatenate(outs, axis=1) * gain_ref[...] * jax.nn.sigmoid(o_ref[0, rows, :])
        out_ref[0, rows, :] = y.astype(out_ref.dtype)
        return carry

    lax.fori_loop(0, nc, fin, 0)


def _mlstm_call(q, k, v, o, gt, qc, kc, vc, gtc, bias, gain):
    b, t, d_qk = q.shape
    tc = qc.shape[1]
    d_v = v.shape[2]
    pairs = M_HEADS // 2
    dk = d_qk // M_HEADS
    assert d_qk // pairs == LANES and d_v // pairs == 2 * LANES and CHUNK == LANES
    n_chunks = (t + tc) // CHUNK
    seq = lambda n, w: pl.BlockSpec((1, n, w), lambda bi, p: (bi, 0, p))
    allg = lambda n: pl.BlockSpec((1, n, LANES), lambda bi, p: (bi, 0, 0))
    chunk_f32 = lambda rows, cols: pltpu.VMEM((n_chunks, rows, cols), F32)
    return pl.pallas_call(
        functools.partial(_mlstm_body, dk=dk),
        grid=(b, pairs),
        in_specs=[seq(t, LANES), seq(t, LANES), seq(t, 2 * LANES), seq(t, 2 * LANES), allg(t),
                  seq(tc, LANES), seq(tc, LANES), seq(tc, 2 * LANES), allg(tc),
                  pl.BlockSpec((1, LANES), lambda bi, p: (0, 0)),
                  pl.BlockSpec((1, 2 * LANES), lambda bi, p: (0, p))],
        out_specs=seq(t, 2 * LANES),
        out_shape=jax.ShapeDtypeStruct((b, t, d_v), F32),
        scratch_shapes=[pltpu.VMEM((t, 2 * LANES), F32),
                        chunk_f32(CHUNK, LANES), chunk_f32(CHUNK, LANES),
                        chunk_f32(2 * SUBLANES, CHUNK),
                        chunk_f32(SUBLANES, LANES), chunk_f32(SUBLANES, LANES),
                        chunk_f32(SUBLANES, LANES), chunk_f32(SUBLANES, LANES),
                        pltpu.VMEM((4, LANES, 2 * LANES), F32)],
        compiler_params=_cparams(("parallel", "parallel")),
        name="mlstm",
    )(q, k, v, o, gt, qc, kc, vc, gtc, bias, gain)


def _outproj_body(x_ref, lru_ref, mls_ref, wa_ref, wb_ref, g_ref, o_ref):
    w_cols, r, dm = mls_ref.shape[1:]
    m = mls_ref[0].reshape(w_cols * r, dm).astype(BF16)
    m = jnp.dot(_colmajor_perm(w_cols, r), m, preferred_element_type=F32).astype(BF16)
    y = (jnp.dot(lru_ref[0], wa_ref[...], preferred_element_type=F32)
         + jnp.dot(m, wb_ref[...], preferred_element_type=F32))
    o_ref[0] = x_ref[0] + g_ref[0] * y


def _outproj_call(x3d, lru, mls_cm, w_a, w_b, gate):
    b, t, d = x3d.shape
    da, dm = lru.shape[2], mls_cm.shape[2]
    rows = t // GRID_W
    r_tile = PROJ_TM // GRID_W
    assert r_tile == SUBLANES
    mls_view = mls_cm.reshape(b, GRID_W, rows, dm)
    return pl.pallas_call(
        _outproj_body,
        grid=(b, t // PROJ_TM),
        in_specs=[pl.BlockSpec((1, PROJ_TM, d), lambda bi, i: (bi, i, 0)),
                  pl.BlockSpec((1, PROJ_TM, da), lambda bi, i: (bi, i, 0)),
                  pl.BlockSpec((1, GRID_W, r_tile, dm), lambda bi, i: (bi, 0, i, 0)),
                  pl.BlockSpec((da, d), lambda bi, i: (0, 0)),
                  pl.BlockSpec((dm, d), lambda bi, i: (0, 0)),
                  pl.BlockSpec((1, 1, d), lambda bi, i: (bi, 0, 0))],
        out_specs=pl.BlockSpec((1, PROJ_TM, d), lambda bi, i: (bi, i, 0)),
        out_shape=jax.ShapeDtypeStruct((b, t, d), F32),
        compiler_params=_cparams(("parallel", "parallel")),
        name="outproj",
    )(x3d, lru, mls_view, w_a, w_b, gate)


def kernel(x, c, ctx, c_ctx, ada_w, ada_b, ffn1_norm, ffn1_w_up, ffn1_w_down, mix_norm, w_in, b_mgate, lru_conv_w, lru_conv_b, lru_w_r, lru_b_r, lru_w_i, lru_b_i, lru_lam, mlstm_norm, w_out, ffn2_norm, ffn2_w_up, ffn2_w_down, final_norm):
    b, t, d = x.shape
    tc = ctx.shape[1]
    assert ada_w.shape[0] == 1, "single-layer block only"
    assert t % GRID_W == 0 and t % PROJ_TM == 0
    rows = t // GRID_W
    d_lru = lru_conv_w.shape[2]
    d_mv = mlstm_norm.shape[1]
    d_mqk = (w_in.shape[2] - 2 * d_lru - 2 * d_mv - 4 * M_HEADS) // 2

    pad = SUBLANES - b - 1
    cc = jnp.concatenate([c, c_ctx[None, :], jnp.zeros((pad, d), F32)], axis=0)
    mod = _ada_call(cc, ada_w[0], ada_b[0][None, :]).reshape(SUBLANES, N_MOD, d)
    lat = lambda i: mod[:b, i][:, None, :]
    cxt = lambda i: mod[b:b + 1, i][:, None, :]
    row = lambda v: v[0][None, :]

    up1, down1 = ffn1_w_up[0], ffn1_w_down[0]
    x1 = _ffn_call(x.reshape(b * t, d), row(ffn1_norm), lat(0), lat(1), lat(2), up1, down1,
                   row(ffn1_norm), rows_per_mod=t, final_norm=False, name="ffn_pre")
    ctx1 = _ffn_call(ctx.reshape(b * tc, d), row(ffn1_norm), cxt(0), cxt(1), cxt(2), up1, down1,
                     row(ffn1_norm), rows_per_mod=b * tc, final_norm=False, name="ffn_pre_ctx")
    x1 = x1.reshape(b, t, d)
    ctx1 = ctx1.reshape(b, tc, d)

    w_in0 = w_in[0]
    w_a = w_in0[:, :2 * d_lru].astype(BF16)
    n_gate = 4 * M_HEADS
    w_b = jnp.concatenate([w_in0[:, 2 * d_lru:], jnp.zeros((d, LANES - n_gate), F32)], axis=1).astype(BF16)
    widths_a, dtypes_a = (d_lru, d_lru), (F32, F32)
    widths_b, dtypes_b = (d_mqk, d_mqk, d_mv, d_mv, LANES), (BF16, BF16, BF16, F32, F32)
    mixn = row(mix_norm)
    lx, lg = _proj_call(x1, mixn, lat(3), lat(4), w_a, widths_a, dtypes_a, tm=PROJ_TM, colmajor=False,
                        name="proj_lru")
    q, k, v, o, gt = _proj_call(x1.reshape(b, rows, GRID_W, d), mixn, lat(3), lat(4), w_b, widths_b,
                                dtypes_b, tm=PROJ_TM, colmajor=True, name="proj_mlstm")
    ctx_mod = lambda i: jnp.broadcast_to(cxt(i), (b, 1, d))
    lxc, _ = _proj_call(ctx1, mixn, ctx_mod(3), ctx_mod(4), w_a, widths_a, dtypes_a, tm=tc, colmajor=False,
                        name="proj_lru_ctx")
    qc, kc, vc, _, gtc = _proj_call(ctx1, mixn, ctx_mod(3), ctx_mod(4), w_b, widths_b, dtypes_b, tm=tc,
                                    colmajor=False, name="proj_mlstm_ctx")

    lru_lat = _lru_call(lx, lg, lxc, lru_conv_w[0], row(lru_conv_b), lru_w_r[0], lru_b_r[0], lru_w_i[0],
                        lru_b_i[0], lru_lam[0])
    bias = jnp.concatenate([b_mgate[0], jnp.zeros((LANES - n_gate,), F32)])[None, :]
    mls_cm = _mlstm_call(q, k, v, o, gt, qc, kc, vc, gtc, bias, row(mlstm_norm))

    w_out0 = w_out[0].astype(BF16)
    x2 = _outproj_call(x1, lru_lat, mls_cm, w_out0[:d_lru], w_out0[d_lru:], lat(5))

    up2, down2 = ffn2_w_up[0], ffn2_w_down[0]
    out = _ffn_call(x2.reshape(b * t, d), row(ffn2_norm), lat(6), lat(7), lat(8), up2, down2,
                    final_norm[None, :], rows_per_mod=t, final_norm=True, name="ffn_post")
    return out.reshape(b, t, d)
```

```python
import functools

import jax
import jax.numpy as jnp
from jax import lax
from jax.experimental import pallas as pl
from jax.experimental.pallas import tpu as pltpu

F32 = jnp.float32
BF16 = jnp.bfloat16

GRID_W = 64
LRU_BLOCKS = 8
CONV_W = 4
LRU_C = 8.0
M_HEADS = 8
CHUNK = 128
N_MOD = 9
EPS = 1e-6
HALF = 0.5

LANES = 128
SUBLANES = 8
VMEM_LIMIT_BYTES = 58 * 1024 * 1024

FFN_TM = 1024
FFN_TF = 256
FFN_EDGE_ROWS = 256
PROJ_ROWS = 256
PROJ_TM = 512
LRU_GATE_ROWS = 512
ADA_TN = 1024
WSPLIT_TN = 512


def _cparams(sem):
    return pltpu.CompilerParams(dimension_semantics=sem, vmem_limit_bytes=VMEM_LIMIT_BYTES)


def _sigmoid(z):
    return 0.5 * jnp.tanh(0.5 * z) + 0.5


def _rms(x, g):
    return x * lax.rsqrt(jnp.mean(x * x, axis=-1, keepdims=True) + EPS) * g


def _ada_body(c_ref, w_ref, b_ref, o_ref):
    s = c_ref[...]
    s = s * jax.nn.sigmoid(s)
    o_ref[...] = jnp.dot(s, w_ref[...], preferred_element_type=F32) + b_ref[...]


def _ada_call(cc, w, b):
    rows, d = cc.shape
    n = w.shape[1]
    return pl.pallas_call(
        _ada_body,
        grid=(n // ADA_TN,),
        in_specs=[pl.BlockSpec((rows, d), lambda j: (0, 0)),
                  pl.BlockSpec((d, ADA_TN), lambda j: (0, j)),
                  pl.BlockSpec((1, ADA_TN), lambda j: (0, j))],
        out_specs=pl.BlockSpec((rows, ADA_TN), lambda j: (0, j)),
        out_shape=jax.ShapeDtypeStruct((rows, n), F32),
        compiler_params=_cparams(("arbitrary",)),
        name="ada_mod",
    )(cc, w, b)


def _ffn_body(x_ref, g_ref, sh_ref, sc_ref, gate_ref, wg_ref, wu_ref, wd_ref, fin_ref, o_ref,
              h_scr, *, final_norm):
    j = pl.program_id(1)
    last = pl.num_programs(1) - 1
    tm = x_ref.shape[0]
    row_blocks = [slice(r0, r0 + min(FFN_EDGE_ROWS, tm)) for r0 in range(0, tm, min(FFN_EDGE_ROWS, tm))]

    def contrib(h):
        g = jnp.dot(h, wg_ref[...].astype(BF16), preferred_element_type=F32)
        u = jnp.dot(h, wu_ref[...].astype(BF16), preferred_element_type=F32)
        a = (g * jax.nn.sigmoid(g) * u).astype(BF16)
        return jnp.dot(a, wd_ref[...].astype(BF16), preferred_element_type=F32)

    @pl.when(j == 0)
    def _():
        for rows in row_blocks:
            h = (_rms(x_ref[rows, :], g_ref[...]) * (1.0 + sc_ref[0]) + sh_ref[0]).astype(BF16)
            h_scr[rows, :] = h
            o_ref[rows, :] = contrib(h)

    @pl.when((j > 0) & (j < last))
    def _():
        o_ref[...] += contrib(h_scr[...])

    @pl.when(j == last)
    def _():
        for rows in row_blocks:
            y = x_ref[rows, :] + HALF * gate_ref[0] * (o_ref[rows, :] + contrib(h_scr[rows, :]))
            if final_norm:
                y = _rms(y, fin_ref[...])
            o_ref[rows, :] = y


def _ffn_call(x2d, norm_g, shift, scale, gate, w_up, w_down, fin_g, *, rows_per_mod, final_norm, name):
    m, d = x2d.shape
    f = w_down.shape[0]
    tm = min(FFN_TM, m)
    tiles_per_mod = rows_per_mod // tm
    nf = f // FFN_TF
    mod_spec = pl.BlockSpec((1, 1, d), lambda i, j: (i // tiles_per_mod, 0, 0))
    vec_spec = pl.BlockSpec((1, d), lambda i, j: (0, 0))
    return pl.pallas_call(
        functools.partial(_ffn_body, final_norm=final_norm),
        grid=(m // tm, nf),
        in_specs=[pl.BlockSpec((tm, d), lambda i, j: (i, 0)),
                  vec_spec, mod_spec, mod_spec, mod_spec,
                  pl.BlockSpec((d, FFN_TF), lambda i, j: (0, j)),
                  pl.BlockSpec((d, FFN_TF), lambda i, j: (0, j + nf)),
                  pl.BlockSpec((FFN_TF, d), lambda i, j: (j, 0)),
                  vec_spec],
        out_specs=pl.BlockSpec((tm, d), lambda i, j: (i, 0)),
        out_shape=jax.ShapeDtypeStruct((m, d), F32),
        scratch_shapes=[pltpu.VMEM((tm, d), BF16)],
        compiler_params=_cparams(("parallel", "arbitrary")),
        name=name,
    )(x2d, norm_g, shift, scale, gate, w_up, w_up, w_down, fin_g)


def _wsplit_body(w_ref, a_ref, b_ref, *, n_valid, n_a_blocks):
    j = pl.program_id(0)
    col = j * WSPLIT_TN + lax.broadcasted_iota(jnp.int32, w_ref.shape, 1)
    w = jnp.where(col < n_valid, w_ref[...], 0.0).astype(BF16)

    @pl.when(j < n_a_blocks)
    def _():
        a_ref[...] = w

    @pl.when(j >= n_a_blocks)
    def _():
        b_ref[...] = w


def _split_w_in(w, n_a):
    d, n = w.shape
    assert n_a % WSPLIT_TN == 0
    nb = pl.cdiv(n, WSPLIT_TN)
    na = n_a // WSPLIT_TN
    return pl.pallas_call(
        functools.partial(_wsplit_body, n_valid=n, n_a_blocks=na),
        grid=(nb,),
        in_specs=[pl.BlockSpec((d, WSPLIT_TN), lambda j: (0, j))],
        out_specs=[pl.BlockSpec((d, WSPLIT_TN), lambda j: (0, jnp.minimum(j, na - 1))),
                   pl.BlockSpec((d, WSPLIT_TN), lambda j: (0, jnp.maximum(j - na, 0)))],
        out_shape=[jax.ShapeDtypeStruct((d, n_a), BF16),
                   jax.ShapeDtypeStruct((d, (nb - na) * WSPLIT_TN), BF16)],
        compiler_params=_cparams(("arbitrary",)),
        name="w_in_split",
    )(w)


def _colmajor_perm(n_rows, n_cols, src0=0, n_src=None):
    assert n_rows & (n_rows - 1) == 0
    m = n_rows * n_cols
    n_src = m if n_src is None else n_src
    j = lax.broadcasted_iota(jnp.int32, (m, n_src), 0)
    i = lax.broadcasted_iota(jnp.int32, (m, n_src), 1) + src0
    src = (j & (n_rows - 1)) * n_cols + (j >> (n_rows.bit_length() - 1))
    return (i == src).astype(BF16)


def _proj_body(x_ref, g_ref, sh_ref, sc_ref, w_ref, *refs, widths, colmajor):
    out_refs, h_scr = refs[:-1], refs[-1]
    tm = h_scr.shape[0]
    rb = min(PROJ_ROWS, tm)
    norm = lambda x: (_rms(x, g_ref[...]) * (1.0 + sc_ref[0]) + sh_ref[0]).astype(BF16)
    if colmajor:
        r, c, d = x_ref.shape[1:]
        rows_per_block = rb // c
        hp = None
        for r0 in range(0, r, rows_per_block):
            h = norm(x_ref[0, r0:r0 + rows_per_block].reshape(rb, d))
            part = jnp.dot(_colmajor_perm(r, c, r0 * c, rb), h, preferred_element_type=F32)
            hp = part if hp is None else hp + part
        h_scr[...] = hp.astype(BF16)
    for r0 in range(0, tm, rb):
        rows = slice(r0, r0 + rb)
        h = h_scr[rows, :] if colmajor else norm(x_ref[0, rows, :])
        off = 0
        for o_ref, wdt in zip(out_refs, widths):
            o_ref[0, rows, :] = jnp.dot(h, w_ref[:, off:off + wdt],
                                        preferred_element_type=F32).astype(o_ref.dtype)
            off += wdt


def _proj_call(x, norm_g, shift, scale, w, widths, dtypes, *, tm, colmajor, name):
    b = x.shape[0]
    d = norm_g.shape[1]
    if colmajor:
        rows, gw = x.shape[1:3]
        cols = tm // rows
        assert cols == SUBLANES and gw % cols == 0
        n_tiles = gw // cols
        x_spec = pl.BlockSpec((1, rows, cols, d), lambda bi, i: (bi, 0, i, 0))
    else:
        n_tiles = x.shape[1] // tm
        x_spec = pl.BlockSpec((1, tm, d), lambda bi, i: (bi, i, 0))
    mod_spec = pl.BlockSpec((1, 1, d), lambda bi, i: (bi, 0, 0))
    return pl.pallas_call(
        functools.partial(_proj_body, widths=tuple(widths), colmajor=colmajor),
        grid=(b, n_tiles),
        in_specs=[x_spec,
                  pl.BlockSpec((1, d), lambda bi, i: (0, 0)),
                  mod_spec, mod_spec,
                  pl.BlockSpec(w.shape, lambda bi, i: (0, 0))],
        out_specs=[pl.BlockSpec((1, tm, wdt), lambda bi, i: (bi, i, 0)) for wdt in widths],
        out_shape=[jax.ShapeDtypeStruct((b, n_tiles * tm, wdt), dt) for wdt, dt in zip(widths, dtypes)],
        scratch_shapes=[pltpu.VMEM((tm, d), BF16)],
        compiler_params=_cparams(("parallel", "parallel")),
        name=name,
    )(x, norm_g, shift, scale, w)


def _lru_conv(x_ref, cw_ref, cb_ref, pad_scr, xc_scr, t):
    zeros = jnp.zeros((SUBLANES, LANES), F32)
    pad_scr[0:SUBLANES, :] = zeros
    pad_scr[SUBLANES:SUBLANES + t, :] = x_ref[0]
    pad_scr[SUBLANES + t:2 * SUBLANES + t, :] = zeros
    ch = min(LRU_GATE_ROWS, t)
    win_rows = ch + 2 * SUBLANES

    def body(c, carry):
        off = pl.multiple_of(c * ch, SUBLANES)
        win = pad_scr[pl.ds(off, win_rows), :]
        acc = cb_ref[...] + cw_ref[2:3, :] * win[SUBLANES:SUBLANES + ch]
        for k, shift in ((0, 2), (1, 1), (3, win_rows - 1)):
            acc = acc + cw_ref[k:k + 1, :] * pltpu.roll(win, shift, 0)[SUBLANES:SUBLANES + ch]
        xc_scr[pl.ds(off, ch), :] = acc
        return carry

    lax.fori_loop(0, t // ch, body, 0)


def _lru_gates(xc_scr, t, d, wr_ref, br_ref, wi_ref, bi_ref, lam_ref, a_scr, b_scr):
    ch = min(LRU_GATE_ROWS, t)
    log_a_unit = -LRU_C * jax.nn.softplus(-lam_ref[d:d + 1, :])
    wr = wr_ref[d, 0].astype(BF16)
    wi = wi_ref[d, 0].astype(BF16)

    def body(c, carry):
        off = pl.multiple_of(c * ch, SUBLANES)
        xc = xc_scr[pl.ds(off, ch), :]
        xb = xc.astype(BF16)
        r = _sigmoid(jnp.dot(xb, wr, preferred_element_type=F32) + br_ref[d:d + 1, :])
        i = _sigmoid(jnp.dot(xb, wi, preferred_element_type=F32) + bi_ref[d:d + 1, :])
        log_a = log_a_unit * r
        a = jnp.exp(log_a)
        a_scr[pl.ds(off, ch), :] = a
        b_scr[pl.ds(off, ch), :] = jnp.sqrt(-jnp.tanh(log_a) * (a * a + 1.0)) * i * xc
        return carry

    lax.fori_loop(0, t // ch, body, 0)


def _lru_scan(a_scr, b_scr, t, d, h0, h_scr):
    groups = t // SUBLANES
    row = lax.broadcasted_iota(jnp.int32, (SUBLANES, LANES), 0)

    def body(g, carry):
        gi = g if d == 0 else groups - 1 - g
        off = pl.multiple_of(gi * SUBLANES, SUBLANES)
        a = a_scr[pl.ds(off, SUBLANES), :]
        bv = b_scr[pl.ds(off, SUBLANES), :]
        for k in (1, 2, 4):
            shift = k if d == 0 else SUBLANES - k
            valid = (row >= k) if d == 0 else (row < SUBLANES - k)
            a_prev = pltpu.roll(a, shift, 0)
            b_prev = pltpu.roll(bv, shift, 0)
            bv = jnp.where(valid, a * b_prev + bv, bv)
            a = jnp.where(valid, a * a_prev, a)
        h = a * carry + bv
        if h_scr is not None:
            if d == 0:
                h_scr[pl.ds(off, SUBLANES), :] = h
            else:
                h_scr[pl.ds(off, SUBLANES), :] += h
        return h[SUBLANES - 1:SUBLANES, :] if d == 0 else h[0:1, :]

    return lax.fori_loop(0, groups, body, h0, unroll=4)


def _lru_body(lx_ref, lg_ref, lxc_ref, cw_ref, cb_ref, wr_ref, br_ref, wi_ref, bi_ref, lam_ref, o_ref,
              pad_scr, xc_scr, xcc_scr, a_scr, b_scr, h_scr):
    t = lx_ref.shape[1]
    tc = lxc_ref.shape[1]
    _lru_conv(lxc_ref, cw_ref, cb_ref, pad_scr, xcc_scr, tc)
    _lru_conv(lx_ref, cw_ref, cb_ref, pad_scr, xc_scr, t)
    gate_refs = (wr_ref, br_ref, wi_ref, bi_ref, lam_ref)
    for d in (0, 1):
        _lru_gates(xcc_scr, tc, d, *gate_refs, a_scr, b_scr)
        h0 = _lru_scan(a_scr, b_scr, tc, d, jnp.zeros((1, LANES), F32), None)
        _lru_gates(xc_scr, t, d, *gate_refs, a_scr, b_scr)
        _lru_scan(a_scr, b_scr, t, d, h0, h_scr)
    o_ref[0] = (jax.nn.gelu(lg_ref[0]) * h_scr[...]).astype(o_ref.dtype)


def _lru_call(lx, lg, lxc, conv_w, conv_b, w_r, b_r, w_i, b_i, lam):
    b, t, d_lru = lx.shape
    tc = lxc.shape[1]
    nb = d_lru // LANES
    assert w_r.shape == (2, nb, LANES, LANES)
    seq = lambda n: pl.BlockSpec((1, n, LANES), lambda bi, j: (bi, 0, j))
    vec = lambda n: pl.BlockSpec((n, LANES), lambda bi, j: (0, j))
    wblk = pl.BlockSpec((2, 1, LANES, LANES), lambda bi, j: (0, j, 0, 0))
    return pl.pallas_call(
        _lru_body,
        grid=(b, nb),
        in_specs=[seq(t), seq(t), seq(tc), vec(CONV_W), vec(1), wblk, vec(2), wblk, vec(2), vec(2)],
        out_specs=seq(t),
        out_shape=jax.ShapeDtypeStruct((b, t, d_lru), BF16),
        scratch_shapes=[pltpu.VMEM((t + 2 * SUBLANES, LANES), F32),
                        pltpu.VMEM((t, LANES), F32), pltpu.VMEM((tc, LANES), F32),
                        pltpu.VMEM((t, LANES), F32), pltpu.VMEM((t, LANES), F32),
                        pltpu.VMEM((t, LANES), F32)],
        compiler_params=_cparams(("parallel", "parallel")),
        name="rglru",
    )(lx, lg, lxc, conv_w, conv_b, w_r, b_r, w_i, b_i, lam)


FWD_GATE_LANES = 2 * M_HEADS


def _split3_dot(tri, x):
    hi = x.astype(BF16)
    r1 = x - hi.astype(F32)
    mid = r1.astype(BF16)
    lo = (r1 - mid.astype(F32)).astype(BF16)
    return (jnp.dot(tri, hi, preferred_element_type=F32) + jnp.dot(tri, mid, preferred_element_type=F32)
            + jnp.dot(tri, lo, preferred_element_type=F32))


def _cummax_rows(x, row, reverse):
    n = x.shape[0]
    k = 1
    while k < n:
        if reverse:
            x = jnp.where(row < n - k, jnp.maximum(x, pltpu.roll(x, n - k, 0)), x)
        else:
            x = jnp.where(row >= k, jnp.maximum(x, pltpu.roll(x, k, 0)), x)
        k *= 2
    return x


def _round_up_bf16(x):
    return (x + jnp.abs(x) * (2.0 ** -7)).astype(BF16)


def _mlstm_gate_prep(gates, kidx, bc_scr, cm_scr, xt_scr, tot_scr, gmax_scr):
    L = gates.shape[0]
    ti = lax.broadcasted_iota(jnp.int32, (L, L), 0)
    si = lax.broadcasted_iota(jnp.int32, (L, L), 1)
    row = lax.broadcasted_iota(jnp.int32, (L, LANES), 0)
    fwd = lax.broadcasted_iota(jnp.int32, (L, LANES), 1) < FWD_GATE_LANES
    lf = jax.nn.log_sigmoid(gates)
    bc_f = _split3_dot((si <= ti).astype(BF16), lf)
    bc_b = _split3_dot((si >= ti).astype(BF16), lf)
    bc = jnp.where(fwd, bc_f, bc_b)
    tot = jnp.where(fwd[0:1], bc_f[L - 1:L], bc_b[0:1])
    x = pltpu.roll(gates, SUBLANES, 1) - bc
    cm = jnp.where(fwd, _cummax_rows(x, row, False), _cummax_rows(x, row, True))
    gmax = jnp.max(tot + x, axis=0, keepdims=True)
    xt = x.T
    bc_scr[kidx] = bc
    cm_scr[kidx] = cm
    xt_scr[kidx, 0:SUBLANES, :] = xt[SUBLANES:2 * SUBLANES]
    xt_scr[kidx, SUBLANES:2 * SUBLANES, :] = xt[3 * SUBLANES:4 * SUBLANES]
    tot_scr[kidx] = jnp.broadcast_to(tot, (SUBLANES, LANES))
    gmax_scr[kidx] = jnp.broadcast_to(gmax, (SUBLANES, LANES))


def _mlstm_chunk(q, v, kidx, step, d, scr, need_h, dk):
    bc_scr, cm_scr, xt_scr, kt_scr, tot_scr, m_in_scr, m_out_scr, s_scr = scr
    L = q.shape[0]
    ti = lax.broadcasted_iota(jnp.int32, (L, L), 0)
    si = lax.broadcasted_iota(jnp.int32, (L, L), 1)
    causal = (si <= ti) if d == 0 else (si >= ti)
    lane = lax.broadcasted_iota(jnp.int32, (L, LANES), 1)
    head0_rows = lax.broadcasted_iota(jnp.int32, (LANES, L), 0) < dk
    srow = lax.broadcasted_iota(jnp.int32, (LANES, 4 * LANES), 0)
    scol = lax.broadcasted_iota(jnp.int32, (LANES, 4 * LANES), 1)
    own_block = (srow < dk) == (scol < 2 * LANES)
    k_t = kt_scr[kidx]
    tot_v, m_in_v, m_out_v = tot_scr[kidx], m_in_scr[step], m_out_scr[step]
    ones = jnp.ones((L, LANES), BF16)
    v1 = jnp.concatenate([v[:, :LANES], ones, v[:, LANES:], ones], axis=1)
    cf0 = FWD_GATE_LANES * d + SUBLANES
    pick = lambda vals, e: vals[0:1, cf0 + e:cf0 + e + 1]
    x_rows = [xt_scr[kidx, SUBLANES * d + e:SUBLANES * d + e + 1, :] for e in (0, 1)]
    s_old = s_scr[d]
    hs = []
    if need_h:
        bc = bc_scr[kidx]
        sel_lane = lax.broadcasted_iota(jnp.int32, (LANES, 2 * LANES), 0)
        sel_col = lax.broadcasted_iota(jnp.int32, (LANES, 2 * LANES), 1)
        sel = (sel_lane == jnp.where(sel_col < LANES, cf0, cf0 + 1)).astype(BF16)
        mx_all = _round_up_bf16(jnp.maximum(m_in_v[0:1, :], cm_scr[kidx]))
        mx_tiles = jnp.dot(mx_all, sel, preferred_element_type=F32)
        q_heads = [jnp.where((lane >= e * dk) & (lane < (e + 1) * dk), q, jnp.zeros_like(q)) for e in (0, 1)]
        qk = jnp.dot(jnp.concatenate(q_heads, axis=0), k_t, preferred_element_type=F32)
        qs = jnp.dot(q, s_old.astype(BF16), preferred_element_type=F32)
        for e in (0, 1):
            cf = cf0 + e
            mx = mx_tiles[:, e * LANES:(e + 1) * LANES]
            w = qk[e * L:(e + 1) * L] * jnp.exp(jnp.where(causal, x_rows[e] - mx, -jnp.inf))
            s_inter = jnp.exp(pick(m_in_v, e) - mx)
            wv = jnp.dot(w.astype(BF16), v1[:, 2 * e * LANES:2 * (e + 1) * LANES],
                         preferred_element_type=F32)
            qs_e = qs[:, 2 * e * LANES:2 * (e + 1) * LANES]
            num = s_inter * qs_e[:, :LANES] + wv[:, :LANES]
            den = s_inter * qs_e[:, LANES:] + wv[:, LANES:]
            m_row = bc[:, cf:cf + 1] + mx[:, cf:cf + 1]
            hs.append(num * (1.0 / jnp.maximum(jnp.abs(den[:, cf:cf + 1]), jnp.exp(-m_row))))
    wg_rows = [jnp.exp(pick(tot_v, e) + x_rows[e] - pick(m_out_v, e)) for e in (0, 1)]
    decays = [jnp.exp(pick(tot_v, e) + pick(m_in_v, e) - pick(m_out_v, e)) for e in (0, 1)]
    kw = (k_t.astype(F32) * jnp.where(head0_rows, wg_rows[0], wg_rows[1])).astype(BF16)
    upd = jnp.dot(kw, v1, preferred_element_type=F32)
    s_scr[d] = jnp.where(srow < dk, decays[0], decays[1]) * s_old + jnp.where(own_block, upd, 0.0)
    return hs


def _mlstm_body(q_ref, k_ref, v_ref, o_ref, gt_ref, qc_ref, kc_ref, vc_ref, gtc_ref, bias_ref, gain_ref,
                out_ref, hs_scr, bc_scr, cm_scr, xt_scr, kt_scr, tot_scr, gmax_scr, m_in_scr, m_out_scr,
                s_scr, *, dk):
    t = q_ref.shape[1]
    tc = qc_ref.shape[1]
    L = CHUNK
    nc, ncc = t // L, tc // L
    n_steps = nc + ncc
    shift = (LANES - 2 * pl.program_id(1)) % LANES
    bias = pltpu.roll(jnp.broadcast_to(bias_ref[...], (SUBLANES, LANES)), shift, 1)[0:1, :]
    prep_scr = (bc_scr, cm_scr, xt_scr, tot_scr, gmax_scr)
    state_scr = (bc_scr, cm_scr, xt_scr, kt_scr, tot_scr, m_in_scr, m_out_scr, s_scr)
    scaled_t = lambda k: (k.astype(F32).T * (dk ** -0.5)).astype(BF16)

    for c in range(ncc):
        rows = slice(c * L, (c + 1) * L)
        _mlstm_gate_prep(pltpu.roll(gtc_ref[0, rows, :], shift, 1) + bias, c, *prep_scr)
        kt_scr[c] = scaled_t(kc_ref[0, rows, :])

    def prep(c, carry):
        rows = pl.ds(pl.multiple_of(c * L, L), L)
        _mlstm_gate_prep(pltpu.roll(gt_ref[0, rows, :], shift, 1) + bias, c + ncc, *prep_scr)
        kt_scr[c + ncc] = scaled_t(k_ref[0, rows, :])
        return carry

    lax.fori_loop(0, nc, prep, 0, unroll=2)

    fwd_id = lambda i: i
    bwd_id = lambda i: (ncc - 1 - i) if i < ncc else (n_steps - 1 - (i - ncc))
    fwd8 = lax.broadcasted_iota(jnp.int32, (SUBLANES, LANES), 1) < FWD_GATE_LANES
    m = jnp.zeros((SUBLANES, LANES), F32)
    for i in range(n_steps):
        tot = jnp.where(fwd8, tot_scr[fwd_id(i)], tot_scr[bwd_id(i)])
        gmax = jnp.where(fwd8, gmax_scr[fwd_id(i)], gmax_scr[bwd_id(i)])
        m_in_scr[i] = m
        m = jnp.maximum(tot + m, gmax)
        m_out_scr[i] = m

    s_scr[...] = jnp.zeros_like(s_scr)
    hs_scr[...] = jnp.zeros_like(hs_scr)
    for i in range(ncc):
        for d, cid in ((0, fwd_id(i)), (1, bwd_id(i))):
            rows = slice(cid * L, (cid + 1) * L)
            _mlstm_chunk(qc_ref[0, rows, :], vc_ref[0, rows, :], cid, i, d, state_scr, False, dk)

    def body(j, carry):
        for d in (0, 1):
            cj = j if d == 0 else nc - 1 - j
            rows = pl.ds(pl.multiple_of(cj * L, L), L)
            hs = _mlstm_chunk(q_ref[0, rows, :], v_ref[0, rows, :], cj + ncc, j + ncc, d, state_scr, True, dk)
            hs_scr[rows, :] += jnp.concatenate(hs, axis=1)
        return carry

    lax.fori_loop(0, nc, body, 0, unroll=4)

    def fin(c, carry):
        rows = pl.ds(pl.multiple_of(c * L, L), L)
        hh = hs_scr[rows, :]
        outs = []
        for e in (0, 1):
            x = hh[:, e * LANES:(e + 1) * LANES]
            outs.append(x * lax.rsqrt(jnp.mean(x * x, axis=-1, keepdims=True) + EPS))
        y = jnp.concatenate(outs, axis=1) * gain_ref[...] * _sigmoid(o_ref[0, rows, :])
        out_ref[0, rows, :] = y.astype(out_ref.dtype)
        return carry

    lax.fori_loop(0, nc, fin, 0, unroll=2)


def _mlstm_call(q, k, v, o, gt, qc, kc, vc, gtc, bias, gain):
    b, t, d_qk = q.shape
    tc = qc.shape[1]
    d_v = v.shape[2]
    pairs = M_HEADS // 2
    dk = d_qk // M_HEADS
    assert d_qk // pairs == LANES and d_v // pairs == 2 * LANES and CHUNK == LANES
    n_chunks = (t + tc) // CHUNK
    seq = lambda n, w: pl.BlockSpec((1, n, w), lambda bi, p: (bi, 0, p))
    allg = lambda n: pl.BlockSpec((1, n, LANES), lambda bi, p: (bi, 0, 0))
    chunk_f32 = lambda rows, cols: pltpu.VMEM((n_chunks, rows, cols), F32)
    return pl.pallas_call(
        functools.partial(_mlstm_body, dk=dk),
        grid=(b, pairs),
        in_specs=[seq(t, LANES), seq(t, LANES), seq(t, 2 * LANES), seq(t, 2 * LANES), allg(t),
                  seq(tc, LANES), seq(tc, LANES), seq(tc, 2 * LANES), allg(tc),
                  pl.BlockSpec((1, LANES), lambda bi, p: (0, 0)),
                  pl.BlockSpec((1, 2 * LANES), lambda bi, p: (0, p))],
        out_specs=seq(t, 2 * LANES),
        out_shape=jax.ShapeDtypeStruct((b, t, d_v), F32),
        scratch_shapes=[pltpu.VMEM((t, 2 * LANES), F32),
                        chunk_f32(CHUNK, LANES), chunk_f32(CHUNK, LANES),
                        chunk_f32(2 * SUBLANES, CHUNK),
                        pltpu.VMEM((n_chunks, LANES, CHUNK), BF16),
                        chunk_f32(SUBLANES, LANES), chunk_f32(SUBLANES, LANES),
                        chunk_f32(SUBLANES, LANES), chunk_f32(SUBLANES, LANES),
                        pltpu.VMEM((2, LANES, 4 * LANES), F32)],
        compiler_params=_cparams(("parallel", "parallel")),
        name="mlstm",
    )(q, k, v, o, gt, qc, kc, vc, gtc, bias, gain)


def _outproj_body(x_ref, lru_ref, mls_ref, wa_ref, wb_ref, g_ref, o_ref):
    w_cols, r, dm = mls_ref.shape[1:]
    m = mls_ref[0].reshape(w_cols * r, dm).astype(BF16)
    m = jnp.dot(_colmajor_perm(w_cols, r), m, preferred_element_type=F32).astype(BF16)
    y = (jnp.dot(lru_ref[0], wa_ref[...], preferred_element_type=F32)
         + jnp.dot(m, wb_ref[...], preferred_element_type=F32))
    o_ref[0] = x_ref[0] + g_ref[0] * y


def _outproj_call(x3d, lru, mls_cm, w_a, w_b, gate):
    b, t, d = x3d.shape
    da, dm = lru.shape[2], mls_cm.shape[2]
    rows = t // GRID_W
    r_tile = PROJ_TM // GRID_W
    assert r_tile == SUBLANES
    mls_view = mls_cm.reshape(b, GRID_W, rows, dm)
    return pl.pallas_call(
        _outproj_body,
        grid=(b, t // PROJ_TM),
        in_specs=[pl.BlockSpec((1, PROJ_TM, d), lambda bi, i: (bi, i, 0)),
                  pl.BlockSpec((1, PROJ_TM, da), lambda bi, i: (bi, i, 0)),
                  pl.BlockSpec((1, GRID_W, r_tile, dm), lambda bi, i: (bi, 0, i, 0)),
                  pl.BlockSpec((da, d), lambda bi, i: (0, 0)),
                  pl.BlockSpec((dm, d), lambda bi, i: (0, 0)),
                  pl.BlockSpec((1, 1, d), lambda bi, i: (bi, 0, 0))],
        out_specs=pl.BlockSpec((1, PROJ_TM, d), lambda bi, i: (bi, i, 0)),
        out_shape=jax.ShapeDtypeStruct((b, t, d), F32),
        compiler_params=_cparams(("parallel", "parallel")),
        name="outproj",
    )(x3d, lru, mls_view, w_a, w_b, gate)


def kernel(x, c, ctx, c_ctx, ada_w, ada_b, ffn1_norm, ffn1_w_up, ffn1_w_down, mix_norm, w_in, b_mgate, lru_conv_w, lru_conv_b, lru_w_r, lru_b_r, lru_w_i, lru_b_i, lru_lam, mlstm_norm, w_out, ffn2_norm, ffn2_w_up, ffn2_w_down, final_norm):
    b, t, d = x.shape
    tc = ctx.shape[1]
    assert ada_w.shape[0] == 1, "single-layer block only"
    assert t % GRID_W == 0 and t % PROJ_TM == 0
    rows = t // GRID_W
    d_lru = lru_conv_w.shape[2]
    d_mv = mlstm_norm.shape[1]
    d_mqk = (w_in.shape[2] - 2 * d_lru - 2 * d_mv - 4 * M_HEADS) // 2

    pad = SUBLANES - b - 1
    cc = jnp.concatenate([c, c_ctx[None, :], jnp.zeros((pad, d), F32)], axis=0)
    mod = _ada_call(cc, ada_w[0], ada_b[0][None, :]).reshape(SUBLANES, N_MOD, d)
    lat = lambda i: mod[:b, i][:, None, :]
    cxt = lambda i: mod[b:b + 1, i][:, None, :]
    row = lambda v: v[0][None, :]

    up1, down1 = ffn1_w_up[0], ffn1_w_down[0]
    x1 = _ffn_call(x.reshape(b * t, d), row(ffn1_norm), lat(0), lat(1), lat(2), up1, down1,
                   row(ffn1_norm), rows_per_mod=t, final_norm=False, name="ffn_pre")
    ctx1 = _ffn_call(ctx.reshape(b * tc, d), row(ffn1_norm), cxt(0), cxt(1), cxt(2), up1, down1,
                     row(ffn1_norm), rows_per_mod=b * tc, final_norm=False, name="ffn_pre_ctx")
    x1 = x1.reshape(b, t, d)
    ctx1 = ctx1.reshape(b, tc, d)

    n_gate = 4 * M_HEADS
    w_a, w_b = _split_w_in(w_in[0], 2 * d_lru)
    widths_a, dtypes_a = (d_lru, d_lru), (F32, F32)
    widths_b, dtypes_b = (d_mqk, d_mqk, d_mv, d_mv, LANES), (BF16, BF16, BF16, F32, F32)
    mixn = row(mix_norm)
    lx, lg = _proj_call(x1, mixn, lat(3), lat(4), w_a, widths_a, dtypes_a, tm=PROJ_TM, colmajor=False,
                        name="proj_lru")
    q, k, v, o, gt = _proj_call(x1.reshape(b, rows, GRID_W, d), mixn, lat(3), lat(4), w_b, widths_b,
                                dtypes_b, tm=PROJ_TM, colmajor=True, name="proj_mlstm")
    ctx_mod = lambda i: jnp.broadcast_to(cxt(i), (b, 1, d))
    lxc, _ = _proj_call(ctx1, mixn, ctx_mod(3), ctx_mod(4), w_a, widths_a, dtypes_a, tm=tc, colmajor=False,
                        name="proj_lru_ctx")
    qc, kc, vc, _, gtc = _proj_call(ctx1, mixn, ctx_mod(3), ctx_mod(4), w_b, widths_b, dtypes_b, tm=tc,
                                    colmajor=False, name="proj_mlstm_ctx")

    lru_lat = _lru_call(lx, lg, lxc, lru_conv_w[0], row(lru_conv_b), lru_w_r[0], lru_b_r[0], lru_w_i[0],
                        lru_b_i[0], lru_lam[0])
    bias = jnp.concatenate([b_mgate[0], jnp.zeros((LANES - n_gate,), F32)])[None, :]
    mls_cm = _mlstm_call(q, k, v, o, gt, qc, kc, vc, gtc, bias, row(mlstm_norm))

    w_out0 = w_out[0].astype(BF16)
    x2 = _outproj_call(x1, lru_lat, mls_cm, w_out0[:d_lru], w_out0[d_lru:], lat(5))

    up2, down2 = ffn2_w_up[0], ffn2_w_down[0]
    out = _ffn_call(x2.reshape(b * t, d), row(ffn2_norm), lat(6), lat(7), lat(8), up2, down2,
                    final_norm[None, :], rows_per_mod=t, final_norm=True, name="ffn_post")
    return out.reshape(b, t, d)
```

```python
import functools

import jax
import jax.numpy as jnp
from jax import lax
from jax.experimental import pallas as pl
from jax.experimental.pallas import tpu as pltpu

F32 = jnp.float32
BF16 = jnp.bfloat16

GRID_W = 64
LRU_BLOCKS = 8
CONV_W = 4
LRU_C = 8.0
M_HEADS = 8
CHUNK = 128
N_MOD = 9
EPS = 1e-6
HALF = 0.5

LANES = 128
SUBLANES = 8
VMEM_LIMIT_BYTES = 58 * 1024 * 1024

FFN_TM = 1024
FFN_TF = 256
FFN_EDGE_ROWS = 256
PROJ_ROWS = 256
PROJ_TM = 512
LRU_GATE_ROWS = 512
LRU_SEQ_SCAN_MAX = 64
ADA_TN = 1024
WSPLIT_TN = 512


def _cparams(sem):
    return pltpu.CompilerParams(dimension_semantics=sem, vmem_limit_bytes=VMEM_LIMIT_BYTES)


def _sigmoid(z):
    return 0.5 * jnp.tanh(0.5 * z) + 0.5


def _rms(x, g):
    return x * lax.rsqrt(jnp.mean(x * x, axis=-1, keepdims=True) + EPS) * g


def _ada_body(c_ref, w_ref, b_ref, o_ref):
    s = c_ref[...]
    s = s * jax.nn.sigmoid(s)
    o_ref[...] = jnp.dot(s, w_ref[...], preferred_element_type=F32) + b_ref[...]


def _ada_call(cc, w, b):
    rows, d = cc.shape
    n = w.shape[1]
    return pl.pallas_call(
        _ada_body,
        grid=(n // ADA_TN,),
        in_specs=[pl.BlockSpec((rows, d), lambda j: (0, 0)),
                  pl.BlockSpec((d, ADA_TN), lambda j: (0, j)),
                  pl.BlockSpec((1, ADA_TN), lambda j: (0, j))],
        out_specs=pl.BlockSpec((rows, ADA_TN), lambda j: (0, j)),
        out_shape=jax.ShapeDtypeStruct((rows, n), F32),
        compiler_params=_cparams(("arbitrary",)),
        name="ada_mod",
    )(cc, w, b)


def _ffn_body(x_ref, g_ref, sh_ref, sc_ref, gate_ref, wg_ref, wu_ref, wd_ref, fin_ref, o_ref,
              h_scr, *, final_norm):
    j = pl.program_id(1)
    last = pl.num_programs(1) - 1
    tm = x_ref.shape[0]
    row_blocks = [slice(r0, r0 + min(FFN_EDGE_ROWS, tm)) for r0 in range(0, tm, min(FFN_EDGE_ROWS, tm))]

    def contrib(h):
        g = jnp.dot(h, wg_ref[...].astype(BF16), preferred_element_type=F32)
        u = jnp.dot(h, wu_ref[...].astype(BF16), preferred_element_type=F32)
        a = (g * jax.nn.sigmoid(g) * u).astype(BF16)
        return jnp.dot(a, wd_ref[...].astype(BF16), preferred_element_type=F32)

    @pl.when(j == 0)
    def _():
        for rows in row_blocks:
            h = (_rms(x_ref[rows, :], g_ref[...]) * (1.0 + sc_ref[0]) + sh_ref[0]).astype(BF16)
            h_scr[rows, :] = h
            o_ref[rows, :] = contrib(h)

    @pl.when((j > 0) & (j < last))
    def _():
        o_ref[...] += contrib(h_scr[...])

    @pl.when(j == last)
    def _():
        for rows in row_blocks:
            y = x_ref[rows, :] + HALF * gate_ref[0] * (o_ref[rows, :] + contrib(h_scr[rows, :]))
            if final_norm:
                y = _rms(y, fin_ref[...])
            o_ref[rows, :] = y


def _ffn_call(x2d, norm_g, shift, scale, gate, w_up, w_down, fin_g, *, rows_per_mod, final_norm, name):
    m, d = x2d.shape
    f = w_down.shape[0]
    tm = min(FFN_TM, m)
    tiles_per_mod = rows_per_mod // tm
    nf = f // FFN_TF
    mod_spec = pl.BlockSpec((1, 1, d), lambda i, j: (i // tiles_per_mod, 0, 0))
    vec_spec = pl.BlockSpec((1, d), lambda i, j: (0, 0))
    return pl.pallas_call(
        functools.partial(_ffn_body, final_norm=final_norm),
        grid=(m // tm, nf),
        in_specs=[pl.BlockSpec((tm, d), lambda i, j: (i, 0)),
                  vec_spec, mod_spec, mod_spec, mod_spec,
                  pl.BlockSpec((d, FFN_TF), lambda i, j: (0, j)),
                  pl.BlockSpec((d, FFN_TF), lambda i, j: (0, j + nf)),
                  pl.BlockSpec((FFN_TF, d), lambda i, j: (j, 0)),
                  vec_spec],
        out_specs=pl.BlockSpec((tm, d), lambda i, j: (i, 0)),
        out_shape=jax.ShapeDtypeStruct((m, d), F32),
        scratch_shapes=[pltpu.VMEM((tm, d), BF16)],
        compiler_params=_cparams(("parallel", "arbitrary")),
        name=name,
    )(x2d, norm_g, shift, scale, gate, w_up, w_up, w_down, fin_g)


def _wsplit_body(wt_ref, a_ref, b_ref, *, n_valid, n_a_blocks):
    j = pl.program_id(0)
    feat = j * WSPLIT_TN + lax.broadcasted_iota(jnp.int32, wt_ref.shape, 0)
    w = jnp.where(feat < n_valid, wt_ref[...], 0.0).T.astype(BF16)

    @pl.when(j < n_a_blocks)
    def _():
        a_ref[...] = w

    @pl.when(j >= n_a_blocks)
    def _():
        b_ref[...] = w


def _split_w_in(wt, n_a):
    n, d = wt.shape
    assert n_a % WSPLIT_TN == 0
    nb = pl.cdiv(n, WSPLIT_TN)
    na = n_a // WSPLIT_TN
    return pl.pallas_call(
        functools.partial(_wsplit_body, n_valid=n, n_a_blocks=na),
        grid=(nb,),
        in_specs=[pl.BlockSpec((WSPLIT_TN, d), lambda j: (j, 0))],
        out_specs=[pl.BlockSpec((d, WSPLIT_TN), lambda j: (0, jnp.minimum(j, na - 1))),
                   pl.BlockSpec((d, WSPLIT_TN), lambda j: (0, jnp.maximum(j - na, 0)))],
        out_shape=[jax.ShapeDtypeStruct((d, n_a), BF16),
                   jax.ShapeDtypeStruct((d, (nb - na) * WSPLIT_TN), BF16)],
        compiler_params=_cparams(("arbitrary",)),
        name="w_in_split",
    )(wt)


def _colmajor_perm(n_rows, n_cols, src0=0, n_src=None):
    assert n_rows & (n_rows - 1) == 0
    m = n_rows * n_cols
    n_src = m if n_src is None else n_src
    j = lax.broadcasted_iota(jnp.int32, (m, n_src), 0)
    i = lax.broadcasted_iota(jnp.int32, (m, n_src), 1) + src0
    src = (j & (n_rows - 1)) * n_cols + (j >> (n_rows.bit_length() - 1))
    return (i == src).astype(BF16)


def _proj_body(x_ref, g_ref, sh_ref, sc_ref, w_ref, *refs, widths, colmajor):
    out_refs, h_scr = refs[:-1], refs[-1]
    tm = h_scr.shape[0]
    rb = min(PROJ_ROWS, tm)
    norm = lambda x: (_rms(x, g_ref[...]) * (1.0 + sc_ref[0]) + sh_ref[0]).astype(BF16)
    if colmajor:
        r, c, d = x_ref.shape[1:]
        rows_per_block = rb // c
        hp = None
        for r0 in range(0, r, rows_per_block):
            h = norm(x_ref[0, r0:r0 + rows_per_block].reshape(rb, d))
            part = jnp.dot(_colmajor_perm(r, c, r0 * c, rb), h, preferred_element_type=F32)
            hp = part if hp is None else hp + part
        h_scr[...] = hp.astype(BF16)
    for r0 in range(0, tm, rb):
        rows = slice(r0, r0 + rb)
        h = h_scr[rows, :] if colmajor else norm(x_ref[0, rows, :])
        off = 0
        for o_ref, wdt in zip(out_refs, widths):
            o_ref[0, rows, :] = jnp.dot(h, w_ref[:, off:off + wdt],
                                        preferred_element_type=F32).astype(o_ref.dtype)
            off += wdt


def _proj_call(x, norm_g, shift, scale, w, widths, dtypes, *, tm, colmajor, name):
    b = x.shape[0]
    d = norm_g.shape[1]
    if colmajor:
        rows, gw = x.shape[1:3]
        cols = tm // rows
        assert cols == SUBLANES and gw % cols == 0
        n_tiles = gw // cols
        x_spec = pl.BlockSpec((1, rows, cols, d), lambda bi, i: (bi, 0, i, 0))
    else:
        n_tiles = x.shape[1] // tm
        x_spec = pl.BlockSpec((1, tm, d), lambda bi, i: (bi, i, 0))
    mod_spec = pl.BlockSpec((1, 1, d), lambda bi, i: (bi, 0, 0))
    return pl.pallas_call(
        functools.partial(_proj_body, widths=tuple(widths), colmajor=colmajor),
        grid=(b, n_tiles),
        in_specs=[x_spec,
                  pl.BlockSpec((1, d), lambda bi, i: (0, 0)),
                  mod_spec, mod_spec,
                  pl.BlockSpec(w.shape, lambda bi, i: (0, 0))],
        out_specs=[pl.BlockSpec((1, tm, wdt), lambda bi, i: (bi, i, 0)) for wdt in widths],
        out_shape=[jax.ShapeDtypeStruct((b, n_tiles * tm, wdt), dt) for wdt, dt in zip(widths, dtypes)],
        scratch_shapes=[pltpu.VMEM((tm, d), BF16)],
        compiler_params=_cparams(("parallel", "parallel")),
        name=name,
    )(x, norm_g, shift, scale, w)


def _lru_conv(x_ref, cw_ref, cb_ref, pad_scr, xc_scr, t):
    zeros = jnp.zeros((SUBLANES, LANES), F32)
    pad_scr[0:SUBLANES, :] = zeros
    pad_scr[SUBLANES:SUBLANES + t, :] = x_ref[0]
    pad_scr[SUBLANES + t:2 * SUBLANES + t, :] = zeros
    ch = min(LRU_GATE_ROWS, t)
    win_rows = ch + 2 * SUBLANES

    def body(c, carry):
        off = pl.multiple_of(c * ch, SUBLANES)
        win = pad_scr[pl.ds(off, win_rows), :]
        acc = cb_ref[...] + cw_ref[2:3, :] * win[SUBLANES:SUBLANES + ch]
        for k, shift in ((0, 2), (1, 1), (3, win_rows - 1)):
            acc = acc + cw_ref[k:k + 1, :] * pltpu.roll(win, shift, 0)[SUBLANES:SUBLANES + ch]
        xc_scr[pl.ds(off, ch), :] = acc
        return carry

    lax.fori_loop(0, t // ch, body, 0)


def _lru_gates(xc_scr, t, d, wr_ref, br_ref, wi_ref, bi_ref, lam_ref, a_scr, b_scr):
    ch = min(LRU_GATE_ROWS, t)
    half_unit = (-0.5 * LRU_C) * jax.nn.softplus(-lam_ref[d:d + 1, :])
    wr = (0.5 * wr_ref[d, 0]).astype(BF16)
    wi = (0.5 * wi_ref[d, 0]).astype(BF16)
    br = 0.5 * br_ref[d:d + 1, :]
    bi = 0.5 * bi_ref[d:d + 1, :]

    def body(c, carry):
        off = pl.multiple_of(c * ch, SUBLANES)
        xc = xc_scr[pl.ds(off, ch), :]
        xb = xc.astype(BF16)
        tr = jnp.tanh(jnp.dot(xb, wr, preferred_element_type=F32) + br)
        ti = jnp.tanh(jnp.dot(xb, wi, preferred_element_type=F32) + bi)
        log_a = half_unit * tr + half_unit
        a = jnp.exp(log_a)
        a_scr[pl.ds(off, ch), :] = a
        y = -jnp.tanh(log_a) * (a * a + 1.0)
        mult = jnp.where(y > 0.0, y * lax.rsqrt(y), 0.0)
        b_scr[pl.ds(off, ch), :] = (mult * xc) * (0.5 * ti + 0.5)
        return carry

    lax.fori_loop(0, t // ch, body, 0, unroll=min(2, t // ch))


def _lru_scan(a_ref, b_ref, t, d, h0, out_ref, accumulate=False):
    groups = t // SUBLANES
    row = lax.broadcasted_iota(jnp.int32, (SUBLANES, LANES), 0)

    def body(g, carry):
        gi = g if d == 0 else groups - 1 - g
        off = pl.multiple_of(gi * SUBLANES, SUBLANES)
        a = a_ref[pl.ds(off, SUBLANES), :]
        bv = b_ref[pl.ds(off, SUBLANES), :]
        for k in (1, 2, 4):
            shift = k if d == 0 else SUBLANES - k
            valid = (row >= k) if d == 0 else (row < SUBLANES - k)
            a_prev = pltpu.roll(a, shift, 0)
            b_prev = pltpu.roll(bv, shift, 0)
            bv = jnp.where(valid, a * b_prev + bv, bv)
            a = jnp.where(valid, a * a_prev, a)
        h = a * carry + bv
        if out_ref is not None:
            if accumulate:
                out_ref[pl.ds(off, SUBLANES), :] += h
            else:
                out_ref[pl.ds(off, SUBLANES), :] = h
        return h[SUBLANES - 1:SUBLANES, :] if d == 0 else h[0:1, :]

    return lax.fori_loop(0, groups, body, h0, unroll=min(4, groups))


def _lru_scan_planes(a_ref, b_ref, n, d, h0, out_ref, accumulate, levels):
    if n <= LRU_SEQ_SCAN_MAX:
        _lru_scan(a_ref, b_ref, n, d, h0, out_ref, accumulate)
        return
    g = n // SUBLANES
    p_scr, q_scr, s_scr = levels[0]
    order = list(range(SUBLANES)) if d == 0 else list(range(SUBLANES - 1, -1, -1))
    group_rows = SUBLANES * SUBLANES

    def up(v, carry):
        base = pl.multiple_of(v * group_rows, group_rows)
        rows = pl.ds(pl.multiple_of(v * SUBLANES, SUBLANES), SUBLANES)
        p = h = None
        for j in order:
            a = a_ref[pl.ds(base + j, SUBLANES, stride=SUBLANES), :]
            bv = b_ref[pl.ds(base + j, SUBLANES, stride=SUBLANES), :]
            if p is None:
                p, h = a, bv
            else:
                h = a * h + bv
                p = a * p
            p_scr[j, rows, :] = p
            q_scr[j, rows, :] = h
        return carry

    lax.fori_loop(0, g // SUBLANES, up, 0, unroll=2)

    last = order[-1]
    _lru_scan_planes(p_scr.at[last], q_scr.at[last], g, d, h0, s_scr, False, levels[1:])
    s = s_scr[...]
    row = lax.broadcasted_iota(jnp.int32, (g, LANES), 0)
    if d == 0:
        s_scr[...] = jnp.where(row == 0, h0, pltpu.roll(s, 1, 0))
    else:
        s_scr[...] = jnp.where(row == g - 1, h0, pltpu.roll(s, g - 1, 0))

    def down(v, carry):
        base = pl.multiple_of(v * group_rows, group_rows)
        rows = pl.ds(pl.multiple_of(v * SUBLANES, SUBLANES), SUBLANES)
        x = s_scr[rows, :]
        for j in range(SUBLANES):
            dst = pl.ds(base + j, SUBLANES, stride=SUBLANES)
            val = q_scr[j, rows, :] + p_scr[j, rows, :] * x
            if accumulate:
                val = val + out_ref[dst, :]
            out_ref[dst, :] = val
        return carry

    lax.fori_loop(0, g // SUBLANES, down, 0, unroll=2)


def _lru_body(lx_ref, lg_ref, lxc_ref, cw_ref, cb_ref, wr_ref, br_ref, wi_ref, bi_ref, lam_ref, o_ref,
              pad_scr, xc_scr, xcc_scr, a_scr, b_scr, h_scr, *level_scr):
    levels = [level_scr[i:i + 3] for i in range(0, len(level_scr), 3)]
    t = lx_ref.shape[1]
    tc = lxc_ref.shape[1]
    _lru_conv(lxc_ref, cw_ref, cb_ref, pad_scr, xcc_scr, tc)
    _lru_conv(lx_ref, cw_ref, cb_ref, pad_scr, xc_scr, t)
    gate_refs = (wr_ref, br_ref, wi_ref, bi_ref, lam_ref)
    for d in (0, 1):
        _lru_gates(xcc_scr, tc, d, *gate_refs, a_scr, b_scr)
        h0 = _lru_scan(a_scr, b_scr, tc, d, jnp.zeros((1, LANES), F32), None)
        _lru_gates(xc_scr, t, d, *gate_refs, a_scr, b_scr)
        _lru_scan_planes(a_scr, b_scr, t, d, h0, h_scr, d == 1, levels)
    o_ref[0] = (jax.nn.gelu(lg_ref[0]) * h_scr[...]).astype(o_ref.dtype)


def _lru_call(lx, lg, lxc, conv_w, conv_b, w_r, b_r, w_i, b_i, lam):
    b, t, d_lru = lx.shape
    tc = lxc.shape[1]
    nb = d_lru // LANES
    assert w_r.shape == (2, nb, LANES, LANES)
    seq = lambda n: pl.BlockSpec((1, n, LANES), lambda bi, j: (bi, 0, j))
    vec = lambda n: pl.BlockSpec((n, LANES), lambda bi, j: (0, j))
    wblk = pl.BlockSpec((2, 1, LANES, LANES), lambda bi, j: (0, j, 0, 0))
    level_scr = []
    n = t
    while n > LRU_SEQ_SCAN_MAX:
        assert n % (SUBLANES * SUBLANES) == 0
        n //= SUBLANES
        level_scr += [pltpu.VMEM((SUBLANES, n, LANES), F32), pltpu.VMEM((SUBLANES, n, LANES), F32),
                      pltpu.VMEM((n, LANES), F32)]
    return pl.pallas_call(
        _lru_body,
        grid=(b, nb),
        in_specs=[seq(t), seq(t), seq(tc), vec(CONV_W), vec(1), wblk, vec(2), wblk, vec(2), vec(2)],
        out_specs=seq(t),
        out_shape=jax.ShapeDtypeStruct((b, t, d_lru), BF16),
        scratch_shapes=[pltpu.VMEM((t + 2 * SUBLANES, LANES), F32),
                        pltpu.VMEM((t, LANES), F32), pltpu.VMEM((tc, LANES), F32),
                        pltpu.VMEM((t, LANES), F32), pltpu.VMEM((t, LANES), F32),
                        pltpu.VMEM((t, LANES), F32)] + level_scr,
        compiler_params=_cparams(("parallel", "parallel")),
        name="rglru",
    )(lx, lg, lxc, conv_w, conv_b, w_r, b_r, w_i, b_i, lam)


FWD_GATE_LANES = 2 * M_HEADS


def _split3_dot(tri, x):
    hi = x.astype(BF16)
    r1 = x - hi.astype(F32)
    mid = r1.astype(BF16)
    lo = (r1 - mid.astype(F32)).astype(BF16)
    return (jnp.dot(tri, hi, preferred_element_type=F32) + jnp.dot(tri, mid, preferred_element_type=F32)
            + jnp.dot(tri, lo, preferred_element_type=F32))


def _cummax_rows(x, row, reverse):
    n = x.shape[0]
    k = 1
    while k < n:
        if reverse:
            x = jnp.where(row < n - k, jnp.maximum(x, pltpu.roll(x, n - k, 0)), x)
        else:
            x = jnp.where(row >= k, jnp.maximum(x, pltpu.roll(x, k, 0)), x)
        k *= 2
    return x


def _round_up_bf16(x):
    return (x + jnp.abs(x) * (2.0 ** -7)).astype(BF16)


def _mlstm_gate_prep(gates, kidx, bc_scr, cm_scr, xt_scr, tot_scr, gmax_scr):
    L = gates.shape[0]
    ti = lax.broadcasted_iota(jnp.int32, (L, L), 0)
    si = lax.broadcasted_iota(jnp.int32, (L, L), 1)
    row = lax.broadcasted_iota(jnp.int32, (L, LANES), 0)
    fwd = lax.broadcasted_iota(jnp.int32, (L, LANES), 1) < FWD_GATE_LANES
    lf = jax.nn.log_sigmoid(gates)
    bc_f = _split3_dot((si <= ti).astype(BF16), lf)
    bc_b = _split3_dot((si >= ti).astype(BF16), lf)
    bc = jnp.where(fwd, bc_f, bc_b)
    tot = jnp.where(fwd[0:1], bc_f[L - 1:L], bc_b[0:1])
    x = pltpu.roll(gates, SUBLANES, 1) - bc
    cm = jnp.where(fwd, _cummax_rows(x, row, False), _cummax_rows(x, row, True))
    gmax = jnp.max(tot + x, axis=0, keepdims=True)
    xt = x.T
    bc_scr[kidx] = bc
    cm_scr[kidx] = cm
    xt_scr[kidx, 0:SUBLANES, :] = xt[SUBLANES:2 * SUBLANES]
    xt_scr[kidx, SUBLANES:2 * SUBLANES, :] = xt[3 * SUBLANES:4 * SUBLANES]
    tot_scr[kidx] = jnp.broadcast_to(tot, (SUBLANES, LANES))
    gmax_scr[kidx] = jnp.broadcast_to(gmax, (SUBLANES, LANES))


def _mlstm_chunk(q, v, kidx, step, d, scr, need_h, dk):
    bc_scr, cm_scr, xt_scr, kt_scr, tot_scr, m_in_scr, m_out_scr, s_scr = scr
    L = q.shape[0]
    ti = lax.broadcasted_iota(jnp.int32, (L, L), 0)
    si = lax.broadcasted_iota(jnp.int32, (L, L), 1)
    causal = (si <= ti) if d == 0 else (si >= ti)
    lane = lax.broadcasted_iota(jnp.int32, (L, LANES), 1)
    head0_rows = lax.broadcasted_iota(jnp.int32, (LANES, L), 0) < dk
    srow = lax.broadcasted_iota(jnp.int32, (LANES, 4 * LANES), 0)
    scol = lax.broadcasted_iota(jnp.int32, (LANES, 4 * LANES), 1)
    own_block = (srow < dk) == (scol < 2 * LANES)
    k_t = kt_scr[kidx]
    tot_v, m_in_v, m_out_v = tot_scr[kidx], m_in_scr[step], m_out_scr[step]
    ones = jnp.ones((L, LANES), BF16)
    v1 = jnp.concatenate([v[:, :LANES], ones, v[:, LANES:], ones], axis=1)
    cf0 = FWD_GATE_LANES * d + SUBLANES
    pick = lambda vals, e: vals[0:1, cf0 + e:cf0 + e + 1]
    x_rows = [xt_scr[kidx, SUBLANES * d + e:SUBLANES * d + e + 1, :] for e in (0, 1)]
    s_old = s_scr[d]
    hs = []
    if need_h:
        bc = bc_scr[kidx]
        sel_lane = lax.broadcasted_iota(jnp.int32, (LANES, 2 * LANES), 0)
        sel_col = lax.broadcasted_iota(jnp.int32, (LANES, 2 * LANES), 1)
        sel = (sel_lane == jnp.where(sel_col < LANES, cf0, cf0 + 1)).astype(BF16)
        mx_all = _round_up_bf16(jnp.maximum(m_in_v[0:1, :], cm_scr[kidx]))
        mx_tiles = jnp.dot(mx_all, sel, preferred_element_type=F32)
        q_heads = [jnp.where((lane >= e * dk) & (lane < (e + 1) * dk), q, jnp.zeros_like(q)) for e in (0, 1)]
        qk = jnp.dot(jnp.concatenate(q_heads, axis=0), k_t, preferred_element_type=F32)
        qs = jnp.dot(q, s_old.astype(BF16), preferred_element_type=F32)
        for e in (0, 1):
            cf = cf0 + e
            mx = mx_tiles[:, e * LANES:(e + 1) * LANES]
            w = qk[e * L:(e + 1) * L] * jnp.exp(jnp.where(causal, x_rows[e] - mx, -jnp.inf))
            s_inter = jnp.exp(pick(m_in_v, e) - mx)
            wv = jnp.dot(w.astype(BF16), v1[:, 2 * e * LANES:2 * (e + 1) * LANES],
                         preferred_element_type=F32)
            qs_e = qs[:, 2 * e * LANES:2 * (e + 1) * LANES]
            num = s_inter * qs_e[:, :LANES] + wv[:, :LANES]
            den = s_inter * qs_e[:, LANES:] + wv[:, LANES:]
            m_row = bc[:, cf:cf + 1] + mx[:, cf:cf + 1]
            hs.append(num * (1.0 / jnp.maximum(jnp.abs(den[:, cf:cf + 1]), jnp.exp(-m_row))))
    wg_rows = [jnp.exp(pick(tot_v, e) + x_rows[e] - pick(m_out_v, e)) for e in (0, 1)]
    decays = [jnp.exp(pick(tot_v, e) + pick(m_in_v, e) - pick(m_out_v, e)) for e in (0, 1)]
    kw = (k_t.astype(F32) * jnp.where(head0_rows, wg_rows[0], wg_rows[1])).astype(BF16)
    upd = jnp.dot(kw, v1, preferred_element_type=F32)
    s_scr[d] = jnp.where(srow < dk, decays[0], decays[1]) * s_old + jnp.where(own_block, upd, 0.0)
    return hs


def _mlstm_body(q_ref, k_ref, v_ref, o_ref, gt_ref, qc_ref, kc_ref, vc_ref, gtc_ref, bias_ref, gain_ref,
                out_ref, hs_scr, bc_scr, cm_scr, xt_scr, kt_scr, tot_scr, gmax_scr, m_in_scr, m_out_scr,
                s_scr, *, dk):
    t = q_ref.shape[1]
    tc = qc_ref.shape[1]
    L = CHUNK
    nc, ncc = t // L, tc // L
    n_steps = nc + ncc
    shift = (LANES - 2 * pl.program_id(1)) % LANES
    bias = pltpu.roll(jnp.broadcast_to(bias_ref[...], (SUBLANES, LANES)), shift, 1)[0:1, :]
    prep_scr = (bc_scr, cm_scr, xt_scr, tot_scr, gmax_scr)
    state_scr = (bc_scr, cm_scr, xt_scr, kt_scr, tot_scr, m_in_scr, m_out_scr, s_scr)
    scaled_t = lambda k: (k.astype(F32).T * (dk ** -0.5)).astype(BF16)

    for c in range(ncc):
        rows = slice(c * L, (c + 1) * L)
        _mlstm_gate_prep(pltpu.roll(gtc_ref[0, rows, :], shift, 1) + bias, c, *prep_scr)
        kt_scr[c] = scaled_t(kc_ref[0, rows, :])

    def prep(c, carry):
        rows = pl.ds(pl.multiple_of(c * L, L), L)
        _mlstm_gate_prep(pltpu.roll(gt_ref[0, rows, :], shift, 1) + bias, c + ncc, *prep_scr)
        kt_scr[c + ncc] = scaled_t(k_ref[0, rows, :])
        return carry

    lax.fori_loop(0, nc, prep, 0, unroll=2)

    fwd_id = lambda i: i
    bwd_id = lambda i: (ncc - 1 - i) if i < ncc else (n_steps - 1 - (i - ncc))
    fwd8 = lax.broadcasted_iota(jnp.int32, (SUBLANES, LANES), 1) < FWD_GATE_LANES
    m = jnp.zeros((SUBLANES, LANES), F32)
    for i in range(n_steps):
        tot = jnp.where(fwd8, tot_scr[fwd_id(i)], tot_scr[bwd_id(i)])
        gmax = jnp.where(fwd8, gmax_scr[fwd_id(i)], gmax_scr[bwd_id(i)])
        m_in_scr[i] = m
        m = jnp.maximum(tot + m, gmax)
        m_out_scr[i] = m

    s_scr[...] = jnp.zeros_like(s_scr)
    hs_scr[...] = jnp.zeros_like(hs_scr)
    for i in range(ncc):
        for d, cid in ((0, fwd_id(i)), (1, bwd_id(i))):
            rows = slice(cid * L, (cid + 1) * L)
            _mlstm_chunk(qc_ref[0, rows, :], vc_ref[0, rows, :], cid, i, d, state_scr, False, dk)

    def body(j, carry):
        for d in (0, 1):
            cj = j if d == 0 else nc - 1 - j
            rows = pl.ds(pl.multiple_of(cj * L, L), L)
            hs = _mlstm_chunk(q_ref[0, rows, :], v_ref[0, rows, :], cj + ncc, j + ncc, d, state_scr, True, dk)
            hs_scr[rows, :] += jnp.concatenate(hs, axis=1)
        return carry

    lax.fori_loop(0, nc, body, 0, unroll=4)

    def fin(c, carry):
        rows = pl.ds(pl.multiple_of(c * L, L), L)
        hh = hs_scr[rows, :]
        outs = []
        for e in (0, 1):
            x = hh[:, e * LANES:(e + 1) * LANES]
            outs.append(x * lax.rsqrt(jnp.mean(x * x, axis=-1, keepdims=True) + EPS))
        y = jnp.concatenate(outs, axis=1) * gain_ref[...] * _sigmoid(o_ref[0, rows, :])
        out_ref[0, rows, :] = y.astype(out_ref.dtype)
        return carry

    lax.fori_loop(0, nc, fin, 0, unroll=2)


def _mlstm_call(q, k, v, o, gt, qc, kc, vc, gtc, bias, gain):
    b, t, d_qk = q.shape
    tc = qc.shape[1]
    d_v = v.shape[2]
    pairs = M_HEADS // 2
    dk = d_qk // M_HEADS
    assert d_qk // pairs == LANES and d_v // pairs == 2 * LANES and CHUNK == LANES
    n_chunks = (t + tc) // CHUNK
    seq = lambda n, w: pl.BlockSpec((1, n, w), lambda bi, p: (bi, 0, p))
    allg = lambda n: pl.BlockSpec((1, n, LANES), lambda bi, p: (bi, 0, 0))
    chunk_f32 = lambda rows, cols: pltpu.VMEM((n_chunks, rows, cols), F32)
    return pl.pallas_call(
        functools.partial(_mlstm_body, dk=dk),
        grid=(b, pairs),
        in_specs=[seq(t, LANES), seq(t, LANES), seq(t, 2 * LANES), seq(t, 2 * LANES), allg(t),
                  seq(tc, LANES), seq(tc, LANES), seq(tc, 2 * LANES), allg(tc),
                  pl.BlockSpec((1, LANES), lambda bi, p: (0, 0)),
                  pl.BlockSpec((1, 2 * LANES), lambda bi, p: (0, p))],
        out_specs=seq(t, 2 * LANES),
        out_shape=jax.ShapeDtypeStruct((b, t, d_v), F32),
        scratch_shapes=[pltpu.VMEM((t, 2 * LANES), F32),
                        chunk_f32(CHUNK, LANES), chunk_f32(CHUNK, LANES),
                        chunk_f32(2 * SUBLANES, CHUNK),
                        pltpu.VMEM((n_chunks, LANES, CHUNK), BF16),
                        chunk_f32(SUBLANES, LANES), chunk_f32(SUBLANES, LANES),
                        chunk_f32(SUBLANES, LANES), chunk_f32(SUBLANES, LANES),
                        pltpu.VMEM((2, LANES, 4 * LANES), F32)],
        compiler_params=_cparams(("parallel", "parallel")),
        name="mlstm",
    )(q, k, v, o, gt, qc, kc, vc, gtc, bias, gain)


def _outproj_body(x_ref, lru_ref, mls_ref, wa_ref, wb_ref, g_ref, o_ref):
    w_cols, r, dm = mls_ref.shape[1:]
    m = mls_ref[0].reshape(w_cols * r, dm).astype(BF16)
    m = jnp.dot(_colmajor_perm(w_cols, r), m, preferred_element_type=F32).astype(BF16)
    y = (jnp.dot(lru_ref[0], wa_ref[...], preferred_element_type=F32)
         + jnp.dot(m, wb_ref[...], preferred_element_type=F32))
    o_ref[0] = x_ref[0] + g_ref[0] * y


def _outproj_call(x3d, lru, mls_cm, w_a, w_b, gate):
    b, t, d = x3d.shape
    da, dm = lru.shape[2], mls_cm.shape[2]
    rows = t // GRID_W
    r_tile = PROJ_TM // GRID_W
    assert r_tile == SUBLANES
    mls_view = mls_cm.reshape(b, GRID_W, rows, dm)
    return pl.pallas_call(
        _outproj_body,
        grid=(b, t // PROJ_TM),
        in_specs=[pl.BlockSpec((1, PROJ_TM, d), lambda bi, i: (bi, i, 0)),
                  pl.BlockSpec((1, PROJ_TM, da), lambda bi, i: (bi, i, 0)),
                  pl.BlockSpec((1, GRID_W, r_tile, dm), lambda bi, i: (bi, 0, i, 0)),
                  pl.BlockSpec((da, d), lambda bi, i: (0, 0)),
                  pl.BlockSpec((dm, d), lambda bi, i: (0, 0)),
                  pl.BlockSpec((1, 1, d), lambda bi, i: (bi, 0, 0))],
        out_specs=pl.BlockSpec((1, PROJ_TM, d), lambda bi, i: (bi, i, 0)),
        out_shape=jax.ShapeDtypeStruct((b, t, d), F32),
        compiler_params=_cparams(("parallel", "parallel")),
        name="outproj",
    )(x3d, lru, mls_view, w_a, w_b, gate)


def kernel(x, c, ctx, c_ctx, ada_w, ada_b, ffn1_norm, ffn1_w_up, ffn1_w_down, mix_norm, w_in, b_mgate, lru_conv_w, lru_conv_b, lru_w_r, lru_b_r, lru_w_i, lru_b_i, lru_lam, mlstm_norm, w_out, ffn2_norm, ffn2_w_up, ffn2_w_down, final_norm):
    b, t, d = x.shape
    tc = ctx.shape[1]
    assert ada_w.shape[0] == 1, "single-layer block only"
    assert t % GRID_W == 0 and t % PROJ_TM == 0
    rows = t // GRID_W
    d_lru = lru_conv_w.shape[2]
    d_mv = mlstm_norm.shape[1]
    d_mqk = (w_in.shape[2] - 2 * d_lru - 2 * d_mv - 4 * M_HEADS) // 2

    pad = SUBLANES - b - 1
    cc = jnp.concatenate([c, c_ctx[None, :], jnp.zeros((pad, d), F32)], axis=0)
    mod = _ada_call(cc, ada_w[0], ada_b[0][None, :]).reshape(SUBLANES, N_MOD, d)
    lat = lambda i: mod[:b, i][:, None, :]
    cxt = lambda i: mod[b:b + 1, i][:, None, :]
    row = lambda v: v[0][None, :]

    up1, down1 = ffn1_w_up[0], ffn1_w_down[0]
    x1 = _ffn_call(x.reshape(b * t, d), row(ffn1_norm), lat(0), lat(1), lat(2), up1, down1,
                   row(ffn1_norm), rows_per_mod=t, final_norm=False, name="ffn_pre")
    ctx1 = _ffn_call(ctx.reshape(b * tc, d), row(ffn1_norm), cxt(0), cxt(1), cxt(2), up1, down1,
                     row(ffn1_norm), rows_per_mod=b * tc, final_norm=False, name="ffn_pre_ctx")
    x1 = x1.reshape(b, t, d)
    ctx1 = ctx1.reshape(b, tc, d)

    n_gate = 4 * M_HEADS
    w_a, w_b = _split_w_in(w_in[0].T, 2 * d_lru)
    widths_a, dtypes_a = (d_lru, d_lru), (F32, F32)
    widths_b, dtypes_b = (d_mqk, d_mqk, d_mv, d_mv, LANES), (BF16, BF16, BF16, F32, F32)
    mixn = row(mix_norm)
    lx, lg = _proj_call(x1, mixn, lat(3), lat(4), w_a, widths_a, dtypes_a, tm=PROJ_TM, colmajor=False,
                        name="proj_lru")
    q, k, v, o, gt = _proj_call(x1.reshape(b, rows, GRID_W, d), mixn, lat(3), lat(4), w_b, widths_b,
                                dtypes_b, tm=PROJ_TM, colmajor=True, name="proj_mlstm")
    ctx_mod = lambda i: jnp.broadcast_to(cxt(i), (b, 1, d))
    lxc, _ = _proj_call(ctx1, mixn, ctx_mod(3), ctx_mod(4), w_a, widths_a, dtypes_a, tm=tc, colmajor=False,
                        name="proj_lru_ctx")
    qc, kc, vc, _, gtc = _proj_call(ctx1, mixn, ctx_mod(3), ctx_mod(4), w_b, widths_b, dtypes_b, tm=tc,
                                    colmajor=False, name="proj_mlstm_ctx")

    lru_lat = _lru_call(lx, lg, lxc, lru_conv_w[0], row(lru_conv_b), lru_w_r[0], lru_b_r[0], lru_w_i[0],
                        lru_b_i[0], lru_lam[0])
    bias = jnp.concatenate([b_mgate[0], jnp.zeros((LANES - n_gate,), F32)])[None, :]
    mls_cm = _mlstm_call(q, k, v, o, gt, qc, kc, vc, gtc, bias, row(mlstm_norm))

    w_out0 = w_out[0].astype(BF16)
    x2 = _outproj_call(x1, lru_lat, mls_cm, w_out0[:d_lru], w_out0[d_lru:], lat(5))

    up2, down2 = ffn2_w_up[0], ffn2_w_down[0]
    out = _ffn_call(x2.reshape(b * t, d), row(ffn2_norm), lat(6), lat(7), lat(8), up2, down2,
                    final_norm[None, :], rows_per_mod=t, final_norm=True, name="ffn_post")
    return out.reshape(b, t, d)
```

```python
import functools

import jax
import jax.numpy as jnp
from jax import lax
from jax.experimental import pallas as pl
from jax.experimental.pallas import tpu as pltpu

F32 = jnp.float32
BF16 = jnp.bfloat16

GRID_W = 64
LRU_BLOCKS = 8
CONV_W = 4
LRU_C = 8.0
M_HEADS = 8
CHUNK = 128
N_MOD = 9
EPS = 1e-6
HALF = 0.5

LANES = 128
SUBLANES = 8
VMEM_LIMIT_BYTES = 58 * 1024 * 1024

FFN_TM = 1024
FFN_TF = 256
FFN_EDGE_ROWS = 256
PROJ_ROWS = 256
PROJ_TM = 512
LRU_GATE_ROWS = 512
LRU_SEQ_SCAN_MAX = 64
ADA_TN = 1024
WSPLIT_TN = 512


def _cparams(sem):
    return pltpu.CompilerParams(dimension_semantics=sem, vmem_limit_bytes=VMEM_LIMIT_BYTES)


def _sigmoid(z):
    return 0.5 * jnp.tanh(0.5 * z) + 0.5


def _rms(x, g):
    return x * lax.rsqrt(jnp.mean(x * x, axis=-1, keepdims=True) + EPS) * g


def _ada_body(c_ref, w_ref, b_ref, o_ref):
    s = c_ref[...]
    s = s * jax.nn.sigmoid(s)
    o_ref[...] = jnp.dot(s, w_ref[...], preferred_element_type=F32) + b_ref[...]


def _ada_call(cc, w, b):
    rows, d = cc.shape
    n = w.shape[1]
    return pl.pallas_call(
        _ada_body,
        grid=(n // ADA_TN,),
        in_specs=[pl.BlockSpec((rows, d), lambda j: (0, 0)),
                  pl.BlockSpec((d, ADA_TN), lambda j: (0, j)),
                  pl.BlockSpec((1, ADA_TN), lambda j: (0, j))],
        out_specs=pl.BlockSpec((rows, ADA_TN), lambda j: (0, j)),
        out_shape=jax.ShapeDtypeStruct((rows, n), F32),
        compiler_params=_cparams(("arbitrary",)),
        name="ada_mod",
    )(cc, w, b)


def _ffn_body(*refs, final_norm, n_groups):
    groups_in = [refs[4 * k:4 * k + 4] for k in range(n_groups)]
    g_ref, wg_ref, wu_ref, wd_ref, fin_ref = refs[4 * n_groups:4 * n_groups + 5]
    out_refs = refs[4 * n_groups + 5:5 * n_groups + 5]
    h_scr = refs[-1]
    j = pl.program_id(1)
    last = pl.num_programs(1) - 1
    groups, off = [], 0
    for (x_ref, sh_ref, sc_ref, gate_ref), o_ref in zip(groups_in, out_refs):
        n = x_ref.shape[0]
        rb = min(FFN_EDGE_ROWS, n)
        groups.append((x_ref, sh_ref, sc_ref, gate_ref, o_ref, off, [slice(r0, r0 + rb) for r0 in range(0, n, rb)]))
        off += n

    def contrib(h):
        g = jnp.dot(h, wg_ref[...].astype(BF16), preferred_element_type=F32)
        u = jnp.dot(h, wu_ref[...].astype(BF16), preferred_element_type=F32)
        a = (g * jax.nn.sigmoid(g) * u).astype(BF16)
        return jnp.dot(a, wd_ref[...].astype(BF16), preferred_element_type=F32)

    @pl.when(j == 0)
    def _():
        for x_ref, sh_ref, sc_ref, _, o_ref, off, blocks in groups:
            for rows in blocks:
                h = (_rms(x_ref[rows, :], g_ref[...]) * (1.0 + sc_ref[0]) + sh_ref[0]).astype(BF16)
                h_scr[off + rows.start:off + rows.stop, :] = h
                o_ref[rows, :] = contrib(h)

    @pl.when((j > 0) & (j < last))
    def _():
        res = contrib(h_scr[...])
        for _, _, _, _, o_ref, off, _ in groups:
            o_ref[...] += res[off:off + o_ref.shape[0]]

    @pl.when(j == last)
    def _():
        for x_ref, _, _, gate_ref, o_ref, off, blocks in groups:
            for rows in blocks:
                acc = o_ref[rows, :] + contrib(h_scr[off + rows.start:off + rows.stop, :])
                y = x_ref[rows, :] + HALF * gate_ref[0] * acc
                if final_norm:
                    y = _rms(y, fin_ref[...])
                o_ref[rows, :] = y


def _ffn_call(x2d, mods, norm_g, w_up, w_down, fin_g, *, rows_per_mod, final_norm, name, extra=None):
    m, d = x2d.shape
    f = w_down.shape[0]
    tm = min(FFN_TM, m)
    n_tiles = m // tm
    tiles_per_mod = rows_per_mod // tm
    nf = f // FFN_TF
    vec_spec = pl.BlockSpec((1, d), lambda i, j: (0, 0))
    row_spec = lambda rows: pl.BlockSpec((rows, d), lambda i, j: (i, 0))
    mod_spec = pl.BlockSpec((1, 1, d), lambda i, j: (i // tiles_per_mod, 0, 0))
    const_mod_spec = pl.BlockSpec((1, 1, d), lambda i, j: (0, 0, 0))
    in_specs = [row_spec(tm), mod_spec, mod_spec, mod_spec]
    args = [x2d, *mods]
    out_rows = [tm]
    if extra is not None:
        x_e, mods_e = extra
        te = x_e.shape[0] // n_tiles
        assert te * n_tiles == x_e.shape[0] and te % (2 * SUBLANES) == 0
        in_specs += [row_spec(te), const_mod_spec, const_mod_spec, const_mod_spec]
        args += [x_e, *mods_e]
        out_rows.append(te)
    in_specs += [vec_spec,
                 pl.BlockSpec((d, FFN_TF), lambda i, j: (0, j)),
                 pl.BlockSpec((d, FFN_TF), lambda i, j: (0, j + nf)),
                 pl.BlockSpec((FFN_TF, d), lambda i, j: (j, 0)),
                 vec_spec]
    args += [norm_g, w_up, w_up, w_down, fin_g]
    return pl.pallas_call(
        functools.partial(_ffn_body, final_norm=final_norm, n_groups=len(out_rows)),
        grid=(n_tiles, nf),
        in_specs=in_specs,
        out_specs=[row_spec(r) for r in out_rows],
        out_shape=[jax.ShapeDtypeStruct((r * n_tiles, d), F32) for r in out_rows],
        scratch_shapes=[pltpu.VMEM((sum(out_rows), d), BF16)],
        compiler_params=_cparams(("parallel", "arbitrary")),
        name=name,
    )(*args)


def _wsplit_body(wt_ref, a_ref, b_ref, *, n_valid, n_a_blocks):
    j = pl.program_id(0)
    feat = j * WSPLIT_TN + lax.broadcasted_iota(jnp.int32, wt_ref.shape, 0)
    w = jnp.where(feat < n_valid, wt_ref[...], 0.0).T.astype(BF16)

    @pl.when(j < n_a_blocks)
    def _():
        a_ref[...] = w

    @pl.when(j >= n_a_blocks)
    def _():
        b_ref[...] = w


def _split_w_in(wt, n_a):
    n, d = wt.shape
    assert n_a % WSPLIT_TN == 0
    nb = pl.cdiv(n, WSPLIT_TN)
    na = n_a // WSPLIT_TN
    return pl.pallas_call(
        functools.partial(_wsplit_body, n_valid=n, n_a_blocks=na),
        grid=(nb,),
        in_specs=[pl.BlockSpec((WSPLIT_TN, d), lambda j: (j, 0))],
        out_specs=[pl.BlockSpec((d, WSPLIT_TN), lambda j: (0, jnp.minimum(j, na - 1))),
                   pl.BlockSpec((d, WSPLIT_TN), lambda j: (0, jnp.maximum(j - na, 0)))],
        out_shape=[jax.ShapeDtypeStruct((d, n_a), BF16),
                   jax.ShapeDtypeStruct((d, (nb - na) * WSPLIT_TN), BF16)],
        compiler_params=_cparams(("arbitrary",)),
        name="w_in_split",
    )(wt)


def _colmajor_perm(n_rows, n_cols, src0=0, n_src=None):
    assert n_rows & (n_rows - 1) == 0
    m = n_rows * n_cols
    n_src = m if n_src is None else n_src
    j = lax.broadcasted_iota(jnp.int32, (m, n_src), 0)
    i = lax.broadcasted_iota(jnp.int32, (m, n_src), 1) + src0
    src = (j & (n_rows - 1)) * n_cols + (j >> (n_rows.bit_length() - 1))
    return (i == src).astype(BF16)


def _proj_body(x_ref, g_ref, sh_ref, sc_ref, w_ref, *refs, widths, colmajor):
    out_refs, h_scr = refs[:-1], refs[-1]
    tm = h_scr.shape[0]
    rb = min(PROJ_ROWS, tm)
    norm = lambda x: (_rms(x, g_ref[...]) * (1.0 + sc_ref[0]) + sh_ref[0]).astype(BF16)
    if colmajor:
        r, c, d = x_ref.shape[1:]
        rows_per_block = rb // c
        hp = None
        for r0 in range(0, r, rows_per_block):
            h = norm(x_ref[0, r0:r0 + rows_per_block].reshape(rb, d))
            part = jnp.dot(_colmajor_perm(r, c, r0 * c, rb), h, preferred_element_type=F32)
            hp = part if hp is None else hp + part
        h_scr[...] = hp.astype(BF16)
    for r0 in range(0, tm, rb):
        rows = slice(r0, r0 + rb)
        h = h_scr[rows, :] if colmajor else norm(x_ref[0, rows, :])
        off = 0
        for o_ref, wdt in zip(out_refs, widths):
            o_ref[0, rows, :] = jnp.dot(h, w_ref[:, off:off + wdt],
                                        preferred_element_type=F32).astype(o_ref.dtype)
            off += wdt


def _proj_call(x, norm_g, shift, scale, w, widths, dtypes, *, tm, colmajor, name):
    b = x.shape[0]
    d = norm_g.shape[1]
    if colmajor:
        rows, gw = x.shape[1:3]
        cols = tm // rows
        assert cols == SUBLANES and gw % cols == 0
        n_tiles = gw // cols
        x_spec = pl.BlockSpec((1, rows, cols, d), lambda bi, i: (bi, 0, i, 0))
    else:
        n_tiles = x.shape[1] // tm
        x_spec = pl.BlockSpec((1, tm, d), lambda bi, i: (bi, i, 0))
    mod_spec = pl.BlockSpec((1, 1, d), lambda bi, i: (bi, 0, 0))
    return pl.pallas_call(
        functools.partial(_proj_body, widths=tuple(widths), colmajor=colmajor),
        grid=(b, n_tiles),
        in_specs=[x_spec,
                  pl.BlockSpec((1, d), lambda bi, i: (0, 0)),
                  mod_spec, mod_spec,
                  pl.BlockSpec(w.shape, lambda bi, i: (0, 0))],
        out_specs=[pl.BlockSpec((1, tm, wdt), lambda bi, i: (bi, i, 0)) for wdt in widths],
        out_shape=[jax.ShapeDtypeStruct((b, n_tiles * tm, wdt), dt) for wdt, dt in zip(widths, dtypes)],
        scratch_shapes=[pltpu.VMEM((tm, d), BF16)],
        compiler_params=_cparams(("parallel", "parallel")),
        name=name,
    )(x, norm_g, shift, scale, w)


def _lru_conv(x_ref, cw_ref, cb_ref, pad_scr, xc_scr, t):
    zeros = jnp.zeros((SUBLANES, LANES), F32)
    pad_scr[0:SUBLANES, :] = zeros
    pad_scr[SUBLANES:SUBLANES + t, :] = x_ref[0]
    pad_scr[SUBLANES + t:2 * SUBLANES + t, :] = zeros
    ch = min(LRU_GATE_ROWS, t)
    win_rows = ch + 2 * SUBLANES

    def body(c, carry):
        off = pl.multiple_of(c * ch, SUBLANES)
        win = pad_scr[pl.ds(off, win_rows), :]
        acc = cb_ref[...] + cw_ref[2:3, :] * win[SUBLANES:SUBLANES + ch]
        for k, shift in ((0, 2), (1, 1), (3, win_rows - 1)):
            acc = acc + cw_ref[k:k + 1, :] * pltpu.roll(win, shift, 0)[SUBLANES:SUBLANES + ch]
        xc_scr[pl.ds(off, ch), :] = acc
        return carry

    lax.fori_loop(0, t // ch, body, 0)


def _lru_gates(xc_scr, t, d, wr_ref, br_ref, wi_ref, bi_ref, lam_ref, a_scr, b_scr):
    ch = min(LRU_GATE_ROWS, t)
    half_unit = (-0.5 * LRU_C) * jax.nn.softplus(-lam_ref[d:d + 1, :])
    wr = (0.5 * wr_ref[d, 0]).astype(BF16)
    wi = (0.5 * wi_ref[d, 0]).astype(BF16)
    br = 0.5 * br_ref[d:d + 1, :]
    bi = 0.5 * bi_ref[d:d + 1, :]

    def body(c, carry):
        off = pl.multiple_of(c * ch, SUBLANES)
        xc = xc_scr[pl.ds(off, ch), :]
        xb = xc.astype(BF16)
        tr = jnp.tanh(jnp.dot(xb, wr, preferred_element_type=F32) + br)
        ti = jnp.tanh(jnp.dot(xb, wi, preferred_element_type=F32) + bi)
        log_a = half_unit * tr + half_unit
        a = jnp.exp(log_a)
        a_scr[pl.ds(off, ch), :] = a
        y = -jnp.tanh(log_a) * (a * a + 1.0)
        mult = jnp.where(y > 0.0, y * lax.rsqrt(y), 0.0)
        b_scr[pl.ds(off, ch), :] = (mult * xc) * (0.5 * ti + 0.5)
        return carry

    lax.fori_loop(0, t // ch, body, 0, unroll=min(2, t // ch))


def _lru_scan(a_ref, b_ref, t, d, h0, out_ref, accumulate=False):
    groups = t // SUBLANES
    row = lax.broadcasted_iota(jnp.int32, (SUBLANES, LANES), 0)

    def body(g, carry):
        gi = g if d == 0 else groups - 1 - g
        off = pl.multiple_of(gi * SUBLANES, SUBLANES)
        a = a_ref[pl.ds(off, SUBLANES), :]
        bv = b_ref[pl.ds(off, SUBLANES), :]
        for k in (1, 2, 4):
            shift = k if d == 0 else SUBLANES - k
            valid = (row >= k) if d == 0 else (row < SUBLANES - k)
            a_prev = pltpu.roll(a, shift, 0)
            b_prev = pltpu.roll(bv, shift, 0)
            bv = jnp.where(valid, a * b_prev + bv, bv)
            a = jnp.where(valid, a * a_prev, a)
        h = a * carry + bv
        if out_ref is not None:
            if accumulate:
                out_ref[pl.ds(off, SUBLANES), :] += h
            else:
                out_ref[pl.ds(off, SUBLANES), :] = h
        return h[SUBLANES - 1:SUBLANES, :] if d == 0 else h[0:1, :]

    return lax.fori_loop(0, groups, body, h0, unroll=min(4, groups))


def _lru_scan_planes(a_ref, b_ref, n, d, h0, out_ref, accumulate, levels):
    if n <= LRU_SEQ_SCAN_MAX:
        _lru_scan(a_ref, b_ref, n, d, h0, out_ref, accumulate)
        return
    g = n // SUBLANES
    p_scr, q_scr, s_scr = levels[0]
    order = list(range(SUBLANES)) if d == 0 else list(range(SUBLANES - 1, -1, -1))
    group_rows = SUBLANES * SUBLANES

    def up(v, carry):
        base = pl.multiple_of(v * group_rows, group_rows)
        rows = pl.ds(pl.multiple_of(v * SUBLANES, SUBLANES), SUBLANES)
        p = h = None
        for j in order:
            a = a_ref[pl.ds(base + j, SUBLANES, stride=SUBLANES), :]
            bv = b_ref[pl.ds(base + j, SUBLANES, stride=SUBLANES), :]
            if p is None:
                p, h = a, bv
            else:
                h = a * h + bv
                p = a * p
            p_scr[j, rows, :] = p
            q_scr[j, rows, :] = h
        return carry

    lax.fori_loop(0, g // SUBLANES, up, 0, unroll=4)

    last = order[-1]
    _lru_scan_planes(p_scr.at[last], q_scr.at[last], g, d, h0, s_scr, False, levels[1:])
    s = s_scr[...]
    row = lax.broadcasted_iota(jnp.int32, (g, LANES), 0)
    if d == 0:
        s_scr[...] = jnp.where(row == 0, h0, pltpu.roll(s, 1, 0))
    else:
        s_scr[...] = jnp.where(row == g - 1, h0, pltpu.roll(s, g - 1, 0))

    def down(v, carry):
        base = pl.multiple_of(v * group_rows, group_rows)
        rows = pl.ds(pl.multiple_of(v * SUBLANES, SUBLANES), SUBLANES)
        x = s_scr[rows, :]
        for j in range(SUBLANES):
            dst = pl.ds(base + j, SUBLANES, stride=SUBLANES)
            val = q_scr[j, rows, :] + p_scr[j, rows, :] * x
            if accumulate:
                val = val + out_ref[dst, :]
            out_ref[dst, :] = val
        return carry

    lax.fori_loop(0, g // SUBLANES, down, 0, unroll=4)


def _lru_body(lx_ref, lg_ref, lxc_ref, cw_ref, cb_ref, wr_ref, br_ref, wi_ref, bi_ref, lam_ref, o_ref,
              pad_scr, xc_scr, xcc_scr, a_scr, b_scr, h_scr, *level_scr):
    levels = [level_scr[i:i + 3] for i in range(0, len(level_scr), 3)]
    t = lx_ref.shape[1]
    tc = lxc_ref.shape[1]
    _lru_conv(lxc_ref, cw_ref, cb_ref, pad_scr, xcc_scr, tc)
    _lru_conv(lx_ref, cw_ref, cb_ref, pad_scr, xc_scr, t)
    gate_refs = (wr_ref, br_ref, wi_ref, bi_ref, lam_ref)
    for d in (0, 1):
        _lru_gates(xcc_scr, tc, d, *gate_refs, a_scr, b_scr)
        h0 = _lru_scan(a_scr, b_scr, tc, d, jnp.zeros((1, LANES), F32), None)
        _lru_gates(xc_scr, t, d, *gate_refs, a_scr, b_scr)
        _lru_scan_planes(a_scr, b_scr, t, d, h0, h_scr, d == 1, levels)
    o_ref[0] = (jax.nn.gelu(lg_ref[0]) * h_scr[...]).astype(o_ref.dtype)


def _lru_call(lx, lg, lxc, conv_w, conv_b, w_r, b_r, w_i, b_i, lam):
    b, t, d_lru = lx.shape
    tc = lxc.shape[1]
    nb = d_lru // LANES
    assert w_r.shape == (2, nb, LANES, LANES)
    seq = lambda n: pl.BlockSpec((1, n, LANES), lambda bi, j: (bi, 0, j))
    vec = lambda n: pl.BlockSpec((n, LANES), lambda bi, j: (0, j))
    wblk = pl.BlockSpec((2, 1, LANES, LANES), lambda bi, j: (0, j, 0, 0))
    level_scr = []
    n = t
    while n > LRU_SEQ_SCAN_MAX:
        assert n % (SUBLANES * SUBLANES) == 0
        n //= SUBLANES
        level_scr += [pltpu.VMEM((SUBLANES, n, LANES), F32), pltpu.VMEM((SUBLANES, n, LANES), F32),
                      pltpu.VMEM((n, LANES), F32)]
    return pl.pallas_call(
        _lru_body,
        grid=(b, nb),
        in_specs=[seq(t), seq(t), seq(tc), vec(CONV_W), vec(1), wblk, vec(2), wblk, vec(2), vec(2)],
        out_specs=seq(t),
        out_shape=jax.ShapeDtypeStruct((b, t, d_lru), BF16),
        scratch_shapes=[pltpu.VMEM((t + 2 * SUBLANES, LANES), F32),
                        pltpu.VMEM((t, LANES), F32), pltpu.VMEM((tc, LANES), F32),
                        pltpu.VMEM((t, LANES), F32), pltpu.VMEM((t, LANES), F32),
                        pltpu.VMEM((t, LANES), F32)] + level_scr,
        compiler_params=_cparams(("parallel", "parallel")),
        name="rglru",
    )(lx, lg, lxc, conv_w, conv_b, w_r, b_r, w_i, b_i, lam)


FWD_GATE_LANES = 2 * M_HEADS


def _split3_dot(tri, x):
    hi = x.astype(BF16)
    r1 = x - hi.astype(F32)
    mid = r1.astype(BF16)
    lo = (r1 - mid.astype(F32)).astype(BF16)
    return (jnp.dot(tri, hi, preferred_element_type=F32) + jnp.dot(tri, mid, preferred_element_type=F32)
            + jnp.dot(tri, lo, preferred_element_type=F32))


def _cummax_rows(x, row, reverse):
    n = x.shape[0]
    k = 1
    while k < n:
        if reverse:
            x = jnp.where(row < n - k, jnp.maximum(x, pltpu.roll(x, n - k, 0)), x)
        else:
            x = jnp.where(row >= k, jnp.maximum(x, pltpu.roll(x, k, 0)), x)
        k *= 2
    return x


def _round_up_bf16(x):
    return (x + jnp.abs(x) * (2.0 ** -7)).astype(BF16)


def _mlstm_gate_prep(gates, kidx, bc_scr, cm_scr, xt_scr, tot_scr, gmax_scr):
    L = gates.shape[0]
    ti = lax.broadcasted_iota(jnp.int32, (L, L), 0)
    si = lax.broadcasted_iota(jnp.int32, (L, L), 1)
    row = lax.broadcasted_iota(jnp.int32, (L, LANES), 0)
    fwd = lax.broadcasted_iota(jnp.int32, (L, LANES), 1) < FWD_GATE_LANES
    lf = jax.nn.log_sigmoid(gates)
    bc_f = _split3_dot((si <= ti).astype(BF16), lf)
    bc_b = _split3_dot((si >= ti).astype(BF16), lf)
    bc = jnp.where(fwd, bc_f, bc_b)
    tot = jnp.where(fwd[0:1], bc_f[L - 1:L], bc_b[0:1])
    x = pltpu.roll(gates, SUBLANES, 1) - bc
    cm = jnp.where(fwd, _cummax_rows(x, row, False), _cummax_rows(x, row, True))
    gmax = jnp.max(tot + x, axis=0, keepdims=True)
    xt = x.T
    bc_scr[kidx] = bc
    cm_scr[kidx] = _round_up_bf16(cm)
    xt_scr[kidx, 0:SUBLANES, :] = xt[SUBLANES:2 * SUBLANES]
    xt_scr[kidx, SUBLANES:2 * SUBLANES, :] = xt[3 * SUBLANES:4 * SUBLANES]
    tot_scr[kidx] = jnp.broadcast_to(tot, (SUBLANES, LANES))
    gmax_scr[kidx] = jnp.broadcast_to(gmax, (SUBLANES, LANES))


def _mlstm_chunk(q, v, kidx, step, d, scr, need_h, dk):
    bc_scr, cm_scr, xt_scr, kt_scr, tot_scr, m_in_scr, m_out_scr, s_scr = scr
    L = q.shape[0]
    ti = lax.broadcasted_iota(jnp.int32, (L, L), 0)
    si = lax.broadcasted_iota(jnp.int32, (L, L), 1)
    causal = (si <= ti) if d == 0 else (si >= ti)
    lane = lax.broadcasted_iota(jnp.int32, (L, LANES), 1)
    head0_rows = lax.broadcasted_iota(jnp.int32, (LANES, L), 0) < dk
    srow = lax.broadcasted_iota(jnp.int32, (LANES, 2 * LANES), 0)
    k_t = kt_scr[kidx]
    tot_v, m_in_v, m_out_v = tot_scr[kidx], m_in_scr[step], m_out_scr[step]
    ones = jnp.ones((L, LANES), BF16)
    v1 = jnp.concatenate([v[:, :LANES], ones, v[:, LANES:], ones], axis=1)
    cf0 = FWD_GATE_LANES * d + SUBLANES
    pick = lambda vals, e: vals[0:1, cf0 + e:cf0 + e + 1]
    x_rows = [xt_scr[kidx, SUBLANES * d + e:SUBLANES * d + e + 1, :] for e in (0, 1)]
    s_old = s_scr[d]
    hs = []
    if need_h:
        bc = bc_scr[kidx]
        sel_lane = lax.broadcasted_iota(jnp.int32, (LANES, 2 * LANES), 0)
        sel_col = lax.broadcasted_iota(jnp.int32, (LANES, 2 * LANES), 1)
        sel = (sel_lane == jnp.where(sel_col < LANES, cf0, cf0 + 1)).astype(BF16)
        mx_all = jnp.maximum(_round_up_bf16(m_in_v[0:1, :]), cm_scr[kidx])
        mx_tiles = jnp.dot(mx_all, sel, preferred_element_type=F32)
        q_heads = [jnp.where((lane >= e * dk) & (lane < (e + 1) * dk), q, jnp.zeros_like(q)) for e in (0, 1)]
        q2 = jnp.concatenate(q_heads, axis=0)
        qk = jnp.dot(q2, k_t, preferred_element_type=F32)
        qs = jnp.dot(q2, s_old.astype(BF16), preferred_element_type=F32)
        for e in (0, 1):
            cf = cf0 + e
            mx = mx_tiles[:, e * LANES:(e + 1) * LANES]
            w = qk[e * L:(e + 1) * L] * jnp.exp(jnp.where(causal, x_rows[e] - mx, -jnp.inf))
            s_inter = jnp.exp(pick(m_in_v, e) - mx)
            wv = jnp.dot(w.astype(BF16), v1[:, 2 * e * LANES:2 * (e + 1) * LANES],
                         preferred_element_type=F32)
            qs_e = qs[e * L:(e + 1) * L]
            num = s_inter * qs_e[:, :LANES] + wv[:, :LANES]
            den = s_inter * qs_e[:, LANES:] + wv[:, LANES:]
            m_row = bc[:, cf:cf + 1] + mx[:, cf:cf + 1]
            hs.append(num * (1.0 / jnp.maximum(jnp.abs(den[:, cf:cf + 1]), jnp.exp(-m_row))))
    wg_rows = [jnp.exp(pick(tot_v, e) + x_rows[e] - pick(m_out_v, e)) for e in (0, 1)]
    decays = [jnp.exp(pick(tot_v, e) + pick(m_in_v, e) - pick(m_out_v, e)) for e in (0, 1)]
    kw = (k_t.astype(F32) * jnp.where(head0_rows, wg_rows[0], wg_rows[1])).astype(BF16)
    own = jnp.concatenate([jnp.dot(kw[:dk], v1[:, :2 * LANES], preferred_element_type=F32),
                           jnp.dot(kw[dk:], v1[:, 2 * LANES:], preferred_element_type=F32)], axis=0)
    s_scr[d] = jnp.where(srow < dk, decays[0], decays[1]) * s_old + own
    return hs


def _mlstm_body(q_ref, k_ref, v_ref, o_ref, gt_ref, qc_ref, kc_ref, vc_ref, gtc_ref, bias_ref, gain_ref,
                out_ref, hs_scr, bc_scr, cm_scr, xt_scr, kt_scr, tot_scr, gmax_scr, m_in_scr, m_out_scr,
                s_scr, *, dk):
    t = q_ref.shape[1]
    tc = qc_ref.shape[1]
    L = CHUNK
    nc, ncc = t // L, tc // L
    n_steps = nc + ncc
    shift = (LANES - 2 * pl.program_id(1)) % LANES
    bias = pltpu.roll(jnp.broadcast_to(bias_ref[...], (SUBLANES, LANES)), shift, 1)[0:1, :]
    prep_scr = (bc_scr, cm_scr, xt_scr, tot_scr, gmax_scr)
    state_scr = (bc_scr, cm_scr, xt_scr, kt_scr, tot_scr, m_in_scr, m_out_scr, s_scr)
    scaled_t = lambda k: (k.astype(F32).T * (dk ** -0.5)).astype(BF16)

    for c in range(ncc):
        rows = slice(c * L, (c + 1) * L)
        _mlstm_gate_prep(pltpu.roll(gtc_ref[0, rows, :], shift, 1) + bias, c, *prep_scr)
        kt_scr[c] = scaled_t(kc_ref[0, rows, :])

    def prep(c, carry):
        rows = pl.ds(pl.multiple_of(c * L, L), L)
        _mlstm_gate_prep(pltpu.roll(gt_ref[0, rows, :], shift, 1) + bias, c + ncc, *prep_scr)
        kt_scr[c + ncc] = scaled_t(k_ref[0, rows, :])
        return carry

    lax.fori_loop(0, nc, prep, 0, unroll=4)

    fwd_id = lambda i: i
    bwd_id = lambda i: (ncc - 1 - i) if i < ncc else (n_steps - 1 - (i - ncc))
    fwd8 = lax.broadcasted_iota(jnp.int32, (SUBLANES, LANES), 1) < FWD_GATE_LANES
    m = jnp.zeros((SUBLANES, LANES), F32)
    for i in range(n_steps):
        tot = jnp.where(fwd8, tot_scr[fwd_id(i)], tot_scr[bwd_id(i)])
        gmax = jnp.where(fwd8, gmax_scr[fwd_id(i)], gmax_scr[bwd_id(i)])
        m_in_scr[i] = m
        m = jnp.maximum(tot + m, gmax)
        m_out_scr[i] = m

    s_scr[...] = jnp.zeros_like(s_scr)
    hs_scr[...] = jnp.zeros_like(hs_scr)
    for i in range(ncc):
        for d, cid in ((0, fwd_id(i)), (1, bwd_id(i))):
            rows = slice(cid * L, (cid + 1) * L)
            _mlstm_chunk(qc_ref[0, rows, :], vc_ref[0, rows, :], cid, i, d, state_scr, False, dk)

    def body(j, carry):
        for d in (0, 1):
            cj = j if d == 0 else nc - 1 - j
            rows = pl.ds(pl.multiple_of(cj * L, L), L)
            hs = _mlstm_chunk(q_ref[0, rows, :], v_ref[0, rows, :], cj + ncc, j + ncc, d, state_scr, True, dk)
            hs_scr[rows, :] += jnp.concatenate(hs, axis=1)
        return carry

    lax.fori_loop(0, nc, body, 0, unroll=4)

    def fin(c, carry):
        rows = pl.ds(pl.multiple_of(c * L, L), L)
        hh = hs_scr[rows, :]
        outs = []
        for e in (0, 1):
            x = hh[:, e * LANES:(e + 1) * LANES]
            outs.append(x * lax.rsqrt(jnp.mean(x * x, axis=-1, keepdims=True) + EPS))
        y = jnp.concatenate(outs, axis=1) * gain_ref[...] * _sigmoid(o_ref[0, rows, :])
        out_ref[0, rows, :] = y.astype(out_ref.dtype)
        return carry

    lax.fori_loop(0, nc, fin, 0, unroll=2)


def _mlstm_call(q, k, v, o, gt, qc, kc, vc, gtc, bias, gain):
    b, t, d_qk = q.shape
    tc = qc.shape[1]
    d_v = v.shape[2]
    pairs = M_HEADS // 2
    dk = d_qk // M_HEADS
    assert d_qk // pairs == LANES and d_v // pairs == 2 * LANES and CHUNK == LANES
    n_chunks = (t + tc) // CHUNK
    seq = lambda n, w: pl.BlockSpec((1, n, w), lambda bi, p: (bi, 0, p))
    allg = lambda n: pl.BlockSpec((1, n, LANES), lambda bi, p: (bi, 0, 0))
    chunk_f32 = lambda rows, cols: pltpu.VMEM((n_chunks, rows, cols), F32)
    return pl.pallas_call(
        functools.partial(_mlstm_body, dk=dk),
        grid=(b, pairs),
        in_specs=[seq(t, LANES), seq(t, LANES), seq(t, 2 * LANES), seq(t, 2 * LANES), allg(t),
                  seq(tc, LANES), seq(tc, LANES), seq(tc, 2 * LANES), allg(tc),
                  pl.BlockSpec((1, LANES), lambda bi, p: (0, 0)),
                  pl.BlockSpec((1, 2 * LANES), lambda bi, p: (0, p))],
        out_specs=seq(t, 2 * LANES),
        out_shape=jax.ShapeDtypeStruct((b, t, d_v), F32),
        scratch_shapes=[pltpu.VMEM((t, 2 * LANES), F32),
                        chunk_f32(CHUNK, LANES), pltpu.VMEM((n_chunks, CHUNK, LANES), BF16),
                        chunk_f32(2 * SUBLANES, CHUNK),
                        pltpu.VMEM((n_chunks, LANES, CHUNK), BF16),
                        chunk_f32(SUBLANES, LANES), chunk_f32(SUBLANES, LANES),
                        chunk_f32(SUBLANES, LANES), chunk_f32(SUBLANES, LANES),
                        pltpu.VMEM((2, LANES, 2 * LANES), F32)],
        compiler_params=_cparams(("parallel", "parallel")),
        name="mlstm",
    )(q, k, v, o, gt, qc, kc, vc, gtc, bias, gain)


def _outproj_body(x_ref, lru_ref, mls_ref, wa_ref, wb_ref, g_ref, o_ref):
    w_cols, r, dm = mls_ref.shape[1:]
    m = mls_ref[0].reshape(w_cols * r, dm).astype(BF16)
    m = jnp.dot(_colmajor_perm(w_cols, r), m, preferred_element_type=F32).astype(BF16)
    y = (jnp.dot(lru_ref[0], wa_ref[...], preferred_element_type=F32)
         + jnp.dot(m, wb_ref[...], preferred_element_type=F32))
    o_ref[0] = x_ref[0] + g_ref[0] * y


def _outproj_call(x3d, lru, mls_cm, w_a, w_b, gate):
    b, t, d = x3d.shape
    da, dm = lru.shape[2], mls_cm.shape[2]
    rows = t // GRID_W
    r_tile = PROJ_TM // GRID_W
    assert r_tile == SUBLANES
    mls_view = mls_cm.reshape(b, GRID_W, rows, dm)
    return pl.pallas_call(
        _outproj_body,
        grid=(b, t // PROJ_TM),
        in_specs=[pl.BlockSpec((1, PROJ_TM, d), lambda bi, i: (bi, i, 0)),
                  pl.BlockSpec((1, PROJ_TM, da), lambda bi, i: (bi, i, 0)),
                  pl.BlockSpec((1, GRID_W, r_tile, dm), lambda bi, i: (bi, 0, i, 0)),
                  pl.BlockSpec((da, d), lambda bi, i: (0, 0)),
                  pl.BlockSpec((dm, d), lambda bi, i: (0, 0)),
                  pl.BlockSpec((1, 1, d), lambda bi, i: (bi, 0, 0))],
        out_specs=pl.BlockSpec((1, PROJ_TM, d), lambda bi, i: (bi, i, 0)),
        out_shape=jax.ShapeDtypeStruct((b, t, d), F32),
        compiler_params=_cparams(("parallel", "parallel")),
        name="outproj",
    )(x3d, lru, mls_view, w_a, w_b, gate)


def kernel(x, c, ctx, c_ctx, ada_w, ada_b, ffn1_norm, ffn1_w_up, ffn1_w_down, mix_norm, w_in, b_mgate, lru_conv_w, lru_conv_b, lru_w_r, lru_b_r, lru_w_i, lru_b_i, lru_lam, mlstm_norm, w_out, ffn2_norm, ffn2_w_up, ffn2_w_down, final_norm):
    b, t, d = x.shape
    tc = ctx.shape[1]
    assert ada_w.shape[0] == 1, "single-layer block only"
    assert t % GRID_W == 0 and t % PROJ_TM == 0
    rows = t // GRID_W
    d_lru = lru_conv_w.shape[2]
    d_mv = mlstm_norm.shape[1]
    d_mqk = (w_in.shape[2] - 2 * d_lru - 2 * d_mv - 4 * M_HEADS) // 2

    pad = SUBLANES - b - 1
    cc = jnp.concatenate([c, c_ctx[None, :], jnp.zeros((pad, d), F32)], axis=0)
    mod = _ada_call(cc, ada_w[0], ada_b[0][None, :]).reshape(SUBLANES, N_MOD, d)
    lat = lambda i: mod[:b, i][:, None, :]
    cxt = lambda i: mod[b:b + 1, i][:, None, :]
    row = lambda v: v[0][None, :]

    x1, ctx1 = _ffn_call(x.reshape(b * t, d), (lat(0), lat(1), lat(2)), row(ffn1_norm), ffn1_w_up[0],
                         ffn1_w_down[0], row(ffn1_norm), rows_per_mod=t, final_norm=False, name="ffn_pre",
                         extra=(ctx.reshape(b * tc, d), (cxt(0), cxt(1), cxt(2))))
    x1 = x1.reshape(b, t, d)
    ctx1 = ctx1.reshape(b, tc, d)

    n_gate = 4 * M_HEADS
    w_a, w_b = _split_w_in(w_in[0].T, 2 * d_lru)
    widths_a, dtypes_a = (d_lru, d_lru), (F32, F32)
    widths_b, dtypes_b = (d_mqk, d_mqk, d_mv, d_mv, LANES), (BF16, BF16, BF16, F32, F32)
    mixn = row(mix_norm)
    lx, lg = _proj_call(x1, mixn, lat(3), lat(4), w_a, widths_a, dtypes_a, tm=PROJ_TM, colmajor=False,
                        name="proj_lru")
    q, k, v, o, gt = _proj_call(x1.reshape(b, rows, GRID_W, d), mixn, lat(3), lat(4), w_b, widths_b,
                                dtypes_b, tm=PROJ_TM, colmajor=True, name="proj_mlstm")
    ctx_mod = lambda i: jnp.broadcast_to(cxt(i), (b, 1, d))
    lxc, _ = _proj_call(ctx1, mixn, ctx_mod(3), ctx_mod(4), w_a, widths_a, dtypes_a, tm=tc, colmajor=False,
                        name="proj_lru_ctx")
    qc, kc, vc, _, gtc = _proj_call(ctx1, mixn, ctx_mod(3), ctx_mod(4), w_b, widths_b, dtypes_b, tm=tc,
                                    colmajor=False, name="proj_mlstm_ctx")

    lru_lat = _lru_call(lx, lg, lxc, lru_conv_w[0], row(lru_conv_b), lru_w_r[0], lru_b_r[0], lru_w_i[0],
                        lru_b_i[0], lru_lam[0])
    bias = jnp.concatenate([b_mgate[0], jnp.zeros((LANES - n_gate,), F32)])[None, :]
    mls_cm = _mlstm_call(q, k, v, o, gt, qc, kc, vc, gtc, bias, row(mlstm_norm))

    w_out0 = w_out[0].astype(BF16)
    x2 = _outproj_call(x1, lru_lat, mls_cm, w_out0[:d_lru], w_out0[d_lru:], lat(5))

    out, = _ffn_call(x2.reshape(b * t, d), (lat(6), lat(7), lat(8)), row(ffn2_norm), ffn2_w_up[0],
                     ffn2_w_down[0], final_norm[None, :], rows_per_mod=t, final_norm=True, name="ffn_post")
    return out.reshape(b, t, d)
```

```python
import functools

import jax
import jax.numpy as jnp
from jax import lax
from jax.experimental import pallas as pl
from jax.experimental.pallas import tpu as pltpu

F32 = jnp.float32
BF16 = jnp.bfloat16

GRID_W = 64
LRU_BLOCKS = 8
CONV_W = 4
LRU_C = 8.0
M_HEADS = 8
CHUNK = 128
N_MOD = 9
EPS = 1e-6
HALF = 0.5

LANES = 128
SUBLANES = 8
VMEM_LIMIT_BYTES = 58 * 1024 * 1024

FFN_TM = 1024
FFN_TF = 256
FFN_EDGE_ROWS = 512
PROJ_ROWS = 256
PROJ_TM = 512
LRU_GATE_ROWS = 512
LRU_SEQ_SCAN_MAX = 64
ADA_TN = 1024
WSPLIT_TN = 512


def _cparams(sem):
    return pltpu.CompilerParams(dimension_semantics=sem, vmem_limit_bytes=VMEM_LIMIT_BYTES)


def _sigmoid(z):
    return 0.5 * jnp.tanh(0.5 * z) + 0.5


def _rms(x, g):
    return x * lax.rsqrt(jnp.mean(x * x, axis=-1, keepdims=True) + EPS) * g


def _ada_body(c_ref, w_ref, b_ref, o_ref):
    s = c_ref[...]
    s = s * jax.nn.sigmoid(s)
    o_ref[...] = jnp.dot(s, w_ref[...], preferred_element_type=F32) + b_ref[...]


def _ada_call(cc, w, b):
    rows, d = cc.shape
    n = w.shape[1]
    return pl.pallas_call(
        _ada_body,
        grid=(n // ADA_TN,),
        in_specs=[pl.BlockSpec((rows, d), lambda j: (0, 0)),
                  pl.BlockSpec((d, ADA_TN), lambda j: (0, j)),
                  pl.BlockSpec((1, ADA_TN), lambda j: (0, j))],
        out_specs=pl.BlockSpec((rows, ADA_TN), lambda j: (0, j)),
        out_shape=jax.ShapeDtypeStruct((rows, n), F32),
        compiler_params=_cparams(("arbitrary",)),
        name="ada_mod",
    )(cc, w, b)


def _ffn_body(*refs, final_norm, n_groups):
    groups_in = [refs[4 * k:4 * k + 4] for k in range(n_groups)]
    g_ref, wg_ref, wu_ref, wd_ref, fin_ref = refs[4 * n_groups:4 * n_groups + 5]
    out_refs = refs[4 * n_groups + 5:5 * n_groups + 5]
    h_scr = refs[-1]
    j = pl.program_id(1)
    last = pl.num_programs(1) - 1
    groups, off = [], 0
    for (x_ref, sh_ref, sc_ref, gate_ref), o_ref in zip(groups_in, out_refs):
        n = x_ref.shape[0]
        rb = min(FFN_EDGE_ROWS // 2 if final_norm else FFN_EDGE_ROWS, n)
        groups.append((x_ref, sh_ref, sc_ref, gate_ref, o_ref, off, [slice(r0, r0 + rb) for r0 in range(0, n, rb)]))
        off += n

    def contrib(h):
        g = jnp.dot(h, wg_ref[...].astype(BF16), preferred_element_type=F32)
        u = jnp.dot(h, wu_ref[...].astype(BF16), preferred_element_type=F32)
        a = (g * jax.nn.sigmoid(g) * u).astype(BF16)
        return jnp.dot(a, wd_ref[...].astype(BF16), preferred_element_type=F32)

    @pl.when(j == 0)
    def _():
        for x_ref, sh_ref, sc_ref, _, o_ref, off, blocks in groups:
            for rows in blocks:
                h = (_rms(x_ref[rows, :], g_ref[...]) * (1.0 + sc_ref[0]) + sh_ref[0]).astype(BF16)
                h_scr[off + rows.start:off + rows.stop, :] = h
                o_ref[rows, :] = contrib(h)

    @pl.when((j > 0) & (j < last))
    def _():
        res = contrib(h_scr[...])
        for _, _, _, _, o_ref, off, _ in groups:
            o_ref[...] += res[off:off + o_ref.shape[0]]

    @pl.when(j == last)
    def _():
        for x_ref, _, _, gate_ref, o_ref, off, blocks in groups:
            for rows in blocks:
                acc = o_ref[rows, :] + contrib(h_scr[off + rows.start:off + rows.stop, :])
                y = x_ref[rows, :] + HALF * gate_ref[0] * acc
                if final_norm:
                    y = _rms(y, fin_ref[...])
                o_ref[rows, :] = y


def _ffn_call(x2d, mods, norm_g, w_up, w_down, fin_g, *, rows_per_mod, final_norm, name, extra=None):
    m, d = x2d.shape
    f = w_down.shape[0]
    tm = min(FFN_TM, m)
    n_tiles = m // tm
    tiles_per_mod = rows_per_mod // tm
    nf = f // FFN_TF
    vec_spec = pl.BlockSpec((1, d), lambda i, j: (0, 0))
    row_spec = lambda rows: pl.BlockSpec((rows, d), lambda i, j: (i, 0))
    mod_spec = pl.BlockSpec((1, 1, d), lambda i, j: (i // tiles_per_mod, 0, 0))
    const_mod_spec = pl.BlockSpec((1, 1, d), lambda i, j: (0, 0, 0))
    in_specs = [row_spec(tm), mod_spec, mod_spec, mod_spec]
    args = [x2d, *mods]
    out_rows = [tm]
    if extra is not None:
        x_e, mods_e = extra
        te = x_e.shape[0] // n_tiles
        assert te * n_tiles == x_e.shape[0] and te % (2 * SUBLANES) == 0
        in_specs += [row_spec(te), const_mod_spec, const_mod_spec, const_mod_spec]
        args += [x_e, *mods_e]
        out_rows.append(te)
    in_specs += [vec_spec,
                 pl.BlockSpec((d, FFN_TF), lambda i, j: (0, j)),
                 pl.BlockSpec((d, FFN_TF), lambda i, j: (0, j + nf)),
                 pl.BlockSpec((FFN_TF, d), lambda i, j: (j, 0)),
                 vec_spec]
    args += [norm_g, w_up, w_up, w_down, fin_g]
    return pl.pallas_call(
        functools.partial(_ffn_body, final_norm=final_norm, n_groups=len(out_rows)),
        grid=(n_tiles, nf),
        in_specs=in_specs,
        out_specs=[row_spec(r) for r in out_rows],
        out_shape=[jax.ShapeDtypeStruct((r * n_tiles, d), F32) for r in out_rows],
        scratch_shapes=[pltpu.VMEM((sum(out_rows), d), BF16)],
        compiler_params=_cparams(("parallel", "arbitrary")),
        name=name,
    )(*args)


def _wsplit_body(wt_ref, a_ref, b_ref, *, n_valid, n_a_blocks):
    j = pl.program_id(0)
    feat = j * WSPLIT_TN + lax.broadcasted_iota(jnp.int32, wt_ref.shape, 0)
    w = jnp.where(feat < n_valid, wt_ref[...], 0.0).T.astype(BF16)

    @pl.when(j < n_a_blocks)
    def _():
        a_ref[...] = w

    @pl.when(j >= n_a_blocks)
    def _():
        b_ref[...] = w


def _split_w_in(wt, n_a):
    n, d = wt.shape
    assert n_a % WSPLIT_TN == 0
    nb = pl.cdiv(n, WSPLIT_TN)
    na = n_a // WSPLIT_TN
    return pl.pallas_call(
        functools.partial(_wsplit_body, n_valid=n, n_a_blocks=na),
        grid=(nb,),
        in_specs=[pl.BlockSpec((WSPLIT_TN, d), lambda j: (j, 0))],
        out_specs=[pl.BlockSpec((d, WSPLIT_TN), lambda j: (0, jnp.minimum(j, na - 1))),
                   pl.BlockSpec((d, WSPLIT_TN), lambda j: (0, jnp.maximum(j - na, 0)))],
        out_shape=[jax.ShapeDtypeStruct((d, n_a), BF16),
                   jax.ShapeDtypeStruct((d, (nb - na) * WSPLIT_TN), BF16)],
        compiler_params=_cparams(("arbitrary",)),
        name="w_in_split",
    )(wt)


def _colmajor_perm(n_rows, n_cols, src0=0, n_src=None):
    assert n_rows & (n_rows - 1) == 0
    m = n_rows * n_cols
    n_src = m if n_src is None else n_src
    j = lax.broadcasted_iota(jnp.int32, (m, n_src), 0)
    i = lax.broadcasted_iota(jnp.int32, (m, n_src), 1) + src0
    src = (j & (n_rows - 1)) * n_cols + (j >> (n_rows.bit_length() - 1))
    return (i == src).astype(BF16)


def _proj_body(x_ref, g_ref, sh_ref, sc_ref, w_ref, *refs, widths, colmajor):
    out_refs, h_scr = refs[:-1], refs[-1]
    tm = h_scr.shape[0]
    rb = min(PROJ_ROWS, tm)
    norm = lambda x: (_rms(x, g_ref[...]) * (1.0 + sc_ref[0]) + sh_ref[0]).astype(BF16)
    if colmajor:
        r, c, d = x_ref.shape[1:]
        rows_per_block = rb // c
        hp = None
        for r0 in range(0, r, rows_per_block):
            h = norm(x_ref[0, r0:r0 + rows_per_block].reshape(rb, d))
            part = jnp.dot(_colmajor_perm(r, c, r0 * c, rb), h, preferred_element_type=F32)
            hp = part if hp is None else hp + part
        h_scr[...] = hp.astype(BF16)
    for r0 in range(0, tm, rb):
        rows = slice(r0, r0 + rb)
        h = h_scr[rows, :] if colmajor else norm(x_ref[0, rows, :])
        off = 0
        for o_ref, wdt in zip(out_refs, widths):
            o_ref[0, rows, :] = jnp.dot(h, w_ref[:, off:off + wdt],
                                        preferred_element_type=F32).astype(o_ref.dtype)
            off += wdt


def _proj_call(x, norm_g, shift, scale, w, widths, dtypes, *, tm, colmajor, name):
    b = x.shape[0]
    d = norm_g.shape[1]
    if colmajor:
        rows, gw = x.shape[1:3]
        cols = tm // rows
        assert cols == SUBLANES and gw % cols == 0
        n_tiles = gw // cols
        x_spec = pl.BlockSpec((1, rows, cols, d), lambda bi, i: (bi, 0, i, 0))
    else:
        n_tiles = x.shape[1] // tm
        x_spec = pl.BlockSpec((1, tm, d), lambda bi, i: (bi, i, 0))
    mod_spec = pl.BlockSpec((1, 1, d), lambda bi, i: (bi, 0, 0))
    return pl.pallas_call(
        functools.partial(_proj_body, widths=tuple(widths), colmajor=colmajor),
        grid=(b, n_tiles),
        in_specs=[x_spec,
                  pl.BlockSpec((1, d), lambda bi, i: (0, 0)),
                  mod_spec, mod_spec,
                  pl.BlockSpec(w.shape, lambda bi, i: (0, 0))],
        out_specs=[pl.BlockSpec((1, tm, wdt), lambda bi, i: (bi, i, 0)) for wdt in widths],
        out_shape=[jax.ShapeDtypeStruct((b, n_tiles * tm, wdt), dt) for wdt, dt in zip(widths, dtypes)],
        scratch_shapes=[pltpu.VMEM((tm, d), BF16)],
        compiler_params=_cparams(("parallel", "parallel")),
        name=name,
    )(x, norm_g, shift, scale, w)


def _lru_conv(x_ref, cw_ref, cb_ref, pad_scr, xc_scr, t):
    zeros = jnp.zeros((SUBLANES, LANES), F32)
    pad_scr[0:SUBLANES, :] = zeros
    pad_scr[SUBLANES:SUBLANES + t, :] = x_ref[0]
    pad_scr[SUBLANES + t:2 * SUBLANES + t, :] = zeros
    ch = min(LRU_GATE_ROWS, t)
    win_rows = ch + 2 * SUBLANES

    def body(c, carry):
        off = pl.multiple_of(c * ch, SUBLANES)
        win = pad_scr[pl.ds(off, win_rows), :]
        acc = cb_ref[...] + cw_ref[2:3, :] * win[SUBLANES:SUBLANES + ch]
        for k, shift in ((0, 2), (1, 1), (3, win_rows - 1)):
            acc = acc + cw_ref[k:k + 1, :] * pltpu.roll(win, shift, 0)[SUBLANES:SUBLANES + ch]
        xc_scr[pl.ds(off, ch), :] = acc
        return carry

    lax.fori_loop(0, t // ch, body, 0)


def _lru_gates(xc_scr, t, d, wr_ref, br_ref, wi_ref, bi_ref, lam_ref, a_scr, b_scr):
    ch = min(LRU_GATE_ROWS, t)
    half_unit = (-0.5 * LRU_C) * jax.nn.softplus(-lam_ref[d:d + 1, :])
    wr = (0.5 * wr_ref[d, 0]).astype(BF16)
    wi = (0.5 * wi_ref[d, 0]).astype(BF16)
    br = 0.5 * br_ref[d:d + 1, :]
    bi = 0.5 * bi_ref[d:d + 1, :]

    def body(c, carry):
        off = pl.multiple_of(c * ch, SUBLANES)
        xc = xc_scr[pl.ds(off, ch), :]
        xb = xc.astype(BF16)
        tr = jnp.tanh(jnp.dot(xb, wr, preferred_element_type=F32) + br)
        ti = jnp.tanh(jnp.dot(xb, wi, preferred_element_type=F32) + bi)
        log_a = half_unit * tr + half_unit
        a = jnp.exp(log_a)
        a_scr[pl.ds(off, ch), :] = a
        y = -jnp.tanh(log_a) * (a * a + 1.0)
        mult = jnp.where(y > 0.0, y * lax.rsqrt(y), 0.0)
        b_scr[pl.ds(off, ch), :] = (mult * xc) * (0.5 * ti + 0.5)
        return carry

    lax.fori_loop(0, t // ch, body, 0, unroll=min(4, t // ch))


def _lru_scan(a_ref, b_ref, t, d, h0, out_ref, accumulate=False):
    groups = t // SUBLANES
    row = lax.broadcasted_iota(jnp.int32, (SUBLANES, LANES), 0)

    def body(g, carry):
        gi = g if d == 0 else groups - 1 - g
        off = pl.multiple_of(gi * SUBLANES, SUBLANES)
        a = a_ref[pl.ds(off, SUBLANES), :]
        bv = b_ref[pl.ds(off, SUBLANES), :]
        for k in (1, 2, 4):
            shift = k if d == 0 else SUBLANES - k
            valid = (row >= k) if d == 0 else (row < SUBLANES - k)
            a_prev = pltpu.roll(a, shift, 0)
            b_prev = pltpu.roll(bv, shift, 0)
            bv = jnp.where(valid, a * b_prev + bv, bv)
            a = jnp.where(valid, a * a_prev, a)
        h = a * carry + bv
        if out_ref is not None:
            if accumulate:
                out_ref[pl.ds(off, SUBLANES), :] += h
            else:
                out_ref[pl.ds(off, SUBLANES), :] = h
        return h[SUBLANES - 1:SUBLANES, :] if d == 0 else h[0:1, :]

    return lax.fori_loop(0, groups, body, h0, unroll=min(4, groups))


def _lru_scan_planes(a_ref, b_ref, n, d, h0, out_ref, accumulate, levels):
    if n <= LRU_SEQ_SCAN_MAX:
        _lru_scan(a_ref, b_ref, n, d, h0, out_ref, accumulate)
        return
    g = n // SUBLANES
    p_scr, q_scr, s_scr = levels[0]
    order = list(range(SUBLANES)) if d == 0 else list(range(SUBLANES - 1, -1, -1))
    group_rows = SUBLANES * SUBLANES

    def up(v, carry):
        base = pl.multiple_of(v * group_rows, group_rows)
        rows = pl.ds(pl.multiple_of(v * SUBLANES, SUBLANES), SUBLANES)
        p = h = None
        for j in order:
            a = a_ref[pl.ds(base + j, SUBLANES, stride=SUBLANES), :]
            bv = b_ref[pl.ds(base + j, SUBLANES, stride=SUBLANES), :]
            if p is None:
                p, h = a, bv
            else:
                h = a * h + bv
                p = a * p
            p_scr[j, rows, :] = p
            q_scr[j, rows, :] = h
        return carry

    lax.fori_loop(0, g // SUBLANES, up, 0, unroll=4)

    last = order[-1]
    _lru_scan_planes(p_scr.at[last], q_scr.at[last], g, d, h0, s_scr, False, levels[1:])
    s = s_scr[...]
    row = lax.broadcasted_iota(jnp.int32, (g, LANES), 0)
    if d == 0:
        s_scr[...] = jnp.where(row == 0, h0, pltpu.roll(s, 1, 0))
    else:
        s_scr[...] = jnp.where(row == g - 1, h0, pltpu.roll(s, g - 1, 0))

    def down(v, carry):
        base = pl.multiple_of(v * group_rows, group_rows)
        rows = pl.ds(pl.multiple_of(v * SUBLANES, SUBLANES), SUBLANES)
        x = s_scr[rows, :]
        for j in range(SUBLANES):
            dst = pl.ds(base + j, SUBLANES, stride=SUBLANES)
            val = q_scr[j, rows, :] + p_scr[j, rows, :] * x
            if accumulate:
                val = val + out_ref[dst, :]
            out_ref[dst, :] = val
        return carry

    lax.fori_loop(0, g // SUBLANES, down, 0, unroll=4)


def _lru_body(lx_ref, lg_ref, lxc_ref, cw_ref, cb_ref, wr_ref, br_ref, wi_ref, bi_ref, lam_ref, o_ref,
              pad_scr, xc_scr, xcc_scr, a_scr, b_scr, h_scr, *level_scr):
    levels = [level_scr[i:i + 3] for i in range(0, len(level_scr), 3)]
    t = lx_ref.shape[1]
    tc = lxc_ref.shape[1]
    _lru_conv(lxc_ref, cw_ref, cb_ref, pad_scr, xcc_scr, tc)
    _lru_conv(lx_ref, cw_ref, cb_ref, pad_scr, xc_scr, t)
    gate_refs = (wr_ref, br_ref, wi_ref, bi_ref, lam_ref)
    for d in (0, 1):
        _lru_gates(xcc_scr, tc, d, *gate_refs, a_scr, b_scr)
        h0 = _lru_scan(a_scr, b_scr, tc, d, jnp.zeros((1, LANES), F32), None)
        _lru_gates(xc_scr, t, d, *gate_refs, a_scr, b_scr)
        _lru_scan_planes(a_scr, b_scr, t, d, h0, h_scr, d == 1, levels)
    o_ref[0] = (jax.nn.gelu(lg_ref[0]) * h_scr[...]).astype(o_ref.dtype)


def _lru_call(lx, lg, lxc, conv_w, conv_b, w_r, b_r, w_i, b_i, lam):
    b, t, d_lru = lx.shape
    tc = lxc.shape[1]
    nb = d_lru // LANES
    assert w_r.shape == (2, nb, LANES, LANES)
    seq = lambda n: pl.BlockSpec((1, n, LANES), lambda bi, j: (bi, 0, j))
    vec = lambda n: pl.BlockSpec((n, LANES), lambda bi, j: (0, j))
    wblk = pl.BlockSpec((2, 1, LANES, LANES), lambda bi, j: (0, j, 0, 0))
    level_scr = []
    n = t
    while n > LRU_SEQ_SCAN_MAX:
        assert n % (SUBLANES * SUBLANES) == 0
        n //= SUBLANES
        level_scr += [pltpu.VMEM((SUBLANES, n, LANES), F32), pltpu.VMEM((SUBLANES, n, LANES), F32),
                      pltpu.VMEM((n, LANES), F32)]
    return pl.pallas_call(
        _lru_body,
        grid=(b, nb),
        in_specs=[seq(t), seq(t), seq(tc), vec(CONV_W), vec(1), wblk, vec(2), wblk, vec(2), vec(2)],
        out_specs=seq(t),
        out_shape=jax.ShapeDtypeStruct((b, t, d_lru), BF16),
        scratch_shapes=[pltpu.VMEM((t + 2 * SUBLANES, LANES), F32),
                        pltpu.VMEM((t, LANES), F32), pltpu.VMEM((tc, LANES), F32),
                        pltpu.VMEM((t, LANES), F32), pltpu.VMEM((t, LANES), F32),
                        pltpu.VMEM((t, LANES), F32)] + level_scr,
        compiler_params=_cparams(("parallel", "parallel")),
        name="rglru",
    )(lx, lg, lxc, conv_w, conv_b, w_r, b_r, w_i, b_i, lam)


FWD_GATE_LANES = 2 * M_HEADS


def _split3_dot(tri, x):
    hi = x.astype(BF16)
    r1 = x - hi.astype(F32)
    mid = r1.astype(BF16)
    lo = (r1 - mid.astype(F32)).astype(BF16)
    return (jnp.dot(tri, hi, preferred_element_type=F32) + jnp.dot(tri, mid, preferred_element_type=F32)
            + jnp.dot(tri, lo, preferred_element_type=F32))


def _cummax_rows(x, row, reverse):
    n = x.shape[0]
    k = 1
    while k < n:
        if reverse:
            x = jnp.where(row < n - k, jnp.maximum(x, pltpu.roll(x, n - k, 0)), x)
        else:
            x = jnp.where(row >= k, jnp.maximum(x, pltpu.roll(x, k, 0)), x)
        k *= 2
    return x


def _round_up_bf16(x):
    return (x + jnp.abs(x) * (2.0 ** -7)).astype(BF16)


def _mlstm_gate_prep(gates, kidx, bc_scr, cm_scr, xt_scr, tot_scr, gmax_scr):
    L = gates.shape[0]
    ti = lax.broadcasted_iota(jnp.int32, (L, L), 0)
    si = lax.broadcasted_iota(jnp.int32, (L, L), 1)
    row = lax.broadcasted_iota(jnp.int32, (L, LANES), 0)
    fwd = lax.broadcasted_iota(jnp.int32, (L, LANES), 1) < FWD_GATE_LANES
    lf = jax.nn.log_sigmoid(gates)
    bc_f = _split3_dot((si <= ti).astype(BF16), lf)
    bc_b = _split3_dot((si >= ti).astype(BF16), lf)
    bc = jnp.where(fwd, bc_f, bc_b)
    tot = jnp.where(fwd[0:1], bc_f[L - 1:L], bc_b[0:1])
    x = pltpu.roll(gates, SUBLANES, 1) - bc
    cm = jnp.where(fwd, _cummax_rows(x, row, False), _cummax_rows(x, row, True))
    gmax = jnp.max(tot + x, axis=0, keepdims=True)
    xt = x.T
    bc_scr[kidx] = bc
    cm_scr[kidx] = _round_up_bf16(cm)
    xt_scr[kidx, 0:SUBLANES, :] = xt[SUBLANES:2 * SUBLANES]
    xt_scr[kidx, SUBLANES:2 * SUBLANES, :] = xt[3 * SUBLANES:4 * SUBLANES]
    tot_scr[kidx] = jnp.broadcast_to(tot, (SUBLANES, LANES))
    gmax_scr[kidx] = jnp.broadcast_to(gmax, (SUBLANES, LANES))


def _mlstm_chunk(q, v, kidx, step, d, scr, need_h, dk):
    bc_scr, cm_scr, xt_scr, kt_scr, tot_scr, m_in_scr, m_out_scr, s_scr = scr
    L = q.shape[0]
    ti = lax.broadcasted_iota(jnp.int32, (L, L), 0)
    si = lax.broadcasted_iota(jnp.int32, (L, L), 1)
    causal = (si <= ti) if d == 0 else (si >= ti)
    lane = lax.broadcasted_iota(jnp.int32, (L, LANES), 1)
    head0_rows = lax.broadcasted_iota(jnp.int32, (LANES, L), 0) < dk
    srow = lax.broadcasted_iota(jnp.int32, (LANES, 2 * LANES), 0)
    k_t = kt_scr[kidx]
    tot_v, m_in_v, m_out_v = tot_scr[kidx], m_in_scr[step], m_out_scr[step]
    ones = jnp.ones((L, LANES), BF16)
    v1 = jnp.concatenate([v[:, :LANES], ones, v[:, LANES:], ones], axis=1)
    cf0 = FWD_GATE_LANES * d + SUBLANES
    pick = lambda vals, e: vals[0:1, cf0 + e:cf0 + e + 1]
    x_rows = [xt_scr[kidx, SUBLANES * d + e:SUBLANES * d + e + 1, :] for e in (0, 1)]
    s_old = s_scr[d]
    hs = []
    if need_h:
        bc = bc_scr[kidx]
        sel_lane = lax.broadcasted_iota(jnp.int32, (LANES, 2 * LANES), 0)
        sel_col = lax.broadcasted_iota(jnp.int32, (LANES, 2 * LANES), 1)
        sel = (sel_lane == jnp.where(sel_col < LANES, cf0, cf0 + 1)).astype(BF16)
        mx_all = jnp.maximum(_round_up_bf16(m_in_v[0:1, :]), cm_scr[kidx])
        mx_tiles = jnp.dot(mx_all, sel, preferred_element_type=F32)
        q_heads = [jnp.where((lane >= e * dk) & (lane < (e + 1) * dk), q, jnp.zeros_like(q)) for e in (0, 1)]
        q2 = jnp.concatenate(q_heads, axis=0)
        qk = jnp.dot(q2, k_t, preferred_element_type=F32)
        qs = jnp.dot(q2, s_old.astype(BF16), preferred_element_type=F32)
        for e in (0, 1):
            cf = cf0 + e
            mx = mx_tiles[:, e * LANES:(e + 1) * LANES]
            w = qk[e * L:(e + 1) * L] * jnp.exp(jnp.where(causal, x_rows[e] - mx, -jnp.inf))
            s_inter = jnp.exp(pick(m_in_v, e) - mx)
            wv = jnp.dot(w.astype(BF16), v1[:, 2 * e * LANES:2 * (e + 1) * LANES],
                         preferred_element_type=F32)
            qs_e = qs[e * L:(e + 1) * L]
            num = s_inter * qs_e[:, :LANES] + wv[:, :LANES]
            den = s_inter * qs_e[:, LANES:] + wv[:, LANES:]
            m_row = bc[:, cf:cf + 1] + mx[:, cf:cf + 1]
            hs.append(num * (1.0 / jnp.maximum(jnp.abs(den[:, cf:cf + 1]), jnp.exp(-m_row))))
    wg_rows = [jnp.exp(pick(tot_v, e) + x_rows[e] - pick(m_out_v, e)) for e in (0, 1)]
    decays = [jnp.exp(pick(tot_v, e) + pick(m_in_v, e) - pick(m_out_v, e)) for e in (0, 1)]
    kw = (k_t.astype(F32) * jnp.where(head0_rows, wg_rows[0], wg_rows[1])).astype(BF16)
    own = jnp.concatenate([jnp.dot(kw[:dk], v1[:, :2 * LANES], preferred_element_type=F32),
                           jnp.dot(kw[dk:], v1[:, 2 * LANES:], preferred_element_type=F32)], axis=0)
    s_scr[d] = jnp.where(srow < dk, decays[0], decays[1]) * s_old + own
    return hs


def _mlstm_body(q_ref, k_ref, v_ref, o_ref, gt_ref, qc_ref, kc_ref, vc_ref, gtc_ref, bias_ref, gain_ref,
                out_ref, hs_scr, bc_scr, cm_scr, xt_scr, kt_scr, tot_scr, gmax_scr, m_in_scr, m_out_scr,
                s_scr, *, dk):
    t = q_ref.shape[1]
    tc = qc_ref.shape[1]
    L = CHUNK
    nc, ncc = t // L, tc // L
    n_steps = nc + ncc
    shift = (LANES - 2 * pl.program_id(1)) % LANES
    bias = pltpu.roll(jnp.broadcast_to(bias_ref[...], (SUBLANES, LANES)), shift, 1)[0:1, :]
    prep_scr = (bc_scr, cm_scr, xt_scr, tot_scr, gmax_scr)
    state_scr = (bc_scr, cm_scr, xt_scr, kt_scr, tot_scr, m_in_scr, m_out_scr, s_scr)
    scaled_t = lambda k: (k.astype(F32).T * (dk ** -0.5)).astype(BF16)

    for c in range(ncc):
        rows = slice(c * L, (c + 1) * L)
        _mlstm_gate_prep(pltpu.roll(gtc_ref[0, rows, :], shift, 1) + bias, c, *prep_scr)
        kt_scr[c] = scaled_t(kc_ref[0, rows, :])

    def prep(c, carry):
        rows = pl.ds(pl.multiple_of(c * L, L), L)
        _mlstm_gate_prep(pltpu.roll(gt_ref[0, rows, :], shift, 1) + bias, c + ncc, *prep_scr)
        kt_scr[c + ncc] = scaled_t(k_ref[0, rows, :])
        return carry

    lax.fori_loop(0, nc, prep, 0, unroll=4)

    fwd_id = lambda i: i
    bwd_id = lambda i: (ncc - 1 - i) if i < ncc else (n_steps - 1 - (i - ncc))
    fwd8 = lax.broadcasted_iota(jnp.int32, (SUBLANES, LANES), 1) < FWD_GATE_LANES
    m = jnp.zeros((SUBLANES, LANES), F32)
    for i in range(n_steps):
        tot = jnp.where(fwd8, tot_scr[fwd_id(i)], tot_scr[bwd_id(i)])
        gmax = jnp.where(fwd8, gmax_scr[fwd_id(i)], gmax_scr[bwd_id(i)])
        m_in_scr[i] = m
        m = jnp.maximum(tot + m, gmax)
        m_out_scr[i] = m

    s_scr[...] = jnp.zeros_like(s_scr)
    hs_scr[...] = jnp.zeros_like(hs_scr)
    for i in range(ncc):
        for d, cid in ((0, fwd_id(i)), (1, bwd_id(i))):
            rows = slice(cid * L, (cid + 1) * L)
            _mlstm_chunk(qc_ref[0, rows, :], vc_ref[0, rows, :], cid, i, d, state_scr, False, dk)

    def body(j, carry):
        for d in (0, 1):
            cj = j if d == 0 else nc - 1 - j
            rows = pl.ds(pl.multiple_of(cj * L, L), L)
            hs = _mlstm_chunk(q_ref[0, rows, :], v_ref[0, rows, :], cj + ncc, j + ncc, d, state_scr, True, dk)
            hs_scr[rows, :] += jnp.concatenate(hs, axis=1)
        return carry

    lax.fori_loop(0, nc, body, 0, unroll=8)

    def fin(c, carry):
        rows = pl.ds(pl.multiple_of(c * L, L), L)
        hh = hs_scr[rows, :]
        outs = []
        for e in (0, 1):
            x = hh[:, e * LANES:(e + 1) * LANES]
            outs.append(x * lax.rsqrt(jnp.mean(x * x, axis=-1, keepdims=True) + EPS))
        y = jnp.concatenate(outs, axis=1) * gain_ref[...] * _sigmoid(o_ref[0, rows, :])
        out_ref[0, rows, :] = y.astype(out_ref.dtype)
        return carry

    lax.fori_loop(0, nc, fin, 0, unroll=2)


def _mlstm_call(q, k, v, o, gt, qc, kc, vc, gtc, bias, gain):
    b, t, d_qk = q.shape
    tc = qc.shape[1]
    d_v = v.shape[2]
    pairs = M_HEADS // 2
    dk = d_qk // M_HEADS
    assert d_qk // pairs == LANES and d_v // pairs == 2 * LANES and CHUNK == LANES
    n_chunks = (t + tc) // CHUNK
    seq = lambda n, w: pl.BlockSpec((1, n, w), lambda bi, p: (bi, 0, p))
    allg = lambda n: pl.BlockSpec((1, n, LANES), lambda bi, p: (bi, 0, 0))
    chunk_f32 = lambda rows, cols: pltpu.VMEM((n_chunks, rows, cols), F32)
    return pl.pallas_call(
        functools.partial(_mlstm_body, dk=dk),
        grid=(b, pairs),
        in_specs=[seq(t, LANES), seq(t, LANES), seq(t, 2 * LANES), seq(t, 2 * LANES), allg(t),
                  seq(tc, LANES), seq(tc, LANES), seq(tc, 2 * LANES), allg(tc),
                  pl.BlockSpec((1, LANES), lambda bi, p: (0, 0)),
                  pl.BlockSpec((1, 2 * LANES), lambda bi, p: (0, p))],
        out_specs=seq(t, 2 * LANES),
        out_shape=jax.ShapeDtypeStruct((b, t, d_v), F32),
        scratch_shapes=[pltpu.VMEM((t, 2 * LANES), F32),
                        chunk_f32(CHUNK, LANES), pltpu.VMEM((n_chunks, CHUNK, LANES), BF16),
                        chunk_f32(2 * SUBLANES, CHUNK),
                        pltpu.VMEM((n_chunks, LANES, CHUNK), BF16),
                        chunk_f32(SUBLANES, LANES), chunk_f32(SUBLANES, LANES),
                        chunk_f32(SUBLANES, LANES), chunk_f32(SUBLANES, LANES),
                        pltpu.VMEM((2, LANES, 2 * LANES), F32)],
        compiler_params=_cparams(("parallel", "parallel")),
        name="mlstm",
    )(q, k, v, o, gt, qc, kc, vc, gtc, bias, gain)


def _outproj_body(x_ref, lru_ref, mls_ref, wa_ref, wb_ref, g_ref, o_ref):
    w_cols, r, dm = mls_ref.shape[1:]
    m = mls_ref[0].reshape(w_cols * r, dm).astype(BF16)
    m = jnp.dot(_colmajor_perm(w_cols, r), m, preferred_element_type=F32).astype(BF16)
    y = (jnp.dot(lru_ref[0], wa_ref[...], preferred_element_type=F32)
         + jnp.dot(m, wb_ref[...], preferred_element_type=F32))
    o_ref[0] = x_ref[0] + g_ref[0] * y


def _outproj_call(x3d, lru, mls_cm, w_a, w_b, gate):
    b, t, d = x3d.shape
    da, dm = lru.shape[2], mls_cm.shape[2]
    rows = t // GRID_W
    r_tile = PROJ_TM // GRID_W
    assert r_tile == SUBLANES
    mls_view = mls_cm.reshape(b, GRID_W, rows, dm)
    return pl.pallas_call(
        _outproj_body,
        grid=(b, t // PROJ_TM),
        in_specs=[pl.BlockSpec((1, PROJ_TM, d), lambda bi, i: (bi, i, 0)),
                  pl.BlockSpec((1, PROJ_TM, da), lambda bi, i: (bi, i, 0)),
                  pl.BlockSpec((1, GRID_W, r_tile, dm), lambda bi, i: (bi, 0, i, 0)),
                  pl.BlockSpec((da, d), lambda bi, i: (0, 0)),
                  pl.BlockSpec((dm, d), lambda bi, i: (0, 0)),
                  pl.BlockSpec((1, 1, d), lambda bi, i: (bi, 0, 0))],
        out_specs=pl.BlockSpec((1, PROJ_TM, d), lambda bi, i: (bi, i, 0)),
        out_shape=jax.ShapeDtypeStruct((b, t, d), F32),
        compiler_params=_cparams(("parallel", "parallel")),
        name="outproj",
    )(x3d, lru, mls_view, w_a, w_b, gate)


def kernel(x, c, ctx, c_ctx, ada_w, ada_b, ffn1_norm, ffn1_w_up, ffn1_w_down, mix_norm, w_in, b_mgate, lru_conv_w, lru_conv_b, lru_w_r, lru_b_r, lru_w_i, lru_b_i, lru_lam, mlstm_norm, w_out, ffn2_norm, ffn2_w_up, ffn2_w_down, final_norm):
    b, t, d = x.shape
    tc = ctx.shape[1]
    assert ada_w.shape[0] == 1, "single-layer block only"
    assert t % GRID_W == 0 and t % PROJ_TM == 0
    rows = t // GRID_W
    d_lru = lru_conv_w.shape[2]
    d_mv = mlstm_norm.shape[1]
    d_mqk = (w_in.shape[2] - 2 * d_lru - 2 * d_mv - 4 * M_HEADS) // 2

    pad = SUBLANES - b - 1
    cc = jnp.concatenate([c, c_ctx[None, :], jnp.zeros((pad, d), F32)], axis=0)
    mod = _ada_call(cc, ada_w[0], ada_b[0][None, :]).reshape(SUBLANES, N_MOD, d)
    lat = lambda i: mod[:b, i][:, None, :]
    cxt = lambda i: mod[b:b + 1, i][:, None, :]
    row = lambda v: v[0][None, :]

    x1, ctx1 = _ffn_call(x.reshape(b * t, d), (lat(0), lat(1), lat(2)), row(ffn1_norm), ffn1_w_up[0],
                         ffn1_w_down[0], row(ffn1_norm), rows_per_mod=t, final_norm=False, name="ffn_pre",
                         extra=(ctx.reshape(b * tc, d), (cxt(0), cxt(1), cxt(2))))
    x1 = x1.reshape(b, t, d)
    ctx1 = ctx1.reshape(b, tc, d)

    n_gate = 4 * M_HEADS
    w_a, w_b = _split_w_in(w_in[0].T, 2 * d_lru)
    widths_a, dtypes_a = (d_lru, d_lru), (F32, F32)
    widths_b, dtypes_b = (d_mqk, d_mqk, d_mv, d_mv, LANES), (BF16, BF16, BF16, F32, F32)
    mixn = row(mix_norm)
    lx, lg = _proj_call(x1, mixn, lat(3), lat(4), w_a, widths_a, dtypes_a, tm=PROJ_TM, colmajor=False,
                        name="proj_lru")
    q, k, v, o, gt = _proj_call(x1.reshape(b, rows, GRID_W, d), mixn, lat(3), lat(4), w_b, widths_b,
                                dtypes_b, tm=PROJ_TM, colmajor=True, name="proj_mlstm")
    ctx_mod = lambda i: jnp.broadcast_to(cxt(i), (b, 1, d))
    lxc, _ = _proj_call(ctx1, mixn, ctx_mod(3), ctx_mod(4), w_a, widths_a, dtypes_a, tm=tc, colmajor=False,
                        name="proj_lru_ctx")
    qc, kc, vc, _, gtc = _proj_call(ctx1, mixn, ctx_mod(3), ctx_mod(4), w_b, widths_b, dtypes_b, tm=tc,
                                    colmajor=False, name="proj_mlstm_ctx")

    lru_lat = _lru_call(lx, lg, lxc, lru_conv_w[0], row(lru_conv_b), lru_w_r[0], lru_b_r[0], lru_w_i[0],
                        lru_b_i[0], lru_lam[0])
    bias = jnp.concatenate([b_mgate[0], jnp.zeros((LANES - n_gate,), F32)])[None, :]
    mls_cm = _mlstm_call(q, k, v, o, gt, qc, kc, vc, gtc, bias, row(mlstm_norm))

    w_out0 = w_out[0].astype(BF16)
    x2 = _outproj_call(x1, lru_lat, mls_cm, w_out0[:d_lru], w_out0[d_lru:], lat(5))

    out, = _ffn_call(x2.reshape(b * t, d), (lat(6), lat(7), lat(8)), row(ffn2_norm), ffn2_w_up[0],
                     ffn2_w_down[0], final_norm[None, :], rows_per_mod=t, final_norm=True, name="ffn_post")
    return out.reshape(b, t, d)
```

```python
import functools

import jax
import jax.numpy as jnp
from jax import lax
from jax.experimental import pallas as pl
from jax.experimental.pallas import tpu as pltpu

F32 = jnp.float32
BF16 = jnp.bfloat16

GRID_W = 64
LRU_BLOCKS = 8
CONV_W = 4
LRU_C = 8.0
M_HEADS = 8
CHUNK = 128
N_MOD = 9
EPS = 1e-6
HALF = 0.5

LANES = 128
SUBLANES = 8
VMEM_LIMIT_BYTES = 58 * 1024 * 1024

FFN_TM = 1024
FFN_TF = 256
FFN_EDGE_ROWS = 512
PROJ_ROWS = 256
PROJ_TM = 512
LRU_GATE_ROWS = 512
LRU_SEQ_SCAN_MAX = 64
ADA_TN = 1024
WSPLIT_TN = 512


def _cparams(sem):
    return pltpu.CompilerParams(dimension_semantics=sem, vmem_limit_bytes=VMEM_LIMIT_BYTES)


def _sigmoid(z):
    return 0.5 * jnp.tanh(0.5 * z) + 0.5


def _rms(x, g):
    return x * lax.rsqrt(jnp.mean(x * x, axis=-1, keepdims=True) + EPS) * g


def _ada_body(c_ref, w_ref, b_ref, o_ref):
    s = c_ref[...]
    s = s * jax.nn.sigmoid(s)
    o_ref[...] = jnp.dot(s, w_ref[...], preferred_element_type=F32) + b_ref[...]


def _ada_call(cc, w, b):
    rows, d = cc.shape
    n = w.shape[1]
    return pl.pallas_call(
        _ada_body,
        grid=(n // ADA_TN,),
        in_specs=[pl.BlockSpec((rows, d), lambda j: (0, 0)),
                  pl.BlockSpec((d, ADA_TN), lambda j: (0, j)),
                  pl.BlockSpec((1, ADA_TN), lambda j: (0, j))],
        out_specs=pl.BlockSpec((rows, ADA_TN), lambda j: (0, j)),
        out_shape=jax.ShapeDtypeStruct((rows, n), F32),
        compiler_params=_cparams(("arbitrary",)),
        name="ada_mod",
    )(cc, w, b)


def _ffn_body(*refs, final_norm, n_groups):
    groups_in = [refs[4 * k:4 * k + 4] for k in range(n_groups)]
    g_ref, wg_ref, wu_ref, wd_ref, fin_ref = refs[4 * n_groups:4 * n_groups + 5]
    out_refs = refs[4 * n_groups + 5:5 * n_groups + 5]
    h_scr = refs[-1]
    j = pl.program_id(1)
    last = pl.num_programs(1) - 1
    groups, off = [], 0
    for (x_ref, sh_ref, sc_ref, gate_ref), o_ref in zip(groups_in, out_refs):
        n = x_ref.shape[0]
        blocks = lambda rb: [slice(r0, r0 + min(rb, n)) for r0 in range(0, n, min(rb, n))]
        groups.append((x_ref, sh_ref, sc_ref, gate_ref, o_ref, off, blocks(FFN_EDGE_ROWS),
                       blocks(FFN_EDGE_ROWS // 2 if final_norm else FFN_EDGE_ROWS)))
        off += n

    def contrib(h):
        g = jnp.dot(h, wg_ref[...].astype(BF16), preferred_element_type=F32)
        u = jnp.dot(h, wu_ref[...].astype(BF16), preferred_element_type=F32)
        a = (g * jax.nn.sigmoid(g) * u).astype(BF16)
        return jnp.dot(a, wd_ref[...].astype(BF16), preferred_element_type=F32)

    @pl.when(j == 0)
    def _():
        for x_ref, sh_ref, sc_ref, _, o_ref, off, blocks, _ in groups:
            for rows in blocks:
                h = (_rms(x_ref[rows, :], g_ref[...]) * (1.0 + sc_ref[0]) + sh_ref[0]).astype(BF16)
                h_scr[off + rows.start:off + rows.stop, :] = h
                o_ref[rows, :] = contrib(h)

    @pl.when((j > 0) & (j < last))
    def _():
        res = contrib(h_scr[...])
        for _, _, _, _, o_ref, off, _, _ in groups:
            o_ref[...] += res[off:off + o_ref.shape[0]]

    @pl.when(j == last)
    def _():
        for x_ref, _, _, gate_ref, o_ref, off, _, blocks in groups:
            for rows in blocks:
                acc = o_ref[rows, :] + contrib(h_scr[off + rows.start:off + rows.stop, :])
                y = x_ref[rows, :] + HALF * gate_ref[0] * acc
                if final_norm:
                    y = _rms(y, fin_ref[...])
                o_ref[rows, :] = y


def _ffn_call(x2d, mods, norm_g, w_up, w_down, fin_g, *, rows_per_mod, final_norm, name, extra=None):
    m, d = x2d.shape
    f = w_down.shape[0]
    tm = min(FFN_TM, m)
    n_tiles = m // tm
    tiles_per_mod = rows_per_mod // tm
    nf = f // FFN_TF
    vec_spec = pl.BlockSpec((1, d), lambda i, j: (0, 0))
    row_spec = lambda rows: pl.BlockSpec((rows, d), lambda i, j: (i, 0))
    mod_spec = pl.BlockSpec((1, 1, d), lambda i, j: (i // tiles_per_mod, 0, 0))
    const_mod_spec = pl.BlockSpec((1, 1, d), lambda i, j: (0, 0, 0))
    in_specs = [row_spec(tm), mod_spec, mod_spec, mod_spec]
    args = [x2d, *mods]
    out_rows = [tm]
    if extra is not None:
        x_e, mods_e = extra
        te = x_e.shape[0] // n_tiles
        assert te * n_tiles == x_e.shape[0] and te % (2 * SUBLANES) == 0
        in_specs += [row_spec(te), const_mod_spec, const_mod_spec, const_mod_spec]
        args += [x_e, *mods_e]
        out_rows.append(te)
    in_specs += [vec_spec,
                 pl.BlockSpec((d, FFN_TF), lambda i, j: (0, j)),
                 pl.BlockSpec((d, FFN_TF), lambda i, j: (0, j + nf)),
                 pl.BlockSpec((FFN_TF, d), lambda i, j: (j, 0)),
                 vec_spec]
    args += [norm_g, w_up, w_up, w_down, fin_g]
    return pl.pallas_call(
        functools.partial(_ffn_body, final_norm=final_norm, n_groups=len(out_rows)),
        grid=(n_tiles, nf),
        in_specs=in_specs,
        out_specs=[row_spec(r) for r in out_rows],
        out_shape=[jax.ShapeDtypeStruct((r * n_tiles, d), F32) for r in out_rows],
        scratch_shapes=[pltpu.VMEM((sum(out_rows), d), BF16)],
        compiler_params=_cparams(("parallel", "arbitrary")),
        name=name,
    )(*args)


def _wsplit_body(wt_ref, a_ref, b_ref, *, n_valid, n_a_blocks):
    j = pl.program_id(0)
    feat = j * WSPLIT_TN + lax.broadcasted_iota(jnp.int32, wt_ref.shape, 0)
    w = jnp.where(feat < n_valid, wt_ref[...], 0.0).T.astype(BF16)

    @pl.when(j < n_a_blocks)
    def _():
        a_ref[...] = w

    @pl.when(j >= n_a_blocks)
    def _():
        b_ref[...] = w


def _split_w_in(wt, n_a):
    n, d = wt.shape
    assert n_a % WSPLIT_TN == 0
    nb = pl.cdiv(n, WSPLIT_TN)
    na = n_a // WSPLIT_TN
    return pl.pallas_call(
        functools.partial(_wsplit_body, n_valid=n, n_a_blocks=na),
        grid=(nb,),
        in_specs=[pl.BlockSpec((WSPLIT_TN, d), lambda j: (j, 0))],
        out_specs=[pl.BlockSpec((d, WSPLIT_TN), lambda j: (0, jnp.minimum(j, na - 1))),
                   pl.BlockSpec((d, WSPLIT_TN), lambda j: (0, jnp.maximum(j - na, 0)))],
        out_shape=[jax.ShapeDtypeStruct((d, n_a), BF16),
                   jax.ShapeDtypeStruct((d, (nb - na) * WSPLIT_TN), BF16)],
        compiler_params=_cparams(("arbitrary",)),
        name="w_in_split",
    )(wt)


def _colmajor_perm(n_rows, n_cols, src0=0, n_src=None):
    assert n_rows & (n_rows - 1) == 0
    m = n_rows * n_cols
    n_src = m if n_src is None else n_src
    j = lax.broadcasted_iota(jnp.int32, (m, n_src), 0)
    i = lax.broadcasted_iota(jnp.int32, (m, n_src), 1) + src0
    src = (j & (n_rows - 1)) * n_cols + (j >> (n_rows.bit_length() - 1))
    return (i == src).astype(BF16)


def _proj_body(x_ref, g_ref, sh_ref, sc_ref, w_ref, *refs, widths, colmajor):
    out_refs, h_scr = refs[:-1], refs[-1]
    tm = h_scr.shape[0]
    rb = min(PROJ_ROWS, tm)
    norm = lambda x: (_rms(x, g_ref[...]) * (1.0 + sc_ref[0]) + sh_ref[0]).astype(BF16)
    if colmajor:
        r, c, d = x_ref.shape[1:]
        rows_per_block = rb // c
        hp = None
        for r0 in range(0, r, rows_per_block):
            h = norm(x_ref[0, r0:r0 + rows_per_block].reshape(rb, d))
            part = jnp.dot(_colmajor_perm(r, c, r0 * c, rb), h, preferred_element_type=F32)
            hp = part if hp is None else hp + part
        h_scr[...] = hp.astype(BF16)
    for r0 in range(0, tm, rb):
        rows = slice(r0, r0 + rb)
        h = h_scr[rows, :] if colmajor else norm(x_ref[0, rows, :])
        off = 0
        for o_ref, wdt in zip(out_refs, widths):
            o_ref[0, rows, :] = jnp.dot(h, w_ref[:, off:off + wdt],
                                        preferred_element_type=F32).astype(o_ref.dtype)
            off += wdt


def _proj_call(x, norm_g, shift, scale, w, widths, dtypes, *, tm, colmajor, name):
    b = x.shape[0]
    d = norm_g.shape[1]
    if colmajor:
        rows, gw = x.shape[1:3]
        cols = tm // rows
        assert cols == SUBLANES and gw % cols == 0
        n_tiles = gw // cols
        x_spec = pl.BlockSpec((1, rows, cols, d), lambda bi, i: (bi, 0, i, 0))
    else:
        n_tiles = x.shape[1] // tm
        x_spec = pl.BlockSpec((1, tm, d), lambda bi, i: (bi, i, 0))
    mod_spec = pl.BlockSpec((1, 1, d), lambda bi, i: (bi, 0, 0))
    return pl.pallas_call(
        functools.partial(_proj_body, widths=tuple(widths), colmajor=colmajor),
        grid=(b, n_tiles),
        in_specs=[x_spec,
                  pl.BlockSpec((1, d), lambda bi, i: (0, 0)),
                  mod_spec, mod_spec,
                  pl.BlockSpec(w.shape, lambda bi, i: (0, 0))],
        out_specs=[pl.BlockSpec((1, tm, wdt), lambda bi, i: (bi, i, 0)) for wdt in widths],
        out_shape=[jax.ShapeDtypeStruct((b, n_tiles * tm, wdt), dt) for wdt, dt in zip(widths, dtypes)],
        scratch_shapes=[pltpu.VMEM((tm, d), BF16)],
        compiler_params=_cparams(("parallel", "parallel")),
        name=name,
    )(x, norm_g, shift, scale, w)


def _lru_conv(x_ref, cw_ref, cb_ref, pad_scr, xc_scr, t):
    zeros = jnp.zeros((SUBLANES, LANES), F32)
    pad_scr[0:SUBLANES, :] = zeros
    pad_scr[SUBLANES:SUBLANES + t, :] = x_ref[0]
    pad_scr[SUBLANES + t:2 * SUBLANES + t, :] = zeros
    ch = min(LRU_GATE_ROWS, t)
    win_rows = ch + 2 * SUBLANES

    def body(c, carry):
        off = pl.multiple_of(c * ch, SUBLANES)
        win = pad_scr[pl.ds(off, win_rows), :]
        acc = cb_ref[...] + cw_ref[2:3, :] * win[SUBLANES:SUBLANES + ch]
        for k, shift in ((0, 2), (1, 1), (3, win_rows - 1)):
            acc = acc + cw_ref[k:k + 1, :] * pltpu.roll(win, shift, 0)[SUBLANES:SUBLANES + ch]
        xc_scr[pl.ds(off, ch), :] = acc
        return carry

    lax.fori_loop(0, t // ch, body, 0)


def _lru_gates(xc_scr, t, d, wr_ref, br_ref, wi_ref, bi_ref, lam_ref, a_scr, b_scr):
    ch = min(LRU_GATE_ROWS, t)
    half_unit = (-0.5 * LRU_C) * jax.nn.softplus(-lam_ref[d:d + 1, :])
    wr = (0.5 * wr_ref[d, 0]).astype(BF16)
    wi = (0.5 * wi_ref[d, 0]).astype(BF16)
    br = 0.5 * br_ref[d:d + 1, :]
    bi = 0.5 * bi_ref[d:d + 1, :]

    def body(c, carry):
        off = pl.multiple_of(c * ch, SUBLANES)
        xc = xc_scr[pl.ds(off, ch), :]
        xb = xc.astype(BF16)
        tr = jnp.tanh(jnp.dot(xb, wr, preferred_element_type=F32) + br)
        ti = jnp.tanh(jnp.dot(xb, wi, preferred_element_type=F32) + bi)
        log_a = half_unit * tr + half_unit
        a = jnp.exp(log_a)
        a_scr[pl.ds(off, ch), :] = a
        y = -jnp.tanh(log_a) * (a * a + 1.0)
        mult = jnp.where(y > 0.0, y * lax.rsqrt(y), 0.0)
        b_scr[pl.ds(off, ch), :] = (mult * xc) * (0.5 * ti + 0.5)
        return carry

    lax.fori_loop(0, t // ch, body, 0, unroll=min(4, t // ch))


def _lru_scan(a_ref, b_ref, t, d, h0, out_ref, accumulate=False):
    groups = t // SUBLANES
    row = lax.broadcasted_iota(jnp.int32, (SUBLANES, LANES), 0)

    def body(g, carry):
        gi = g if d == 0 else groups - 1 - g
        off = pl.multiple_of(gi * SUBLANES, SUBLANES)
        a = a_ref[pl.ds(off, SUBLANES), :]
        bv = b_ref[pl.ds(off, SUBLANES), :]
        for k in (1, 2, 4):
            shift = k if d == 0 else SUBLANES - k
            valid = (row >= k) if d == 0 else (row < SUBLANES - k)
            a_prev = pltpu.roll(a, shift, 0)
            b_prev = pltpu.roll(bv, shift, 0)
            bv = jnp.where(valid, a * b_prev + bv, bv)
            a = jnp.where(valid, a * a_prev, a)
        h = a * carry + bv
        if out_ref is not None:
            if accumulate:
                out_ref[pl.ds(off, SUBLANES), :] += h
            else:
                out_ref[pl.ds(off, SUBLANES), :] = h
        return h[SUBLANES - 1:SUBLANES, :] if d == 0 else h[0:1, :]

    return lax.fori_loop(0, groups, body, h0, unroll=min(4, groups))


def _lru_scan_planes(a_ref, b_ref, n, d, h0, out_ref, accumulate, levels):
    if n <= LRU_SEQ_SCAN_MAX:
        _lru_scan(a_ref, b_ref, n, d, h0, out_ref, accumulate)
        return
    g = n // SUBLANES
    p_scr, q_scr, s_scr = levels[0]
    order = list(range(SUBLANES)) if d == 0 else list(range(SUBLANES - 1, -1, -1))
    group_rows = SUBLANES * SUBLANES

    def up(v, carry):
        base = pl.multiple_of(v * group_rows, group_rows)
        rows = pl.ds(pl.multiple_of(v * SUBLANES, SUBLANES), SUBLANES)
        p = h = None
        for j in order:
            a = a_ref[pl.ds(base + j, SUBLANES, stride=SUBLANES), :]
            bv = b_ref[pl.ds(base + j, SUBLANES, stride=SUBLANES), :]
            if p is None:
                p, h = a, bv
            else:
                h = a * h + bv
                p = a * p
            p_scr[j, rows, :] = p
            q_scr[j, rows, :] = h
        return carry

    lax.fori_loop(0, g // SUBLANES, up, 0, unroll=4)

    last = order[-1]
    _lru_scan_planes(p_scr.at[last], q_scr.at[last], g, d, h0, s_scr, False, levels[1:])
    s = s_scr[...]
    row = lax.broadcasted_iota(jnp.int32, (g, LANES), 0)
    if d == 0:
        s_scr[...] = jnp.where(row == 0, h0, pltpu.roll(s, 1, 0))
    else:
        s_scr[...] = jnp.where(row == g - 1, h0, pltpu.roll(s, g - 1, 0))

    def down(v, carry):
        base = pl.multiple_of(v * group_rows, group_rows)
        rows = pl.ds(pl.multiple_of(v * SUBLANES, SUBLANES), SUBLANES)
        x = s_scr[rows, :]
        for j in range(SUBLANES):
            dst = pl.ds(base + j, SUBLANES, stride=SUBLANES)
            val = q_scr[j, rows, :] + p_scr[j, rows, :] * x
            if accumulate:
                val = val + out_ref[dst, :]
            out_ref[dst, :] = val
        return carry

    lax.fori_loop(0, g // SUBLANES, down, 0, unroll=4)


def _lru_body(lx_ref, lg_ref, lxc_ref, cw_ref, cb_ref, wr_ref, br_ref, wi_ref, bi_ref, lam_ref, o_ref,
              pad_scr, xc_scr, xcc_scr, a_scr, b_scr, h_scr, *level_scr):
    levels = [level_scr[i:i + 3] for i in range(0, len(level_scr), 3)]
    t = lx_ref.shape[1]
    tc = lxc_ref.shape[1]
    _lru_conv(lxc_ref, cw_ref, cb_ref, pad_scr, xcc_scr, tc)
    _lru_conv(lx_ref, cw_ref, cb_ref, pad_scr, xc_scr, t)
    gate_refs = (wr_ref, br_ref, wi_ref, bi_ref, lam_ref)
    for d in (0, 1):
        _lru_gates(xcc_scr, tc, d, *gate_refs, a_scr, b_scr)
        h0 = _lru_scan(a_scr, b_scr, tc, d, jnp.zeros((1, LANES), F32), None)
        _lru_gates(xc_scr, t, d, *gate_refs, a_scr, b_scr)
        _lru_scan_planes(a_scr, b_scr, t, d, h0, h_scr, d == 1, levels)
    o_ref[0] = (jax.nn.gelu(lg_ref[0]) * h_scr[...]).astype(o_ref.dtype)


def _lru_call(lx, lg, lxc, conv_w, conv_b, w_r, b_r, w_i, b_i, lam):
    b, t, d_lru = lx.shape
    tc = lxc.shape[1]
    nb = d_lru // LANES
    assert w_r.shape == (2, nb, LANES, LANES)
    seq = lambda n: pl.BlockSpec((1, n, LANES), lambda bi, j: (bi, 0, j))
    vec = lambda n: pl.BlockSpec((n, LANES), lambda bi, j: (0, j))
    wblk = pl.BlockSpec((2, 1, LANES, LANES), lambda bi, j: (0, j, 0, 0))
    level_scr = []
    n = t
    while n > LRU_SEQ_SCAN_MAX:
        assert n % (SUBLANES * SUBLANES) == 0
        n //= SUBLANES
        level_scr += [pltpu.VMEM((SUBLANES, n, LANES), F32), pltpu.VMEM((SUBLANES, n, LANES), F32),
                      pltpu.VMEM((n, LANES), F32)]
    return pl.pallas_call(
        _lru_body,
        grid=(b, nb),
        in_specs=[seq(t), seq(t), seq(tc), vec(CONV_W), vec(1), wblk, vec(2), wblk, vec(2), vec(2)],
        out_specs=seq(t),
        out_shape=jax.ShapeDtypeStruct((b, t, d_lru), BF16),
        scratch_shapes=[pltpu.VMEM((t + 2 * SUBLANES, LANES), F32),
                        pltpu.VMEM((t, LANES), F32), pltpu.VMEM((tc, LANES), F32),
                        pltpu.VMEM((t, LANES), F32), pltpu.VMEM((t, LANES), F32),
                        pltpu.VMEM((t, LANES), F32)] + level_scr,
        compiler_params=_cparams(("parallel", "parallel")),
        name="rglru",
    )(lx, lg, lxc, conv_w, conv_b, w_r, b_r, w_i, b_i, lam)


FWD_GATE_LANES = 2 * M_HEADS


def _split3_dot(tri, x):
    hi = x.astype(BF16)
    r1 = x - hi.astype(F32)
    mid = r1.astype(BF16)
    lo = (r1 - mid.astype(F32)).astype(BF16)
    return (jnp.dot(tri, hi, preferred_element_type=F32) + jnp.dot(tri, mid, preferred_element_type=F32)
            + jnp.dot(tri, lo, preferred_element_type=F32))


def _cummax_rows(x, row, reverse):
    n = x.shape[0]
    k = 1
    while k < n:
        if reverse:
            x = jnp.where(row < n - k, jnp.maximum(x, pltpu.roll(x, n - k, 0)), x)
        else:
            x = jnp.where(row >= k, jnp.maximum(x, pltpu.roll(x, k, 0)), x)
        k *= 2
    return x


def _round_up_bf16(x):
    return (x + jnp.abs(x) * (2.0 ** -7)).astype(BF16)


def _mlstm_gate_prep(gates, kidx, bc_scr, cm_scr, xt_scr, tot_scr, gmax_scr):
    L = gates.shape[0]
    ti = lax.broadcasted_iota(jnp.int32, (L, L), 0)
    si = lax.broadcasted_iota(jnp.int32, (L, L), 1)
    row = lax.broadcasted_iota(jnp.int32, (L, LANES), 0)
    fwd = lax.broadcasted_iota(jnp.int32, (L, LANES), 1) < FWD_GATE_LANES
    lf = jax.nn.log_sigmoid(gates)
    bc_f = _split3_dot((si <= ti).astype(BF16), lf)
    bc_b = _split3_dot((si >= ti).astype(BF16), lf)
    bc = jnp.where(fwd, bc_f, bc_b)
    tot = jnp.where(fwd[0:1], bc_f[L - 1:L], bc_b[0:1])
    x = pltpu.roll(gates, SUBLANES, 1) - bc
    cm = jnp.where(fwd, _cummax_rows(x, row, False), _cummax_rows(x, row, True))
    gmax = jnp.max(tot + x, axis=0, keepdims=True)
    xt = x.T
    bc_scr[kidx] = bc
    cm_scr[kidx] = _round_up_bf16(cm)
    xt_scr[kidx, 0:SUBLANES, :] = xt[SUBLANES:2 * SUBLANES]
    xt_scr[kidx, SUBLANES:2 * SUBLANES, :] = xt[3 * SUBLANES:4 * SUBLANES]
    tot_scr[kidx] = jnp.broadcast_to(tot, (SUBLANES, LANES))
    gmax_scr[kidx] = jnp.broadcast_to(gmax, (SUBLANES, LANES))


def _mlstm_chunk(q, v, kidx, step, d, scr, need_h, dk):
    bc_scr, cm_scr, xt_scr, kt_scr, tot_scr, m_in_scr, m_out_scr, s_scr = scr
    L = q.shape[0]
    ti = lax.broadcasted_iota(jnp.int32, (L, L), 0)
    si = lax.broadcasted_iota(jnp.int32, (L, L), 1)
    causal = (si <= ti) if d == 0 else (si >= ti)
    lane = lax.broadcasted_iota(jnp.int32, (L, LANES), 1)
    head0_rows = lax.broadcasted_iota(jnp.int32, (LANES, L), 0) < dk
    srow = lax.broadcasted_iota(jnp.int32, (LANES, 2 * LANES), 0)
    k_t = kt_scr[kidx]
    tot_v, m_in_v, m_out_v = tot_scr[kidx], m_in_scr[step], m_out_scr[step]
    ones = jnp.ones((L, LANES), BF16)
    v1 = jnp.concatenate([v[:, :LANES], ones, v[:, LANES:], ones], axis=1)
    cf0 = FWD_GATE_LANES * d + SUBLANES
    pick = lambda vals, e: vals[0:1, cf0 + e:cf0 + e + 1]
    x_rows = [xt_scr[kidx, SUBLANES * d + e:SUBLANES * d + e + 1, :] for e in (0, 1)]
    s_old = s_scr[d]
    hs = []
    if need_h:
        bc = bc_scr[kidx]
        sel_lane = lax.broadcasted_iota(jnp.int32, (LANES, 2 * LANES), 0)
        sel_col = lax.broadcasted_iota(jnp.int32, (LANES, 2 * LANES), 1)
        sel = (sel_lane == jnp.where(sel_col < LANES, cf0, cf0 + 1)).astype(BF16)
        mx_all = jnp.maximum(_round_up_bf16(m_in_v[0:1, :]), cm_scr[kidx])
        mx_tiles = jnp.dot(mx_all, sel, preferred_element_type=F32)
        q_heads = [jnp.where((lane >= e * dk) & (lane < (e + 1) * dk), q, jnp.zeros_like(q)) for e in (0, 1)]
        q2 = jnp.concatenate(q_heads, axis=0)
        qk = jnp.dot(q2, k_t, preferred_element_type=F32)
        qs = jnp.dot(q2, s_old.astype(BF16), preferred_element_type=F32)
        for e in (0, 1):
            cf = cf0 + e
            mx = mx_tiles[:, e * LANES:(e + 1) * LANES]
            w = qk[e * L:(e + 1) * L] * jnp.exp(jnp.where(causal, x_rows[e] - mx, -jnp.inf))
            s_inter = jnp.exp(pick(m_in_v, e) - mx)
            wv = jnp.dot(w.astype(BF16), v1[:, 2 * e * LANES:2 * (e + 1) * LANES],
                         preferred_element_type=F32)
            qs_e = qs[e * L:(e + 1) * L]
            num = s_inter * qs_e[:, :LANES] + wv[:, :LANES]
            den = s_inter * qs_e[:, LANES:] + wv[:, LANES:]
            m_row = bc[:, cf:cf + 1] + mx[:, cf:cf + 1]
            hs.append(num * (1.0 / jnp.maximum(jnp.abs(den[:, cf:cf + 1]), jnp.exp(-m_row))))
    wg_rows = [jnp.exp(pick(tot_v, e) + x_rows[e] - pick(m_out_v, e)) for e in (0, 1)]
    decays = [jnp.exp(pick(tot_v, e) + pick(m_in_v, e) - pick(m_out_v, e)) for e in (0, 1)]
    kw = (k_t.astype(F32) * jnp.where(head0_rows, wg_rows[0], wg_rows[1])).astype(BF16)
    own = jnp.concatenate([jnp.dot(kw[:dk], v1[:, :2 * LANES], preferred_element_type=F32),
                           jnp.dot(kw[dk:], v1[:, 2 * LANES:], preferred_element_type=F32)], axis=0)
    s_scr[d] = jnp.where(srow < dk, decays[0], decays[1]) * s_old + own
    return hs


def _mlstm_body(q_ref, k_ref, v_ref, o_ref, gt_ref, qc_ref, kc_ref, vc_ref, gtc_ref, bias_ref, gain_ref,
                out_ref, hs_scr, bc_scr, cm_scr, xt_scr, kt_scr, tot_scr, gmax_scr, m_in_scr, m_out_scr,
                s_scr, *, dk):
    t = q_ref.shape[1]
    tc = qc_ref.shape[1]
    L = CHUNK
    nc, ncc = t // L, tc // L
    n_steps = nc + ncc
    shift = (LANES - 2 * pl.program_id(1)) % LANES
    bias = pltpu.roll(jnp.broadcast_to(bias_ref[...], (SUBLANES, LANES)), shift, 1)[0:1, :]
    prep_scr = (bc_scr, cm_scr, xt_scr, tot_scr, gmax_scr)
    state_scr = (bc_scr, cm_scr, xt_scr, kt_scr, tot_scr, m_in_scr, m_out_scr, s_scr)
    scaled_t = lambda k: (k.astype(F32).T * (dk ** -0.5)).astype(BF16)

    for c in range(ncc):
        rows = slice(c * L, (c + 1) * L)
        _mlstm_gate_prep(pltpu.roll(gtc_ref[0, rows, :], shift, 1) + bias, c, *prep_scr)
        kt_scr[c] = scaled_t(kc_ref[0, rows, :])

    def prep(c, carry):
        rows = pl.ds(pl.multiple_of(c * L, L), L)
        _mlstm_gate_prep(pltpu.roll(gt_ref[0, rows, :], shift, 1) + bias, c + ncc, *prep_scr)
        kt_scr[c + ncc] = scaled_t(k_ref[0, rows, :])
        return carry

    lax.fori_loop(0, nc, prep, 0, unroll=8)

    fwd_id = lambda i: i
    bwd_id = lambda i: (ncc - 1 - i) if i < ncc else (n_steps - 1 - (i - ncc))
    fwd8 = lax.broadcasted_iota(jnp.int32, (SUBLANES, LANES), 1) < FWD_GATE_LANES
    m = jnp.zeros((SUBLANES, LANES), F32)
    for i in range(n_steps):
        tot = jnp.where(fwd8, tot_scr[fwd_id(i)], tot_scr[bwd_id(i)])
        gmax = jnp.where(fwd8, gmax_scr[fwd_id(i)], gmax_scr[bwd_id(i)])
        m_in_scr[i] = m
        m = jnp.maximum(tot + m, gmax)
        m_out_scr[i] = m

    s_scr[...] = jnp.zeros_like(s_scr)
    hs_scr[...] = jnp.zeros_like(hs_scr)
    for i in range(ncc):
        for d, cid in ((0, fwd_id(i)), (1, bwd_id(i))):
            rows = slice(cid * L, (cid + 1) * L)
            _mlstm_chunk(qc_ref[0, rows, :], vc_ref[0, rows, :], cid, i, d, state_scr, False, dk)

    def body(j, carry):
        for d in (0, 1):
            cj = j if d == 0 else nc - 1 - j
            rows = pl.ds(pl.multiple_of(cj * L, L), L)
            hs = _mlstm_chunk(q_ref[0, rows, :], v_ref[0, rows, :], cj + ncc, j + ncc, d, state_scr, True, dk)
            hs_scr[rows, :] += jnp.concatenate(hs, axis=1)
        return carry

    lax.fori_loop(0, nc, body, 0, unroll=8)

    def fin(c, carry):
        rows = pl.ds(pl.multiple_of(c * L, L), L)
        hh = hs_scr[rows, :]
        outs = []
        for e in (0, 1):
            x = hh[:, e * LANES:(e + 1) * LANES]
            outs.append(x * lax.rsqrt(jnp.mean(x * x, axis=-1, keepdims=True) + EPS))
        y = jnp.concatenate(outs, axis=1) * gain_ref[...] * _sigmoid(o_ref[0, rows, :])
        out_ref[0, rows, :] = y.astype(out_ref.dtype)
        return carry

    lax.fori_loop(0, nc, fin, 0, unroll=4)


def _mlstm_call(q, k, v, o, gt, qc, kc, vc, gtc, bias, gain):
    b, t, d_qk = q.shape
    tc = qc.shape[1]
    d_v = v.shape[2]
    pairs = M_HEADS // 2
    dk = d_qk // M_HEADS
    assert d_qk // pairs == LANES and d_v // pairs == 2 * LANES and CHUNK == LANES
    n_chunks = (t + tc) // CHUNK
    seq = lambda n, w: pl.BlockSpec((1, n, w), lambda bi, p: (bi, 0, p))
    allg = lambda n: pl.BlockSpec((1, n, LANES), lambda bi, p: (bi, 0, 0))
    chunk_f32 = lambda rows, cols: pltpu.VMEM((n_chunks, rows, cols), F32)
    return pl.pallas_call(
        functools.partial(_mlstm_body, dk=dk),
        grid=(b, pairs),
        in_specs=[seq(t, LANES), seq(t, LANES), seq(t, 2 * LANES), seq(t, 2 * LANES), allg(t),
                  seq(tc, LANES), seq(tc, LANES), seq(tc, 2 * LANES), allg(tc),
                  pl.BlockSpec((1, LANES), lambda bi, p: (0, 0)),
                  pl.BlockSpec((1, 2 * LANES), lambda bi, p: (0, p))],
        out_specs=seq(t, 2 * LANES),
        out_shape=jax.ShapeDtypeStruct((b, t, d_v), F32),
        scratch_shapes=[pltpu.VMEM((t, 2 * LANES), F32),
                        chunk_f32(CHUNK, LANES), pltpu.VMEM((n_chunks, CHUNK, LANES), BF16),
                        chunk_f32(2 * SUBLANES, CHUNK),
                        pltpu.VMEM((n_chunks, LANES, CHUNK), BF16),
                        chunk_f32(SUBLANES, LANES), chunk_f32(SUBLANES, LANES),
                        chunk_f32(SUBLANES, LANES), chunk_f32(SUBLANES, LANES),
                        pltpu.VMEM((2, LANES, 2 * LANES), F32)],
        compiler_params=_cparams(("parallel", "parallel")),
        name="mlstm",
    )(q, k, v, o, gt, qc, kc, vc, gtc, bias, gain)


def _outproj_body(x_ref, lru_ref, mls_ref, wa_ref, wb_ref, g_ref, o_ref):
    w_cols, r, dm = mls_ref.shape[1:]
    m = mls_ref[0].reshape(w_cols * r, dm).astype(BF16)
    m = jnp.dot(_colmajor_perm(w_cols, r), m, preferred_element_type=F32).astype(BF16)
    y = (jnp.dot(lru_ref[0], wa_ref[...], preferred_element_type=F32)
         + jnp.dot(m, wb_ref[...], preferred_element_type=F32))
    o_ref[0] = x_ref[0] + g_ref[0] * y


def _outproj_call(x3d, lru, mls_cm, w_a, w_b, gate):
    b, t, d = x3d.shape
    da, dm = lru.shape[2], mls_cm.shape[2]
    rows = t // GRID_W
    r_tile = PROJ_TM // GRID_W
    assert r_tile == SUBLANES
    mls_view = mls_cm.reshape(b, GRID_W, rows, dm)
    return pl.pallas_call(
        _outproj_body,
        grid=(b, t // PROJ_TM),
        in_specs=[pl.BlockSpec((1, PROJ_TM, d), lambda bi, i: (bi, i, 0)),
                  pl.BlockSpec((1, PROJ_TM, da), lambda bi, i: (bi, i, 0)),
                  pl.BlockSpec((1, GRID_W, r_tile, dm), lambda bi, i: (bi, 0, i, 0)),
                  pl.BlockSpec((da, d), lambda bi, i: (0, 0)),
                  pl.BlockSpec((dm, d), lambda bi, i: (0, 0)),
                  pl.BlockSpec((1, 1, d), lambda bi, i: (bi, 0, 0))],
        out_specs=pl.BlockSpec((1, PROJ_TM, d), lambda bi, i: (bi, i, 0)),
        out_shape=jax.ShapeDtypeStruct((b, t, d), F32),
        compiler_params=_cparams(("parallel", "parallel")),
        name="outproj",
    )(x3d, lru, mls_view, w_a, w_b, gate)


def kernel(x, c, ctx, c_ctx, ada_w, ada_b, ffn1_norm, ffn1_w_up, ffn1_w_down, mix_norm, w_in, b_mgate, lru_conv_w, lru_conv_b, lru_w_r, lru_b_r, lru_w_i, lru_b_i, lru_lam, mlstm_norm, w_out, ffn2_norm, ffn2_w_up, ffn2_w_down, final_norm):
    b, t, d = x.shape
    tc = ctx.shape[1]
    assert ada_w.shape[0] == 1, "single-layer block only"
    assert t % GRID_W == 0 and t % PROJ_TM == 0
    rows = t // GRID_W
    d_lru = lru_conv_w.shape[2]
    d_mv = mlstm_norm.shape[1]
    d_mqk = (w_in.shape[2] - 2 * d_lru - 2 * d_mv - 4 * M_HEADS) // 2

    pad = SUBLANES - b - 1
    cc = jnp.concatenate([c, c_ctx[None, :], jnp.zeros((pad, d), F32)], axis=0)
    mod = _ada_call(cc, ada_w[0], ada_b[0][None, :]).reshape(SUBLANES, N_MOD, d)
    lat = lambda i: mod[:b, i][:, None, :]
    cxt = lambda i: mod[b:b + 1, i][:, None, :]
    row = lambda v: v[0][None, :]

    x1, ctx1 = _ffn_call(x.reshape(b * t, d), (lat(0), lat(1), lat(2)), row(ffn1_norm), ffn1_w_up[0],
                         ffn1_w_down[0], row(ffn1_norm), rows_per_mod=t, final_norm=False, name="ffn_pre",
                         extra=(ctx.reshape(b * tc, d), (cxt(0), cxt(1), cxt(2))))
    x1 = x1.reshape(b, t, d)
    ctx1 = ctx1.reshape(b, tc, d)

    n_gate = 4 * M_HEADS
    w_a, w_b = _split_w_in(w_in[0].T, 2 * d_lru)
    widths_a, dtypes_a = (d_lru, d_lru), (F32, F32)
    widths_b, dtypes_b = (d_mqk, d_mqk, d_mv, d_mv, LANES), (BF16, BF16, BF16, F32, F32)
    mixn = row(mix_norm)
    lx, lg = _proj_call(x1, mixn, lat(3), lat(4), w_a, widths_a, dtypes_a, tm=PROJ_TM, colmajor=False,
                        name="proj_lru")
    q, k, v, o, gt = _proj_call(x1.reshape(b, rows, GRID_W, d), mixn, lat(3), lat(4), w_b, widths_b,
                                dtypes_b, tm=PROJ_TM, colmajor=True, name="proj_mlstm")
    ctx_mod = lambda i: jnp.broadcast_to(cxt(i), (b, 1, d))
    lxc, _ = _proj_call(ctx1, mixn, ctx_mod(3), ctx_mod(4), w_a, widths_a, dtypes_a, tm=tc, colmajor=False,
                        name="proj_lru_ctx")
    qc, kc, vc, _, gtc = _proj_call(ctx1, mixn, ctx_mod(3), ctx_mod(4), w_b, widths_b, dtypes_b, tm=tc,
                                    colmajor=False, name="proj_mlstm_ctx")

    lru_lat = _lru_call(lx, lg, lxc, lru_conv_w[0], row(lru_conv_b), lru_w_r[0], lru_b_r[0], lru_w_i[0],
                        lru_b_i[0], lru_lam[0])
    bias = jnp.concatenate([b_mgate[0], jnp.zeros((LANES - n_gate,), F32)])[None, :]
    mls_cm = _mlstm_call(q, k, v, o, gt, qc, kc, vc, gtc, bias, row(mlstm_norm))

    w_out0 = w_out[0].astype(BF16)
    x2 = _outproj_call(x1, lru_lat, mls_cm, w_out0[:d_lru], w_out0[d_lru:], lat(5))

    out, = _ffn_call(x2.reshape(b * t, d), (lat(6), lat(7), lat(8)), row(ffn2_norm), ffn2_w_up[0],
                     ffn2_w_down[0], final_norm[None, :], rows_per_mod=t, final_norm=True, name="ffn_post")
    return out.reshape(b, t, d)
```

```python
import functools

import jax
import jax.numpy as jnp
from jax import lax
from jax.experimental import pallas as pl
from jax.experimental.pallas import tpu as pltpu

F32 = jnp.float32
BF16 = jnp.bfloat16

GRID_W = 64
LRU_BLOCKS = 8
CONV_W = 4
LRU_C = 8.0
M_HEADS = 8
CHUNK = 128
N_MOD = 9
EPS = 1e-6
HALF = 0.5

LANES = 128
SUBLANES = 8
VMEM_LIMIT_BYTES = 58 * 1024 * 1024

FFN_TM = 1024
FFN_TF = 256
FFN_EDGE_ROWS = 512
PROJ_ROWS = 256
PROJ_TM = 512
LRU_GATE_ROWS = 512
LRU_SEQ_SCAN_MAX = 64
ADA_TN = 1024
WSPLIT_TN = 512


def _cparams(sem):
    return pltpu.CompilerParams(dimension_semantics=sem, vmem_limit_bytes=VMEM_LIMIT_BYTES)


def _sigmoid(z):
    return 0.5 * jnp.tanh(0.5 * z) + 0.5


def _rms(x, g):
    return x * lax.rsqrt(jnp.mean(x * x, axis=-1, keepdims=True) + EPS) * g


def _ada_body(c_ref, w_ref, b_ref, o_ref):
    s = c_ref[...]
    s = s * jax.nn.sigmoid(s)
    o_ref[...] = jnp.dot(s, w_ref[...], preferred_element_type=F32) + b_ref[...]


def _ada_call(cc, w, b):
    rows, d = cc.shape
    n = w.shape[1]
    return pl.pallas_call(
        _ada_body,
        grid=(n // ADA_TN,),
        in_specs=[pl.BlockSpec((rows, d), lambda j: (0, 0)),
                  pl.BlockSpec((d, ADA_TN), lambda j: (0, j)),
                  pl.BlockSpec((1, ADA_TN), lambda j: (0, j))],
        out_specs=pl.BlockSpec((rows, ADA_TN), lambda j: (0, j)),
        out_shape=jax.ShapeDtypeStruct((rows, n), F32),
        compiler_params=_cparams(("arbitrary",)),
        name="ada_mod",
    )(cc, w, b)


def _ffn_body(*refs, final_norm, n_groups):
    groups_in = [refs[4 * k:4 * k + 4] for k in range(n_groups)]
    g_ref, wg_ref, wu_ref, wd_ref, fin_ref = refs[4 * n_groups:4 * n_groups + 5]
    out_refs = refs[4 * n_groups + 5:5 * n_groups + 5]
    h_scr = refs[-1]
    j = pl.program_id(1)
    last = pl.num_programs(1) - 1
    groups, off = [], 0
    for (x_ref, sh_ref, sc_ref, gate_ref), o_ref in zip(groups_in, out_refs):
        n = x_ref.shape[0]
        blocks = lambda rb: [slice(r0, r0 + min(rb, n)) for r0 in range(0, n, min(rb, n))]
        groups.append((x_ref, sh_ref, sc_ref, gate_ref, o_ref, off, blocks(FFN_EDGE_ROWS),
                       blocks(FFN_EDGE_ROWS // 2 if final_norm else FFN_EDGE_ROWS)))
        off += n

    def contrib(h):
        g = jnp.dot(h, wg_ref[...].astype(BF16), preferred_element_type=F32)
        u = jnp.dot(h, wu_ref[...].astype(BF16), preferred_element_type=F32)
        a = (g * jax.nn.sigmoid(g) * u).astype(BF16)
        return jnp.dot(a, wd_ref[...].astype(BF16), preferred_element_type=F32)

    @pl.when(j == 0)
    def _():
        for x_ref, sh_ref, sc_ref, _, o_ref, off, blocks, _ in groups:
            for rows in blocks:
                h = (_rms(x_ref[rows, :], g_ref[...]) * (1.0 + sc_ref[0]) + sh_ref[0]).astype(BF16)
                h_scr[off + rows.start:off + rows.stop, :] = h
                o_ref[rows, :] = contrib(h)

    @pl.when((j > 0) & (j < last))
    def _():
        res = contrib(h_scr[...])
        for _, _, _, _, o_ref, off, _, _ in groups:
            o_ref[...] += res[off:off + o_ref.shape[0]]

    @pl.when(j == last)
    def _():
        for x_ref, _, _, gate_ref, o_ref, off, _, blocks in groups:
            for rows in blocks:
                acc = o_ref[rows, :] + contrib(h_scr[off + rows.start:off + rows.stop, :])
                y = x_ref[rows, :] + HALF * gate_ref[0] * acc
                if final_norm:
                    y = _rms(y, fin_ref[...])
                o_ref[rows, :] = y


def _ffn_call(x2d, mods, norm_g, w_up, w_down, fin_g, *, rows_per_mod, final_norm, name, extra=None):
    m, d = x2d.shape
    f = w_down.shape[0]
    tm = min(FFN_TM, m)
    n_tiles = m // tm
    tiles_per_mod = rows_per_mod // tm
    nf = f // FFN_TF
    vec_spec = pl.BlockSpec((1, d), lambda i, j: (0, 0))
    row_spec = lambda rows: pl.BlockSpec((rows, d), lambda i, j: (i, 0))
    mod_spec = pl.BlockSpec((1, 1, d), lambda i, j: (i // tiles_per_mod, 0, 0))
    const_mod_spec = pl.BlockSpec((1, 1, d), lambda i, j: (0, 0, 0))
    in_specs = [row_spec(tm), mod_spec, mod_spec, mod_spec]
    args = [x2d, *mods]
    out_rows = [tm]
    if extra is not None:
        x_e, mods_e = extra
        te = x_e.shape[0] // n_tiles
        assert te * n_tiles == x_e.shape[0] and te % (2 * SUBLANES) == 0
        in_specs += [row_spec(te), const_mod_spec, const_mod_spec, const_mod_spec]
        args += [x_e, *mods_e]
        out_rows.append(te)
    in_specs += [vec_spec,
                 pl.BlockSpec((d, FFN_TF), lambda i, j: (0, j)),
                 pl.BlockSpec((d, FFN_TF), lambda i, j: (0, j + nf)),
                 pl.BlockSpec((FFN_TF, d), lambda i, j: (j, 0)),
                 vec_spec]
    args += [norm_g, w_up, w_up, w_down, fin_g]
    return pl.pallas_call(
        functools.partial(_ffn_body, final_norm=final_norm, n_groups=len(out_rows)),
        grid=(n_tiles, nf),
        in_specs=in_specs,
        out_specs=[row_spec(r) for r in out_rows],
        out_shape=[jax.ShapeDtypeStruct((r * n_tiles, d), F32) for r in out_rows],
        scratch_shapes=[pltpu.VMEM((sum(out_rows), d), BF16)],
        compiler_params=_cparams(("parallel", "arbitrary")),
        name=name,
    )(*args)


def _wsplit_body(wt_ref, a_ref, b_ref, *, n_valid, n_a_blocks):
    j = pl.program_id(0)
    feat = j * WSPLIT_TN + lax.broadcasted_iota(jnp.int32, wt_ref.shape, 0)
    w = jnp.where(feat < n_valid, wt_ref[...], 0.0).T.astype(BF16)

    @pl.when(j < n_a_blocks)
    def _():
        a_ref[...] = w

    @pl.when(j >= n_a_blocks)
    def _():
        b_ref[...] = w


def _split_w_in(wt, n_a):
    n, d = wt.shape
    assert n_a % WSPLIT_TN == 0
    nb = pl.cdiv(n, WSPLIT_TN)
    na = n_a // WSPLIT_TN
    return pl.pallas_call(
        functools.partial(_wsplit_body, n_valid=n, n_a_blocks=na),
        grid=(nb,),
        in_specs=[pl.BlockSpec((WSPLIT_TN, d), lambda j: (j, 0))],
        out_specs=[pl.BlockSpec((d, WSPLIT_TN), lambda j: (0, jnp.minimum(j, na - 1))),
                   pl.BlockSpec((d, WSPLIT_TN), lambda j: (0, jnp.maximum(j - na, 0)))],
        out_shape=[jax.ShapeDtypeStruct((d, n_a), BF16),
                   jax.ShapeDtypeStruct((d, (nb - na) * WSPLIT_TN), BF16)],
        compiler_params=_cparams(("arbitrary",)),
        name="w_in_split",
    )(wt)


def _proj_body(x_ref, g_ref, sh_ref, sc_ref, w_ref, *out_refs, widths, colmajor):
    tm = out_refs[0].shape[1]
    rb = min(PROJ_ROWS, tm)
    norm = lambda x: (_rms(x, g_ref[...]) * (1.0 + sc_ref[0]) + sh_ref[0]).astype(BF16)
    if colmajor:
        r, c, d = x_ref.shape[1:]
        assert rb % r == 0
        xs = jnp.swapaxes(x_ref[0], 0, 1)
    for r0 in range(0, tm, rb):
        rows = slice(r0, r0 + rb)
        if colmajor:
            h = norm(xs[r0 // r:(r0 + rb) // r].reshape(rb, d))
        else:
            h = norm(x_ref[0, rows, :])
        off = 0
        for o_ref, wdt in zip(out_refs, widths):
            o_ref[0, rows, :] = jnp.dot(h, w_ref[:, off:off + wdt],
                                        preferred_element_type=F32).astype(o_ref.dtype)
            off += wdt


def _proj_call(x, norm_g, shift, scale, w, widths, dtypes, *, tm, colmajor, name):
    b = x.shape[0]
    d = norm_g.shape[1]
    if colmajor:
        rows, gw = x.shape[1:3]
        cols = tm // rows
        assert cols == SUBLANES and gw % cols == 0
        n_tiles = gw // cols
        x_spec = pl.BlockSpec((1, rows, cols, d), lambda bi, i: (bi, 0, i, 0))
    else:
        n_tiles = x.shape[1] // tm
        x_spec = pl.BlockSpec((1, tm, d), lambda bi, i: (bi, i, 0))
    mod_spec = pl.BlockSpec((1, 1, d), lambda bi, i: (bi, 0, 0))
    return pl.pallas_call(
        functools.partial(_proj_body, widths=tuple(widths), colmajor=colmajor),
        grid=(b, n_tiles),
        in_specs=[x_spec,
                  pl.BlockSpec((1, d), lambda bi, i: (0, 0)),
                  mod_spec, mod_spec,
                  pl.BlockSpec(w.shape, lambda bi, i: (0, 0))],
        out_specs=[pl.BlockSpec((1, tm, wdt), lambda bi, i: (bi, i, 0)) for wdt in widths],
        out_shape=[jax.ShapeDtypeStruct((b, n_tiles * tm, wdt), dt) for wdt, dt in zip(widths, dtypes)],
        compiler_params=_cparams(("parallel", "parallel")),
        name=name,
    )(x, norm_g, shift, scale, w)


def _lru_conv(x_ref, cw_ref, cb_ref, pad_scr, xc_scr, t):
    zeros = jnp.zeros((SUBLANES, LANES), F32)
    pad_scr[0:SUBLANES, :] = zeros
    pad_scr[SUBLANES:SUBLANES + t, :] = x_ref[0]
    pad_scr[SUBLANES + t:2 * SUBLANES + t, :] = zeros
    ch = min(LRU_GATE_ROWS, t)
    win_rows = ch + 2 * SUBLANES

    def body(c, carry):
        off = pl.multiple_of(c * ch, SUBLANES)
        win = pad_scr[pl.ds(off, win_rows), :]
        acc = cb_ref[...] + cw_ref[2:3, :] * win[SUBLANES:SUBLANES + ch]
        for k, shift in ((0, 2), (1, 1), (3, win_rows - 1)):
            acc = acc + cw_ref[k:k + 1, :] * pltpu.roll(win, shift, 0)[SUBLANES:SUBLANES + ch]
        xc_scr[pl.ds(off, ch), :] = acc
        return carry

    lax.fori_loop(0, t // ch, body, 0)


def _lru_gates(xc_scr, t, d, wr_ref, br_ref, wi_ref, bi_ref, lam_ref, a_scr, b_scr):
    ch = min(LRU_GATE_ROWS, t)
    half_unit = (-0.5 * LRU_C) * jax.nn.softplus(-lam_ref[d:d + 1, :])
    wr = (0.5 * wr_ref[d, 0]).astype(BF16)
    wi = (0.5 * wi_ref[d, 0]).astype(BF16)
    br = 0.5 * br_ref[d:d + 1, :]
    bi = 0.5 * bi_ref[d:d + 1, :]

    def body(c, carry):
        off = pl.multiple_of(c * ch, SUBLANES)
        xc = xc_scr[pl.ds(off, ch), :]
        xb = xc.astype(BF16)
        tr = jnp.tanh(jnp.dot(xb, wr, preferred_element_type=F32) + br)
        ti = jnp.tanh(jnp.dot(xb, wi, preferred_element_type=F32) + bi)
        log_a = half_unit * tr + half_unit
        a = jnp.exp(log_a)
        a_scr[pl.ds(off, ch), :] = a
        y = -jnp.tanh(log_a) * (a * a + 1.0)
        mult = jnp.where(y > 0.0, y * lax.rsqrt(y), 0.0)
        b_scr[pl.ds(off, ch), :] = (mult * xc) * (0.5 * ti + 0.5)
        return carry

    lax.fori_loop(0, t // ch, body, 0, unroll=min(4, t // ch))


def _lru_scan(a_ref, b_ref, t, d, h0, out_ref, accumulate=False):
    groups = t // SUBLANES
    row = lax.broadcasted_iota(jnp.int32, (SUBLANES, LANES), 0)

    def body(g, carry):
        gi = g if d == 0 else groups - 1 - g
        off = pl.multiple_of(gi * SUBLANES, SUBLANES)
        a = a_ref[pl.ds(off, SUBLANES), :]
        bv = b_ref[pl.ds(off, SUBLANES), :]
        for k in (1, 2, 4):
            shift = k if d == 0 else SUBLANES - k
            valid = (row >= k) if d == 0 else (row < SUBLANES - k)
            a_prev = pltpu.roll(a, shift, 0)
            b_prev = pltpu.roll(bv, shift, 0)
            bv = jnp.where(valid, a * b_prev + bv, bv)
            a = jnp.where(valid, a * a_prev, a)
        h = a * carry + bv
        if out_ref is not None:
            if accumulate:
                out_ref[pl.ds(off, SUBLANES), :] += h
            else:
                out_ref[pl.ds(off, SUBLANES), :] = h
        return h[SUBLANES - 1:SUBLANES, :] if d == 0 else h[0:1, :]

    return lax.fori_loop(0, groups, body, h0, unroll=min(4, groups))


def _lru_scan_planes(a_ref, b_ref, n, d, h0, out_ref, accumulate, levels):
    if n <= LRU_SEQ_SCAN_MAX:
        _lru_scan(a_ref, b_ref, n, d, h0, out_ref, accumulate)
        return
    g = n // SUBLANES
    p_scr, q_scr, s_scr = levels[0]
    order = list(range(SUBLANES)) if d == 0 else list(range(SUBLANES - 1, -1, -1))
    group_rows = SUBLANES * SUBLANES

    def up(v, carry):
        base = pl.multiple_of(v * group_rows, group_rows)
        rows = pl.ds(pl.multiple_of(v * SUBLANES, SUBLANES), SUBLANES)
        p = h = None
        for j in order:
            a = a_ref[pl.ds(base + j, SUBLANES, stride=SUBLANES), :]
            bv = b_ref[pl.ds(base + j, SUBLANES, stride=SUBLANES), :]
            if p is None:
                p, h = a, bv
            else:
                h = a * h + bv
                p = a * p
            p_scr[j, rows, :] = p
            q_scr[j, rows, :] = h
        return carry

    lax.fori_loop(0, g // SUBLANES, up, 0, unroll=4)

    last = order[-1]
    _lru_scan_planes(p_scr.at[last], q_scr.at[last], g, d, h0, s_scr, False, levels[1:])
    s = s_scr[...]
    row = lax.broadcasted_iota(jnp.int32, (g, LANES), 0)
    if d == 0:
        s_scr[...] = jnp.where(row == 0, h0, pltpu.roll(s, 1, 0))
    else:
        s_scr[...] = jnp.where(row == g - 1, h0, pltpu.roll(s, g - 1, 0))

    def down(v, carry):
        base = pl.multiple_of(v * group_rows, group_rows)
        rows = pl.ds(pl.multiple_of(v * SUBLANES, SUBLANES), SUBLANES)
        x = s_scr[rows, :]
        for j in range(SUBLANES):
            dst = pl.ds(base + j, SUBLANES, stride=SUBLANES)
            val = q_scr[j, rows, :] + p_scr[j, rows, :] * x
            if accumulate:
                val = val + out_ref[dst, :]
            out_ref[dst, :] = val
        return carry

    lax.fori_loop(0, g // SUBLANES, down, 0, unroll=4)


def _lru_body(lx_ref, lg_ref, lxc_ref, cw_ref, cb_ref, wr_ref, br_ref, wi_ref, bi_ref, lam_ref, o_ref,
              pad_scr, xc_scr, xcc_scr, a_scr, b_scr, h_scr, *level_scr):
    levels = [level_scr[i:i + 3] for i in range(0, len(level_scr), 3)]
    t = lx_ref.shape[1]
    tc = lxc_ref.shape[1]
    _lru_conv(lxc_ref, cw_ref, cb_ref, pad_scr, xcc_scr, tc)
    _lru_conv(lx_ref, cw_ref, cb_ref, pad_scr, xc_scr, t)
    gate_refs = (wr_ref, br_ref, wi_ref, bi_ref, lam_ref)
    for d in (0, 1):
        _lru_gates(xcc_scr, tc, d, *gate_refs, a_scr, b_scr)
        h0 = _lru_scan(a_scr, b_scr, tc, d, jnp.zeros((1, LANES), F32), None)
        _lru_gates(xc_scr, t, d, *gate_refs, a_scr, b_scr)
        _lru_scan_planes(a_scr, b_scr, t, d, h0, h_scr, d == 1, levels)
    o_ref[0] = (jax.nn.gelu(lg_ref[0]) * h_scr[...]).astype(o_ref.dtype)


def _lru_call(lx, lg, lxc, conv_w, conv_b, w_r, b_r, w_i, b_i, lam):
    b, t, d_lru = lx.shape
    tc = lxc.shape[1]
    nb = d_lru // LANES
    assert w_r.shape == (2, nb, LANES, LANES)
    seq = lambda n: pl.BlockSpec((1, n, LANES), lambda bi, j: (bi, 0, j))
    vec = lambda n: pl.BlockSpec((n, LANES), lambda bi, j: (0, j))
    wblk = pl.BlockSpec((2, 1, LANES, LANES), lambda bi, j: (0, j, 0, 0))
    level_scr = []
    n = t
    while n > LRU_SEQ_SCAN_MAX:
        assert n % (SUBLANES * SUBLANES) == 0
        n //= SUBLANES
        level_scr += [pltpu.VMEM((SUBLANES, n, LANES), F32), pltpu.VMEM((SUBLANES, n, LANES), F32),
                      pltpu.VMEM((n, LANES), F32)]
    return pl.pallas_call(
        _lru_body,
        grid=(b, nb),
        in_specs=[seq(t), seq(t), seq(tc), vec(CONV_W), vec(1), wblk, vec(2), wblk, vec(2), vec(2)],
        out_specs=seq(t),
        out_shape=jax.ShapeDtypeStruct((b, t, d_lru), BF16),
        scratch_shapes=[pltpu.VMEM((t + 2 * SUBLANES, LANES), F32),
                        pltpu.VMEM((t, LANES), F32), pltpu.VMEM((tc, LANES), F32),
                        pltpu.VMEM((t, LANES), F32), pltpu.VMEM((t, LANES), F32),
                        pltpu.VMEM((t, LANES), F32)] + level_scr,
        compiler_params=_cparams(("parallel", "parallel")),
        name="rglru",
    )(lx, lg, lxc, conv_w, conv_b, w_r, b_r, w_i, b_i, lam)


FWD_GATE_LANES = 2 * M_HEADS


def _split3_dot(tri, x):
    hi = x.astype(BF16)
    r1 = x - hi.astype(F32)
    mid = r1.astype(BF16)
    lo = (r1 - mid.astype(F32)).astype(BF16)
    return (jnp.dot(tri, hi, preferred_element_type=F32) + jnp.dot(tri, mid, preferred_element_type=F32)
            + jnp.dot(tri, lo, preferred_element_type=F32))


def _cummax_rows(x, row, reverse):
    n = x.shape[0]
    k = 1
    while k < n:
        if reverse:
            x = jnp.where(row < n - k, jnp.maximum(x, pltpu.roll(x, n - k, 0)), x)
        else:
            x = jnp.where(row >= k, jnp.maximum(x, pltpu.roll(x, k, 0)), x)
        k *= 2
    return x


def _round_up_bf16(x):
    return (x + jnp.abs(x) * (2.0 ** -7)).astype(BF16)


def _mlstm_gate_prep(gates, kidx, bc_scr, cm_scr, xt_scr, tot_scr, gmax_scr):
    L = gates.shape[0]
    ti = lax.broadcasted_iota(jnp.int32, (L, L), 0)
    si = lax.broadcasted_iota(jnp.int32, (L, L), 1)
    row = lax.broadcasted_iota(jnp.int32, (L, LANES), 0)
    fwd = lax.broadcasted_iota(jnp.int32, (L, LANES), 1) < FWD_GATE_LANES
    lf = jax.nn.log_sigmoid(gates)
    bc_f = _split3_dot((si <= ti).astype(BF16), lf)
    bc_b = _split3_dot((si >= ti).astype(BF16), lf)
    bc = jnp.where(fwd, bc_f, bc_b)
    tot = jnp.where(fwd[0:1], bc_f[L - 1:L], bc_b[0:1])
    x = pltpu.roll(gates, SUBLANES, 1) - bc
    cm = jnp.where(fwd, _cummax_rows(x, row, False), _cummax_rows(x, row, True))
    gmax = jnp.max(tot + x, axis=0, keepdims=True)
    xt = x.T
    bc_scr[kidx] = bc
    cm_scr[kidx] = _round_up_bf16(cm)
    xt_scr[kidx, 0:SUBLANES, :] = xt[SUBLANES:2 * SUBLANES]
    xt_scr[kidx, SUBLANES:2 * SUBLANES, :] = xt[3 * SUBLANES:4 * SUBLANES]
    tot_scr[kidx] = jnp.broadcast_to(tot, (SUBLANES, LANES))
    gmax_scr[kidx] = jnp.broadcast_to(gmax, (SUBLANES, LANES))


def _mlstm_chunk(q, v, kidx, step, d, scr, need_h, dk):
    bc_scr, cm_scr, xt_scr, kt_scr, tot_scr, m_in_scr, m_out_scr, s_scr = scr
    L = q.shape[0]
    ti = lax.broadcasted_iota(jnp.int32, (L, L), 0)
    si = lax.broadcasted_iota(jnp.int32, (L, L), 1)
    causal = (si <= ti) if d == 0 else (si >= ti)
    lane = lax.broadcasted_iota(jnp.int32, (L, LANES), 1)
    head0_rows = lax.broadcasted_iota(jnp.int32, (LANES, L), 0) < dk
    srow = lax.broadcasted_iota(jnp.int32, (LANES, 2 * LANES), 0)
    k_t = kt_scr[kidx]
    tot_v, m_in_v, m_out_v = tot_scr[kidx], m_in_scr[step], m_out_scr[step]
    ones = jnp.ones((L, LANES), BF16)
    v1 = jnp.concatenate([v[:, :LANES], ones, v[:, LANES:], ones], axis=1)
    cf0 = FWD_GATE_LANES * d + SUBLANES
    pick = lambda vals, e: vals[0:1, cf0 + e:cf0 + e + 1]
    x_rows = [xt_scr[kidx, SUBLANES * d + e:SUBLANES * d + e + 1, :] for e in (0, 1)]
    s_old = s_scr[d]
    hs = []
    if need_h:
        bc = bc_scr[kidx]
        sel_lane = lax.broadcasted_iota(jnp.int32, (LANES, 2 * LANES), 0)
        sel_col = lax.broadcasted_iota(jnp.int32, (LANES, 2 * LANES), 1)
        sel = (sel_lane == jnp.where(sel_col < LANES, cf0, cf0 + 1)).astype(BF16)
        mx_all = jnp.maximum(_round_up_bf16(m_in_v[0:1, :]), cm_scr[kidx])
        mx_tiles = jnp.dot(mx_all, sel, preferred_element_type=F32)
        q_heads = [jnp.where((lane >= e * dk) & (lane < (e + 1) * dk), q, jnp.zeros_like(q)) for e in (0, 1)]
        q2 = jnp.concatenate(q_heads, axis=0)
        qk = jnp.dot(q2, k_t, preferred_element_type=F32)
        qs = jnp.dot(q2, s_old.astype(BF16), preferred_element_type=F32)
        for e in (0, 1):
            cf = cf0 + e
            mx = mx_tiles[:, e * LANES:(e + 1) * LANES]
            w = qk[e * L:(e + 1) * L] * jnp.exp(jnp.where(causal, x_rows[e] - mx, -jnp.inf))
            s_inter = jnp.exp(pick(m_in_v, e) - mx)
            wv = jnp.dot(w.astype(BF16), v1[:, 2 * e * LANES:2 * (e + 1) * LANES],
                         preferred_element_type=F32)
            qs_e = qs[e * L:(e + 1) * L]
            num = s_inter * qs_e[:, :LANES] + wv[:, :LANES]
            den = s_inter * qs_e[:, LANES:] + wv[:, LANES:]
            m_row = bc[:, cf:cf + 1] + mx[:, cf:cf + 1]
            hs.append(num * (1.0 / jnp.maximum(jnp.abs(den[:, cf:cf + 1]), jnp.exp(-m_row))))
    wg_rows = [jnp.exp(pick(tot_v, e) + x_rows[e] - pick(m_out_v, e)) for e in (0, 1)]
    decays = [jnp.exp(pick(tot_v, e) + pick(m_in_v, e) - pick(m_out_v, e)) for e in (0, 1)]
    kw = (k_t.astype(F32) * jnp.where(head0_rows, wg_rows[0], wg_rows[1])).astype(BF16)
    own = jnp.concatenate([jnp.dot(kw[:dk], v1[:, :2 * LANES], preferred_element_type=F32),
                           jnp.dot(kw[dk:], v1[:, 2 * LANES:], preferred_element_type=F32)], axis=0)
    s_scr[d] = jnp.where(srow < dk, decays[0], decays[1]) * s_old + own
    return hs


def _mlstm_body(q_ref, k_ref, v_ref, o_ref, gt_ref, qc_ref, kc_ref, vc_ref, gtc_ref, bias_ref, gain_ref,
                out_ref, hs_scr, bc_scr, cm_scr, xt_scr, kt_scr, tot_scr, gmax_scr, m_in_scr, m_out_scr,
                s_scr, *, dk):
    t = q_ref.shape[1]
    tc = qc_ref.shape[1]
    L = CHUNK
    nc, ncc = t // L, tc // L
    n_steps = nc + ncc
    shift = (LANES - 2 * pl.program_id(1)) % LANES
    bias = pltpu.roll(jnp.broadcast_to(bias_ref[...], (SUBLANES, LANES)), shift, 1)[0:1, :]
    prep_scr = (bc_scr, cm_scr, xt_scr, tot_scr, gmax_scr)
    state_scr = (bc_scr, cm_scr, xt_scr, kt_scr, tot_scr, m_in_scr, m_out_scr, s_scr)
    scaled_t = lambda k: (k.astype(F32).T * (dk ** -0.5)).astype(BF16)

    for c in range(ncc):
        rows = slice(c * L, (c + 1) * L)
        _mlstm_gate_prep(pltpu.roll(gtc_ref[0, rows, :], shift, 1) + bias, c, *prep_scr)
        kt_scr[c] = scaled_t(kc_ref[0, rows, :])

    def prep(c, carry):
        rows = pl.ds(pl.multiple_of(c * L, L), L)
        _mlstm_gate_prep(pltpu.roll(gt_ref[0, rows, :], shift, 1) + bias, c + ncc, *prep_scr)
        kt_scr[c + ncc] = scaled_t(k_ref[0, rows, :])
        return carry

    lax.fori_loop(0, nc, prep, 0, unroll=8)

    fwd_id = lambda i: i
    bwd_id = lambda i: (ncc - 1 - i) if i < ncc else (n_steps - 1 - (i - ncc))
    fwd8 = lax.broadcasted_iota(jnp.int32, (SUBLANES, LANES), 1) < FWD_GATE_LANES
    m = jnp.zeros((SUBLANES, LANES), F32)
    for i in range(n_steps):
        tot = jnp.where(fwd8, tot_scr[fwd_id(i)], tot_scr[bwd_id(i)])
        gmax = jnp.where(fwd8, gmax_scr[fwd_id(i)], gmax_scr[bwd_id(i)])
        m_in_scr[i] = m
        m = jnp.maximum(tot + m, gmax)
        m_out_scr[i] = m

    s_scr[...] = jnp.zeros_like(s_scr)
    hs_scr[...] = jnp.zeros_like(hs_scr)
    for i in range(ncc):
        for d, cid in ((0, fwd_id(i)), (1, bwd_id(i))):
            rows = slice(cid * L, (cid + 1) * L)
            _mlstm_chunk(qc_ref[0, rows, :], vc_ref[0, rows, :], cid, i, d, state_scr, False, dk)

    def body(j, carry):
        for d in (0, 1):
            cj = j if d == 0 else nc - 1 - j
            rows = pl.ds(pl.multiple_of(cj * L, L), L)
            hs = _mlstm_chunk(q_ref[0, rows, :], v_ref[0, rows, :], cj + ncc, j + ncc, d, state_scr, True, dk)
            hs_scr[rows, :] += jnp.concatenate(hs, axis=1)
        return carry

    lax.fori_loop(0, nc, body, 0, unroll=8)

    def fin(c, carry):
        rows = pl.ds(pl.multiple_of(c * L, L), L)
        hh = hs_scr[rows, :]
        outs = []
        for e in (0, 1):
            x = hh[:, e * LANES:(e + 1) * LANES]
            outs.append(x * lax.rsqrt(jnp.mean(x * x, axis=-1, keepdims=True) + EPS))
        y = jnp.concatenate(outs, axis=1) * gain_ref[...] * _sigmoid(o_ref[0, rows, :])
        out_ref[0, rows, :] = y.astype(out_ref.dtype)
        return carry

    lax.fori_loop(0, nc, fin, 0, unroll=4)


def _mlstm_call(q, k, v, o, gt, qc, kc, vc, gtc, bias, gain):
    b, t, d_qk = q.shape
    tc = qc.shape[1]
    d_v = v.shape[2]
    pairs = M_HEADS // 2
    dk = d_qk // M_HEADS
    assert d_qk // pairs == LANES and d_v // pairs == 2 * LANES and CHUNK == LANES
    n_chunks = (t + tc) // CHUNK
    seq = lambda n, w: pl.BlockSpec((1, n, w), lambda bi, p: (bi, 0, p))
    allg = lambda n: pl.BlockSpec((1, n, LANES), lambda bi, p: (bi, 0, 0))
    chunk_f32 = lambda rows, cols: pltpu.VMEM((n_chunks, rows, cols), F32)
    return pl.pallas_call(
        functools.partial(_mlstm_body, dk=dk),
        grid=(b, pairs),
        in_specs=[seq(t, LANES), seq(t, LANES), seq(t, 2 * LANES), seq(t, 2 * LANES), allg(t),
                  seq(tc, LANES), seq(tc, LANES), seq(tc, 2 * LANES), allg(tc),
                  pl.BlockSpec((1, LANES), lambda bi, p: (0, 0)),
                  pl.BlockSpec((1, 2 * LANES), lambda bi, p: (0, p))],
        out_specs=seq(t, 2 * LANES),
        out_shape=jax.ShapeDtypeStruct((b, t, d_v), F32),
        scratch_shapes=[pltpu.VMEM((t, 2 * LANES), F32),
                        chunk_f32(CHUNK, LANES), pltpu.VMEM((n_chunks, CHUNK, LANES), BF16),
                        chunk_f32(2 * SUBLANES, CHUNK),
                        pltpu.VMEM((n_chunks, LANES, CHUNK), BF16),
                        chunk_f32(SUBLANES, LANES), chunk_f32(SUBLANES, LANES),
                        chunk_f32(SUBLANES, LANES), chunk_f32(SUBLANES, LANES),
                        pltpu.VMEM((2, LANES, 2 * LANES), F32)],
        compiler_params=_cparams(("parallel", "parallel")),
        name="mlstm",
    )(q, k, v, o, gt, qc, kc, vc, gtc, bias, gain)


def _outproj_body(x_ref, lru_ref, mls_ref, wa_ref, wb_ref, g_ref, o_ref):
    w_cols, r, dm = mls_ref.shape[1:]
    m = jnp.swapaxes(mls_ref[0], 0, 1).reshape(w_cols * r, dm).astype(BF16)
    y = (jnp.dot(lru_ref[0], wa_ref[...], preferred_element_type=F32)
         + jnp.dot(m, wb_ref[...], preferred_element_type=F32))
    o_ref[0] = x_ref[0] + g_ref[0] * y


def _outproj_call(x3d, lru, mls_cm, w_a, w_b, gate):
    b, t, d = x3d.shape
    da, dm = lru.shape[2], mls_cm.shape[2]
    rows = t // GRID_W
    r_tile = PROJ_TM // GRID_W
    assert r_tile == SUBLANES
    mls_view = mls_cm.reshape(b, GRID_W, rows, dm)
    return pl.pallas_call(
        _outproj_body,
        grid=(b, t // PROJ_TM),
        in_specs=[pl.BlockSpec((1, PROJ_TM, d), lambda bi, i: (bi, i, 0)),
                  pl.BlockSpec((1, PROJ_TM, da), lambda bi, i: (bi, i, 0)),
                  pl.BlockSpec((1, GRID_W, r_tile, dm), lambda bi, i: (bi, 0, i, 0)),
                  pl.BlockSpec((da, d), lambda bi, i: (0, 0)),
                  pl.BlockSpec((dm, d), lambda bi, i: (0, 0)),
                  pl.BlockSpec((1, 1, d), lambda bi, i: (bi, 0, 0))],
        out_specs=pl.BlockSpec((1, PROJ_TM, d), lambda bi, i: (bi, i, 0)),
        out_shape=jax.ShapeDtypeStruct((b, t, d), F32),
        compiler_params=_cparams(("parallel", "parallel")),
        name="outproj",
    )(x3d, lru, mls_view, w_a, w_b, gate)


def kernel(x, c, ctx, c_ctx, ada_w, ada_b, ffn1_norm, ffn1_w_up, ffn1_w_down, mix_norm, w_in, b_mgate, lru_conv_w, lru_conv_b, lru_w_r, lru_b_r, lru_w_i, lru_b_i, lru_lam, mlstm_norm, w_out, ffn2_norm, ffn2_w_up, ffn2_w_down, final_norm):
    b, t, d = x.shape
    tc = ctx.shape[1]
    assert ada_w.shape[0] == 1, "single-layer block only"
    assert t % GRID_W == 0 and t % PROJ_TM == 0
    rows = t // GRID_W
    d_lru = lru_conv_w.shape[2]
    d_mv = mlstm_norm.shape[1]
    d_mqk = (w_in.shape[2] - 2 * d_lru - 2 * d_mv - 4 * M_HEADS) // 2

    pad = SUBLANES - b - 1
    cc = jnp.concatenate([c, c_ctx[None, :], jnp.zeros((pad, d), F32)], axis=0)
    mod = _ada_call(cc, ada_w[0], ada_b[0][None, :]).reshape(SUBLANES, N_MOD, d)
    lat = lambda i: mod[:b, i][:, None, :]
    cxt = lambda i: mod[b:b + 1, i][:, None, :]
    row = lambda v: v[0][None, :]

    x1, ctx1 = _ffn_call(x.reshape(b * t, d), (lat(0), lat(1), lat(2)), row(ffn1_norm), ffn1_w_up[0],
                         ffn1_w_down[0], row(ffn1_norm), rows_per_mod=t, final_norm=False, name="ffn_pre",
                         extra=(ctx.reshape(b * tc, d), (cxt(0), cxt(1), cxt(2))))
    x1 = x1.reshape(b, t, d)
    ctx1 = ctx1.reshape(b, tc, d)

    n_gate = 4 * M_HEADS
    w_a, w_b = _split_w_in(w_in[0].T, 2 * d_lru)
    widths_a, dtypes_a = (d_lru, d_lru), (F32, F32)
    widths_b, dtypes_b = (d_mqk, d_mqk, d_mv, d_mv, LANES), (BF16, BF16, BF16, F32, F32)
    mixn = row(mix_norm)
    lx, lg = _proj_call(x1, mixn, lat(3), lat(4), w_a, widths_a, dtypes_a, tm=PROJ_TM, colmajor=False,
                        name="proj_lru")
    q, k, v, o, gt = _proj_call(x1.reshape(b, rows, GRID_W, d), mixn, lat(3), lat(4), w_b, widths_b,
                                dtypes_b, tm=PROJ_TM, colmajor=True, name="proj_mlstm")
    ctx_mod = lambda i: jnp.broadcast_to(cxt(i), (b, 1, d))
    lxc, _ = _proj_call(ctx1, mixn, ctx_mod(3), ctx_mod(4), w_a, widths_a, dtypes_a, tm=tc, colmajor=False,
                        name="proj_lru_ctx")
    qc, kc, vc, _, gtc = _proj_call(ctx1, mixn, ctx_mod(3), ctx_mod(4), w_b, widths_b, dtypes_b, tm=tc,
                                    colmajor=False, name="proj_mlstm_ctx")

    lru_lat = _lru_call(lx, lg, lxc, lru_conv_w[0], row(lru_conv_b), lru_w_r[0], lru_b_r[0], lru_w_i[0],
                        lru_b_i[0], lru_lam[0])
    bias = jnp.concatenate([b_mgate[0], jnp.zeros((LANES - n_gate,), F32)])[None, :]
    mls_cm = _mlstm_call(q, k, v, o, gt, qc, kc, vc, gtc, bias, row(mlstm_norm))

    w_out0 = w_out[0].astype(BF16)
    x2 = _outproj_call(x1, lru_lat, mls_cm, w_out0[:d_lru], w_out0[d_lru:], lat(5))

    out, = _ffn_call(x2.reshape(b * t, d), (lat(6), lat(7), lat(8)), row(ffn2_norm), ffn2_w_up[0],
                     ffn2_w_down[0], final_norm[None, :], rows_per_mod=t, final_norm=True, name="ffn_post")
    return out.reshape(b, t, d)
```

```python
import functools

import jax
import jax.numpy as jnp
from jax import lax
from jax.experimental import pallas as pl
from jax.experimental.pallas import tpu as pltpu

F32 = jnp.float32
BF16 = jnp.bfloat16

GRID_W = 64
LRU_BLOCKS = 8
CONV_W = 4
LRU_C = 8.0
M_HEADS = 8
CHUNK = 128
N_MOD = 9
EPS = 1e-6
HALF = 0.5

LANES = 128
SUBLANES = 8
BF16_MANTISSA_BITS = 7
VMEM_LIMIT_BYTES = 58 * 1024 * 1024

FFN_TM = 1024
FFN_TF = 256
FFN_EDGE_ROWS = 512
PROJ_ROWS = 256
PROJ_TM = 512
LRU_GATE_ROWS = 512
LRU_SEQ_SCAN_MAX = 64
ADA_TN = 2048
WSPLIT_TN = 512


def _cparams(sem):
    return pltpu.CompilerParams(dimension_semantics=sem, vmem_limit_bytes=VMEM_LIMIT_BYTES)


def _sigmoid(z):
    return 0.5 * jnp.tanh(0.5 * z) + 0.5


def _rms(x, g):
    return x * lax.rsqrt(jnp.mean(x * x, axis=-1, keepdims=True) + EPS) * g


def _ada_body(c_ref, w_ref, b_ref, o_ref):
    s = c_ref[...]
    s = s * jax.nn.sigmoid(s)
    o_ref[...] = jnp.dot(s, w_ref[...], preferred_element_type=F32) + b_ref[...]


def _ada_call(cc, w, b):
    rows, d = cc.shape
    n = w.shape[1]
    return pl.pallas_call(
        _ada_body,
        grid=(n // ADA_TN,),
        in_specs=[pl.BlockSpec((rows, d), lambda j: (0, 0)),
                  pl.BlockSpec((d, ADA_TN), lambda j: (0, j)),
                  pl.BlockSpec((1, ADA_TN), lambda j: (0, j))],
        out_specs=pl.BlockSpec((rows, ADA_TN), lambda j: (0, j)),
        out_shape=jax.ShapeDtypeStruct((rows, n), F32),
        compiler_params=_cparams(("arbitrary",)),
        name="ada_mod",
    )(cc, w, b)


def _ffn_body(*refs, final_norm, n_groups):
    groups_in = [refs[4 * k:4 * k + 4] for k in range(n_groups)]
    g_ref, wg_ref, wu_ref, wd_ref, fin_ref = refs[4 * n_groups:4 * n_groups + 5]
    out_refs = refs[4 * n_groups + 5:5 * n_groups + 5]
    h_scr = refs[-1]
    j = pl.program_id(1)
    last = pl.num_programs(1) - 1
    groups, off = [], 0
    for (x_ref, sh_ref, sc_ref, gate_ref), o_ref in zip(groups_in, out_refs):
        n = x_ref.shape[0]
        blocks = lambda rb: [slice(r0, r0 + min(rb, n)) for r0 in range(0, n, min(rb, n))]
        groups.append((x_ref, sh_ref, sc_ref, gate_ref, o_ref, off, blocks(FFN_EDGE_ROWS),
                       blocks(FFN_EDGE_ROWS // 2 if final_norm else FFN_EDGE_ROWS)))
        off += n

    def contrib(h):
        g = jnp.dot(h, wg_ref[...].astype(BF16), preferred_element_type=F32)
        u = jnp.dot(h, wu_ref[...].astype(BF16), preferred_element_type=F32)
        a = (g * jax.nn.sigmoid(g) * u).astype(BF16)
        return jnp.dot(a, wd_ref[...].astype(BF16), preferred_element_type=F32)

    @pl.when(j == 0)
    def _():
        for x_ref, sh_ref, sc_ref, _, o_ref, off, blocks, _ in groups:
            for rows in blocks:
                h = (_rms(x_ref[rows, :], g_ref[...]) * (1.0 + sc_ref[0]) + sh_ref[0]).astype(BF16)
                h_scr[off + rows.start:off + rows.stop, :] = h
                o_ref[rows, :] = contrib(h)

    @pl.when((j > 0) & (j < last))
    def _():
        res = contrib(h_scr[...])
        for _, _, _, _, o_ref, off, _, _ in groups:
            o_ref[...] += res[off:off + o_ref.shape[0]]

    @pl.when(j == last)
    def _():
        for x_ref, _, _, gate_ref, o_ref, off, _, blocks in groups:
            for rows in blocks:
                acc = o_ref[rows, :] + contrib(h_scr[off + rows.start:off + rows.stop, :])
                y = x_ref[rows, :] + HALF * gate_ref[0] * acc
                if final_norm:
                    y = _rms(y, fin_ref[...])
                o_ref[rows, :] = y


def _ffn_call(x2d, mods, norm_g, w_up, w_down, fin_g, *, rows_per_mod, final_norm, name, extra=None):
    m, d = x2d.shape
    f = w_down.shape[0]
    tm = min(FFN_TM, m)
    n_tiles = m // tm
    tiles_per_mod = rows_per_mod // tm
    nf = f // FFN_TF
    vec_spec = pl.BlockSpec((1, d), lambda i, j: (0, 0))
    row_spec = lambda rows: pl.BlockSpec((rows, d), lambda i, j: (i, 0))
    mod_spec = pl.BlockSpec((1, 1, d), lambda i, j: (i // tiles_per_mod, 0, 0))
    const_mod_spec = pl.BlockSpec((1, 1, d), lambda i, j: (0, 0, 0))
    in_specs = [row_spec(tm), mod_spec, mod_spec, mod_spec]
    args = [x2d, *mods]
    out_rows = [tm]
    if extra is not None:
        x_e, mods_e = extra
        te = x_e.shape[0] // n_tiles
        assert te * n_tiles == x_e.shape[0] and te % (2 * SUBLANES) == 0
        in_specs += [row_spec(te), const_mod_spec, const_mod_spec, const_mod_spec]
        args += [x_e, *mods_e]
        out_rows.append(te)
    in_specs += [vec_spec,
                 pl.BlockSpec((d, FFN_TF), lambda i, j: (0, j)),
                 pl.BlockSpec((d, FFN_TF), lambda i, j: (0, j + nf)),
                 pl.BlockSpec((FFN_TF, d), lambda i, j: (j, 0)),
                 vec_spec]
    args += [norm_g, w_up, w_up, w_down, fin_g]
    return pl.pallas_call(
        functools.partial(_ffn_body, final_norm=final_norm, n_groups=len(out_rows)),
        grid=(n_tiles, nf),
        in_specs=in_specs,
        out_specs=[row_spec(r) for r in out_rows],
        out_shape=[jax.ShapeDtypeStruct((r * n_tiles, d), F32) for r in out_rows],
        scratch_shapes=[pltpu.VMEM((sum(out_rows), d), BF16)],
        compiler_params=_cparams(("parallel", "arbitrary")),
        name=name,
    )(*args)


def _wsplit_body(wt_ref, a_ref, b_ref, *, n_valid, n_a_blocks):
    j = pl.program_id(0)
    feat = j * WSPLIT_TN + lax.broadcasted_iota(jnp.int32, wt_ref.shape, 0)
    w = jnp.where(feat < n_valid, wt_ref[...], 0.0).T.astype(BF16)

    @pl.when(j < n_a_blocks)
    def _():
        a_ref[...] = w

    @pl.when(j >= n_a_blocks)
    def _():
        b_ref[...] = w


def _split_w_in(wt, n_a):
    n, d = wt.shape
    assert n_a % WSPLIT_TN == 0
    nb = pl.cdiv(n, WSPLIT_TN)
    na = n_a // WSPLIT_TN
    return pl.pallas_call(
        functools.partial(_wsplit_body, n_valid=n, n_a_blocks=na),
        grid=(nb,),
        in_specs=[pl.BlockSpec((WSPLIT_TN, d), lambda j: (j, 0))],
        out_specs=[pl.BlockSpec((d, WSPLIT_TN), lambda j: (0, jnp.minimum(j, na - 1))),
                   pl.BlockSpec((d, WSPLIT_TN), lambda j: (0, jnp.maximum(j - na, 0)))],
        out_shape=[jax.ShapeDtypeStruct((d, n_a), BF16),
                   jax.ShapeDtypeStruct((d, (nb - na) * WSPLIT_TN), BF16)],
        compiler_params=_cparams(("arbitrary",)),
        name="w_in_split",
    )(wt)


def _proj_body(x_ref, g_ref, sh_ref, sc_ref, *refs, group_widths, colmajor):
    w_refs, out_refs = refs[:len(group_widths)], refs[len(group_widths):]
    tm = out_refs[0].shape[1]
    rb = min(PROJ_ROWS, tm)
    norm = lambda x: (_rms(x, g_ref[...]) * (1.0 + sc_ref[0]) + sh_ref[0]).astype(BF16)
    if colmajor:
        r, c, d = x_ref.shape[1:]
        assert rb % r == 0
        xs = jnp.swapaxes(x_ref[0], 0, 1)
    for r0 in range(0, tm, rb):
        rows = slice(r0, r0 + rb)
        if colmajor:
            h = norm(xs[r0 // r:(r0 + rb) // r].reshape(rb, d))
        else:
            h = norm(x_ref[0, rows, :])
        outs = iter(out_refs)
        for w_ref, widths in zip(w_refs, group_widths):
            off = 0
            for wdt in widths:
                o_ref = next(outs)
                o_ref[0, rows, :] = jnp.dot(h, w_ref[:, off:off + wdt],
                                            preferred_element_type=F32).astype(o_ref.dtype)
                off += wdt


def _proj_call(x, norm_g, shift, scale, groups, *, tm, colmajor, name):
    b = x.shape[0]
    d = norm_g.shape[1]
    if colmajor:
        rows, gw = x.shape[1:3]
        cols = tm // rows
        assert cols == SUBLANES and gw % cols == 0
        n_tiles = gw // cols
        x_spec = pl.BlockSpec((1, rows, cols, d), lambda bi, i: (bi, 0, i, 0))
    else:
        n_tiles = x.shape[1] // tm
        x_spec = pl.BlockSpec((1, tm, d), lambda bi, i: (bi, i, 0))
    mod_spec = pl.BlockSpec((1, 1, d), lambda bi, i: (bi, 0, 0))
    w_specs = [pl.BlockSpec((d, cols_), functools.partial(lambda bi, i, idx: (0, idx), idx=idx))
               for _, cols_, idx, _, _ in groups]
    assert all(sum(widths) <= cols_ for _, cols_, _, widths, _ in groups)
    outs = [(wdt, dt) for _, _, _, widths, dtypes in groups for wdt, dt in zip(widths, dtypes)]
    return pl.pallas_call(
        functools.partial(_proj_body, group_widths=tuple(tuple(g[3]) for g in groups), colmajor=colmajor),
        grid=(b, n_tiles),
        in_specs=[x_spec, pl.BlockSpec((1, d), lambda bi, i: (0, 0)), mod_spec, mod_spec] + w_specs,
        out_specs=[pl.BlockSpec((1, tm, wdt), lambda bi, i: (bi, i, 0)) for wdt, _ in outs],
        out_shape=[jax.ShapeDtypeStruct((b, n_tiles * tm, wdt), dt) for wdt, dt in outs],
        compiler_params=_cparams(("parallel", "parallel")),
        name=name,
    )(x, norm_g, shift, scale, *[g[0] for g in groups])


def _lru_conv(x_ref, cw_ref, cb_ref, pad_scr, xc_scr, t):
    zeros = jnp.zeros((SUBLANES, LANES), F32)
    pad_scr[0:SUBLANES, :] = zeros
    pad_scr[SUBLANES:SUBLANES + t, :] = x_ref[0]
    pad_scr[SUBLANES + t:2 * SUBLANES + t, :] = zeros
    ch = min(LRU_GATE_ROWS, t)
    win_rows = ch + 2 * SUBLANES

    def body(c, carry):
        off = pl.multiple_of(c * ch, SUBLANES)
        win = pad_scr[pl.ds(off, win_rows), :]
        acc = cb_ref[...] + cw_ref[2:3, :] * win[SUBLANES:SUBLANES + ch]
        for k, shift in ((0, 2), (1, 1), (3, win_rows - 1)):
            acc = acc + cw_ref[k:k + 1, :] * pltpu.roll(win, shift, 0)[SUBLANES:SUBLANES + ch]
        xc_scr[pl.ds(off, ch), :] = acc
        return carry

    lax.fori_loop(0, t // ch, body, 0)


def _lru_gates(xc_scr, t, d, wr_ref, br_ref, wi_ref, bi_ref, lam_ref, a_scr, b_scr):
    ch = min(LRU_GATE_ROWS, t)
    half_unit = (-0.5 * LRU_C) * jax.nn.softplus(-lam_ref[d:d + 1, :])
    wr = (0.5 * wr_ref[d, 0]).astype(BF16)
    wi = (0.5 * wi_ref[d, 0]).astype(BF16)
    br = 0.5 * br_ref[d:d + 1, :]
    bi = 0.5 * bi_ref[d:d + 1, :]

    def body(c, carry):
        off = pl.multiple_of(c * ch, SUBLANES)
        xc = xc_scr[pl.ds(off, ch), :]
        xb = xc.astype(BF16)
        tr = jnp.tanh(jnp.dot(xb, wr, preferred_element_type=F32) + br)
        ti = jnp.tanh(jnp.dot(xb, wi, preferred_element_type=F32) + bi)
        log_a = half_unit * tr + half_unit
        a = jnp.exp(log_a)
        a_scr[pl.ds(off, ch), :] = a
        y = -jnp.tanh(log_a) * (a * a + 1.0)
        mult = jnp.where(y > 0.0, y * lax.rsqrt(y), 0.0)
        b_scr[pl.ds(off, ch), :] = (mult * xc) * (0.5 * ti + 0.5)
        return carry

    lax.fori_loop(0, t // ch, body, 0, unroll=min(4, t // ch))


def _lru_scan(a_ref, b_ref, t, d, h0, out_ref, accumulate=False):
    groups = t // SUBLANES
    row = lax.broadcasted_iota(jnp.int32, (SUBLANES, LANES), 0)

    def body(g, carry):
        gi = g if d == 0 else groups - 1 - g
        off = pl.multiple_of(gi * SUBLANES, SUBLANES)
        a = a_ref[pl.ds(off, SUBLANES), :]
        bv = b_ref[pl.ds(off, SUBLANES), :]
        for k in (1, 2, 4):
            shift = k if d == 0 else SUBLANES - k
            valid = (row >= k) if d == 0 else (row < SUBLANES - k)
            a_prev = pltpu.roll(a, shift, 0)
            b_prev = pltpu.roll(bv, shift, 0)
            bv = jnp.where(valid, a * b_prev + bv, bv)
            a = jnp.where(valid, a * a_prev, a)
        h = a * carry + bv
        if out_ref is not None:
            if accumulate:
                out_ref[pl.ds(off, SUBLANES), :] += h
            else:
                out_ref[pl.ds(off, SUBLANES), :] = h
        return h[SUBLANES - 1:SUBLANES, :] if d == 0 else h[0:1, :]

    return lax.fori_loop(0, groups, body, h0, unroll=min(4, groups))


def _lru_scan_planes(a_ref, b_ref, n, d, h0, out_ref, accumulate, levels):
    if n <= LRU_SEQ_SCAN_MAX:
        _lru_scan(a_ref, b_ref, n, d, h0, out_ref, accumulate)
        return
    g = n // SUBLANES
    p_scr, q_scr, s_scr = levels[0]
    order = list(range(SUBLANES)) if d == 0 else list(range(SUBLANES - 1, -1, -1))
    group_rows = SUBLANES * SUBLANES

    def up(v, carry):
        base = pl.multiple_of(v * group_rows, group_rows)
        rows = pl.ds(pl.multiple_of(v * SUBLANES, SUBLANES), SUBLANES)
        p = h = None
        for j in order:
            a = a_ref[pl.ds(base + j, SUBLANES, stride=SUBLANES), :]
            bv = b_ref[pl.ds(base + j, SUBLANES, stride=SUBLANES), :]
            if p is None:
                p, h = a, bv
            else:
                h = a * h + bv
                p = a * p
            p_scr[j, rows, :] = p
            q_scr[j, rows, :] = h
        return carry

    lax.fori_loop(0, g // SUBLANES, up, 0, unroll=4)

    last = order[-1]
    _lru_scan_planes(p_scr.at[last], q_scr.at[last], g, d, h0, s_scr, False, levels[1:])
    s = s_scr[...]
    row = lax.broadcasted_iota(jnp.int32, (g, LANES), 0)
    if d == 0:
        s_scr[...] = jnp.where(row == 0, h0, pltpu.roll(s, 1, 0))
    else:
        s_scr[...] = jnp.where(row == g - 1, h0, pltpu.roll(s, g - 1, 0))

    def down(v, carry):
        base = pl.multiple_of(v * group_rows, group_rows)
        rows = pl.ds(pl.multiple_of(v * SUBLANES, SUBLANES), SUBLANES)
        x = s_scr[rows, :]
        for j in range(SUBLANES):
            dst = pl.ds(base + j, SUBLANES, stride=SUBLANES)
            val = q_scr[j, rows, :] + p_scr[j, rows, :] * x
            if accumulate:
                val = val + out_ref[dst, :]
            out_ref[dst, :] = val
        return carry

    lax.fori_loop(0, g // SUBLANES, down, 0, unroll=4)


def _lru_body(lx_ref, lg_ref, lxc_ref, cw_ref, cb_ref, wr_ref, br_ref, wi_ref, bi_ref, lam_ref, o_ref,
              pad_scr, xc_scr, xcc_scr, a_scr, b_scr, h_scr, *level_scr):
    levels = [level_scr[i:i + 3] for i in range(0, len(level_scr), 3)]
    t = lx_ref.shape[1]
    tc = lxc_ref.shape[1]
    _lru_conv(lxc_ref, cw_ref, cb_ref, pad_scr, xcc_scr, tc)
    _lru_conv(lx_ref, cw_ref, cb_ref, pad_scr, xc_scr, t)
    gate_refs = (wr_ref, br_ref, wi_ref, bi_ref, lam_ref)
    for d in (0, 1):
        _lru_gates(xcc_scr, tc, d, *gate_refs, a_scr, b_scr)
        h0 = _lru_scan(a_scr, b_scr, tc, d, jnp.zeros((1, LANES), F32), None)
        _lru_gates(xc_scr, t, d, *gate_refs, a_scr, b_scr)
        _lru_scan_planes(a_scr, b_scr, t, d, h0, h_scr, d == 1, levels)
    o_ref[0] = (jax.nn.gelu(lg_ref[0]) * h_scr[...]).astype(o_ref.dtype)


def _lru_call(lx, lg, lxc, conv_w, conv_b, w_r, b_r, w_i, b_i, lam):
    b, t, d_lru = lx.shape
    tc = lxc.shape[1]
    nb = d_lru // LANES
    assert w_r.shape == (2, nb, LANES, LANES)
    seq = lambda n: pl.BlockSpec((1, n, LANES), lambda bi, j: (bi, 0, j))
    vec = lambda n: pl.BlockSpec((n, LANES), lambda bi, j: (0, j))
    wblk = pl.BlockSpec((2, 1, LANES, LANES), lambda bi, j: (0, j, 0, 0))
    level_scr = []
    n = t
    while n > LRU_SEQ_SCAN_MAX:
        assert n % (SUBLANES * SUBLANES) == 0
        n //= SUBLANES
        level_scr += [pltpu.VMEM((SUBLANES, n, LANES), F32), pltpu.VMEM((SUBLANES, n, LANES), F32),
                      pltpu.VMEM((n, LANES), F32)]
    return pl.pallas_call(
        _lru_body,
        grid=(b, nb),
        in_specs=[seq(t), seq(t), seq(tc), vec(CONV_W), vec(1), wblk, vec(2), wblk, vec(2), vec(2)],
        out_specs=seq(t),
        out_shape=jax.ShapeDtypeStruct((b, t, d_lru), BF16),
        scratch_shapes=[pltpu.VMEM((t + 2 * SUBLANES, LANES), F32),
                        pltpu.VMEM((t, LANES), F32), pltpu.VMEM((tc, LANES), F32),
                        pltpu.VMEM((t, LANES), F32), pltpu.VMEM((t, LANES), F32),
                        pltpu.VMEM((t, LANES), F32)] + level_scr,
        compiler_params=_cparams(("parallel", "parallel")),
        name="rglru",
    )(lx, lg, lxc, conv_w, conv_b, w_r, b_r, w_i, b_i, lam)


FWD_GATE_LANES = 2 * M_HEADS


def _split3_dot(tri, x):
    hi = x.astype(BF16)
    r1 = x - hi.astype(F32)
    mid = r1.astype(BF16)
    lo = (r1 - mid.astype(F32)).astype(BF16)
    return (jnp.dot(tri, hi, preferred_element_type=F32) + jnp.dot(tri, mid, preferred_element_type=F32)
            + jnp.dot(tri, lo, preferred_element_type=F32))


def _cummax_rows(x, row, reverse):
    n = x.shape[0]
    k = 1
    while k < n:
        if reverse:
            x = jnp.where(row < n - k, jnp.maximum(x, pltpu.roll(x, n - k, 0)), x)
        else:
            x = jnp.where(row >= k, jnp.maximum(x, pltpu.roll(x, k, 0)), x)
        k *= 2
    return x


def _round_up_bf16(x):
    return (x + jnp.abs(x) * (2.0 ** -BF16_MANTISSA_BITS)).astype(BF16)


def _mlstm_gate_prep(gates, kidx, bc_scr, cm_scr, xt_scr, tot_scr, gmax_scr):
    L = gates.shape[0]
    ti = lax.broadcasted_iota(jnp.int32, (L, L), 0)
    si = lax.broadcasted_iota(jnp.int32, (L, L), 1)
    row = lax.broadcasted_iota(jnp.int32, (L, LANES), 0)
    fwd = lax.broadcasted_iota(jnp.int32, (L, LANES), 1) < FWD_GATE_LANES
    lf = jax.nn.log_sigmoid(gates)
    bc_f = _split3_dot((si <= ti).astype(BF16), lf)
    bc_b = _split3_dot((si >= ti).astype(BF16), lf)
    bc = jnp.where(fwd, bc_f, bc_b)
    tot = jnp.where(fwd[0:1], bc_f[L - 1:L], bc_b[0:1])
    x = pltpu.roll(gates, SUBLANES, 1) - bc
    cm = jnp.where(fwd, _cummax_rows(x, row, False), _cummax_rows(x, row, True))
    gmax = jnp.max(tot + x, axis=0, keepdims=True)
    xt = x.T
    bc_scr[kidx] = bc
    cm_scr[kidx] = _round_up_bf16(cm)
    xt_scr[kidx, 0:SUBLANES, :] = xt[SUBLANES:2 * SUBLANES]
    xt_scr[kidx, SUBLANES:2 * SUBLANES, :] = xt[3 * SUBLANES:4 * SUBLANES]
    tot_scr[kidx] = jnp.broadcast_to(tot, (SUBLANES, LANES))
    gmax_scr[kidx] = jnp.broadcast_to(gmax, (SUBLANES, LANES))


def _mlstm_chunk(q, v, kidx, step, d, scr, need_h, dk):
    bc_scr, cm_scr, xt_scr, kt_scr, tot_scr, m_in_scr, m_out_scr, s_scr = scr
    L = q.shape[0]
    ti = lax.broadcasted_iota(jnp.int32, (L, L), 0)
    si = lax.broadcasted_iota(jnp.int32, (L, L), 1)
    causal = (si <= ti) if d == 0 else (si >= ti)
    lane = lax.broadcasted_iota(jnp.int32, (L, LANES), 1)
    head0_rows = lax.broadcasted_iota(jnp.int32, (LANES, L), 0) < dk
    srow = lax.broadcasted_iota(jnp.int32, (LANES, 2 * LANES), 0)
    k_t = kt_scr[kidx]
    tot_v, m_in_v, m_out_v = tot_scr[kidx], m_in_scr[step], m_out_scr[step]
    ones = jnp.ones((L, LANES), BF16)
    v1 = jnp.concatenate([v[:, :LANES], ones, v[:, LANES:], ones], axis=1)
    cf0 = FWD_GATE_LANES * d + SUBLANES
    pick = lambda vals, e: vals[0:1, cf0 + e:cf0 + e + 1]
    x_rows = [xt_scr[kidx, SUBLANES * d + e:SUBLANES * d + e + 1, :] for e in (0, 1)]
    s_old = s_scr[d]
    hs = []
    if need_h:
        bc = bc_scr[kidx]
        sel_lane = lax.broadcasted_iota(jnp.int32, (LANES, 2 * LANES), 0)
        sel_col = lax.broadcasted_iota(jnp.int32, (LANES, 2 * LANES), 1)
        sel = (sel_lane == jnp.where(sel_col < LANES, cf0, cf0 + 1)).astype(BF16)
        mx_all = jnp.maximum(_round_up_bf16(m_in_v[0:1, :]), cm_scr[kidx])
        mx_tiles = jnp.dot(mx_all, sel, preferred_element_type=F32)
        q_heads = [jnp.where((lane >= e * dk) & (lane < (e + 1) * dk), q, jnp.zeros_like(q)) for e in (0, 1)]
        q2 = jnp.concatenate(q_heads, axis=0)
        qk = jnp.dot(q2, k_t, preferred_element_type=F32)
        qs = jnp.dot(q2, s_old.astype(BF16), preferred_element_type=F32)
        for e in (0, 1):
            cf = cf0 + e
            mx = mx_tiles[:, e * LANES:(e + 1) * LANES]
            w = qk[e * L:(e + 1) * L] * jnp.exp(jnp.where(causal, x_rows[e] - mx, -jnp.inf))
            s_inter = jnp.exp(pick(m_in_v, e) - mx)
            wv = jnp.dot(w.astype(BF16), v1[:, 2 * e * LANES:2 * (e + 1) * LANES],
                         preferred_element_type=F32)
            qs_e = qs[e * L:(e + 1) * L]
            num = s_inter * qs_e[:, :LANES] + wv[:, :LANES]
            den = s_inter * qs_e[:, LANES:] + wv[:, LANES:]
            m_row = bc[:, cf:cf + 1] + mx[:, cf:cf + 1]
            hs.append(num * (1.0 / jnp.maximum(jnp.abs(den[:, cf:cf + 1]), jnp.exp(-m_row))))
    wg_rows = [jnp.exp(pick(tot_v, e) + x_rows[e] - pick(m_out_v, e)) for e in (0, 1)]
    decays = [jnp.exp(pick(tot_v, e) + pick(m_in_v, e) - pick(m_out_v, e)) for e in (0, 1)]
    kw = (k_t.astype(F32) * jnp.where(head0_rows, wg_rows[0], wg_rows[1])).astype(BF16)
    own = jnp.concatenate([jnp.dot(kw[:dk], v1[:, :2 * LANES], preferred_element_type=F32),
                           jnp.dot(kw[dk:], v1[:, 2 * LANES:], preferred_element_type=F32)], axis=0)
    s_scr[d] = jnp.where(srow < dk, decays[0], decays[1]) * s_old + own
    return hs


def _mlstm_body(q_ref, k_ref, v_ref, o_ref, gt_ref, qc_ref, kc_ref, vc_ref, gtc_ref, bias_ref, gain_ref,
                out_ref, hs_scr, bc_scr, cm_scr, xt_scr, kt_scr, tot_scr, gmax_scr, m_in_scr, m_out_scr,
                s_scr, *, dk):
    t = q_ref.shape[1]
    tc = qc_ref.shape[1]
    L = CHUNK
    nc, ncc = t // L, tc // L
    n_steps = nc + ncc
    shift = (LANES - 2 * pl.program_id(1)) % LANES
    bias = pltpu.roll(jnp.broadcast_to(bias_ref[...], (SUBLANES, LANES)), shift, 1)[0:1, :]
    prep_scr = (bc_scr, cm_scr, xt_scr, tot_scr, gmax_scr)
    state_scr = (bc_scr, cm_scr, xt_scr, kt_scr, tot_scr, m_in_scr, m_out_scr, s_scr)
    scaled_t = lambda k: (k.astype(F32).T * (dk ** -0.5)).astype(BF16)

    for c in range(ncc):
        rows = slice(c * L, (c + 1) * L)
        _mlstm_gate_prep(pltpu.roll(gtc_ref[0, rows, :], shift, 1) + bias, c, *prep_scr)
        kt_scr[c] = scaled_t(kc_ref[0, rows, :])

    def prep(c, carry):
        rows = pl.ds(pl.multiple_of(c * L, L), L)
        _mlstm_gate_prep(pltpu.roll(gt_ref[0, rows, :], shift, 1) + bias, c + ncc, *prep_scr)
        kt_scr[c + ncc] = scaled_t(k_ref[0, rows, :])
        return carry

    lax.fori_loop(0, nc, prep, 0, unroll=8)

    fwd_id = lambda i: i
    bwd_id = lambda i: (ncc - 1 - i) if i < ncc else (n_steps - 1 - (i - ncc))
    fwd8 = lax.broadcasted_iota(jnp.int32, (SUBLANES, LANES), 1) < FWD_GATE_LANES
    m = jnp.zeros((SUBLANES, LANES), F32)
    for i in range(n_steps):
        tot = jnp.where(fwd8, tot_scr[fwd_id(i)], tot_scr[bwd_id(i)])
        gmax = jnp.where(fwd8, gmax_scr[fwd_id(i)], gmax_scr[bwd_id(i)])
        m_in_scr[i] = m
        m = jnp.maximum(tot + m, gmax)
        m_out_scr[i] = m

    s_scr[...] = jnp.zeros_like(s_scr)
    for i in range(ncc):
        for d, cid in ((0, fwd_id(i)), (1, bwd_id(i))):
            rows = slice(cid * L, (cid + 1) * L)
            _mlstm_chunk(qc_ref[0, rows, :], vc_ref[0, rows, :], cid, i, d, state_scr, False, dk)

    def body(j, carry):
        for d in (0, 1):
            cj = j if d == 0 else nc - 1 - j
            rows = pl.ds(pl.multiple_of(cj * L, L), L)
            hs = _mlstm_chunk(q_ref[0, rows, :], v_ref[0, rows, :], cj + ncc, j + ncc, d, state_scr, True, dk)
            hs_scr[d, rows, :] = jnp.concatenate(hs, axis=1)
        return carry

    lax.fori_loop(0, nc, body, 0, unroll=8)

    def fin(c, carry):
        rows = pl.ds(pl.multiple_of(c * L, L), L)
        hh = hs_scr[0, rows, :] + hs_scr[1, rows, :]
        outs = []
        for e in (0, 1):
            x = hh[:, e * LANES:(e + 1) * LANES]
            outs.append(x * lax.rsqrt(jnp.mean(x * x, axis=-1, keepdims=True) + EPS))
        y = jnp.concatenate(outs, axis=1) * gain_ref[...] * _sigmoid(o_ref[0, rows, :])
        out_ref[0, rows, :] = y.astype(out_ref.dtype)
        return carry

    lax.fori_loop(0, nc, fin, 0, unroll=4)


def _mlstm_call(q, k, v, o, gt, qc, kc, vc, gtc, bias, gain):
    b, t, d_qk = q.shape
    tc = qc.shape[1]
    d_v = v.shape[2]
    pairs = M_HEADS // 2
    dk = d_qk // M_HEADS
    assert d_qk // pairs == LANES and d_v // pairs == 2 * LANES and CHUNK == LANES
    n_chunks = (t + tc) // CHUNK
    seq = lambda n, w: pl.BlockSpec((1, n, w), lambda bi, p: (bi, 0, p))
    allg = lambda n: pl.BlockSpec((1, n, LANES), lambda bi, p: (bi, 0, 0))
    chunk_f32 = lambda rows, cols: pltpu.VMEM((n_chunks, rows, cols), F32)
    return pl.pallas_call(
        functools.partial(_mlstm_body, dk=dk),
        grid=(b, pairs),
        in_specs=[seq(t, LANES), seq(t, LANES), seq(t, 2 * LANES), seq(t, 2 * LANES), allg(t),
                  seq(tc, LANES), seq(tc, LANES), seq(tc, 2 * LANES), allg(tc),
                  pl.BlockSpec((1, LANES), lambda bi, p: (0, 0)),
                  pl.BlockSpec((1, 2 * LANES), lambda bi, p: (0, p))],
        out_specs=seq(t, 2 * LANES),
        out_shape=jax.ShapeDtypeStruct((b, t, d_v), F32),
        scratch_shapes=[pltpu.VMEM((2, t, 2 * LANES), F32),
                        chunk_f32(CHUNK, LANES), pltpu.VMEM((n_chunks, CHUNK, LANES), BF16),
                        chunk_f32(2 * SUBLANES, CHUNK),
                        pltpu.VMEM((n_chunks, LANES, CHUNK), BF16),
                        chunk_f32(SUBLANES, LANES), chunk_f32(SUBLANES, LANES),
                        chunk_f32(SUBLANES, LANES), chunk_f32(SUBLANES, LANES),
                        pltpu.VMEM((2, LANES, 2 * LANES), F32)],
        compiler_params=_cparams(("parallel", "parallel")),
        name="mlstm",
    )(q, k, v, o, gt, qc, kc, vc, gtc, bias, gain)


def _outproj_body(x_ref, lru_ref, mls_ref, wa_ref, wb_ref, g_ref, o_ref):
    w_cols, r, dm = mls_ref.shape[1:]
    m = jnp.swapaxes(mls_ref[0], 0, 1).reshape(w_cols * r, dm).astype(BF16)
    y = (jnp.dot(lru_ref[0], wa_ref[...], preferred_element_type=F32)
         + jnp.dot(m, wb_ref[...], preferred_element_type=F32))
    o_ref[0] = x_ref[0] + g_ref[0] * y


def _outproj_call(x3d, lru, mls_cm, w_a, w_b, gate):
    b, t, d = x3d.shape
    da, dm = lru.shape[2], mls_cm.shape[2]
    rows = t // GRID_W
    r_tile = PROJ_TM // GRID_W
    assert r_tile == SUBLANES
    mls_view = mls_cm.reshape(b, GRID_W, rows, dm)
    return pl.pallas_call(
        _outproj_body,
        grid=(b, t // PROJ_TM),
        in_specs=[pl.BlockSpec((1, PROJ_TM, d), lambda bi, i: (bi, i, 0)),
                  pl.BlockSpec((1, PROJ_TM, da), lambda bi, i: (bi, i, 0)),
                  pl.BlockSpec((1, GRID_W, r_tile, dm), lambda bi, i: (bi, 0, i, 0)),
                  pl.BlockSpec((da, d), lambda bi, i: (0, 0)),
                  pl.BlockSpec((dm, d), lambda bi, i: (0, 0)),
                  pl.BlockSpec((1, 1, d), lambda bi, i: (bi, 0, 0))],
        out_specs=pl.BlockSpec((1, PROJ_TM, d), lambda bi, i: (bi, i, 0)),
        out_shape=jax.ShapeDtypeStruct((b, t, d), F32),
        compiler_params=_cparams(("parallel", "parallel")),
        name="outproj",
    )(x3d, lru, mls_view, w_a, w_b, gate)


def kernel(x, c, ctx, c_ctx, ada_w, ada_b, ffn1_norm, ffn1_w_up, ffn1_w_down, mix_norm, w_in, b_mgate, lru_conv_w, lru_conv_b, lru_w_r, lru_b_r, lru_w_i, lru_b_i, lru_lam, mlstm_norm, w_out, ffn2_norm, ffn2_w_up, ffn2_w_down, final_norm):
    b, t, d = x.shape
    tc = ctx.shape[1]
    assert ada_w.shape[0] == 1, "single-layer block only"
    assert t % GRID_W == 0 and t % PROJ_TM == 0
    rows = t // GRID_W
    d_lru = lru_conv_w.shape[2]
    d_mv = mlstm_norm.shape[1]
    d_mqk = (w_in.shape[2] - 2 * d_lru - 2 * d_mv - 4 * M_HEADS) // 2

    pad = SUBLANES - b - 1
    cc = jnp.concatenate([c, c_ctx[None, :], jnp.zeros((pad, d), F32)], axis=0)
    mod = _ada_call(cc, ada_w[0], ada_b[0][None, :]).reshape(SUBLANES, N_MOD, d)
    lat = lambda i: mod[:b, i][:, None, :]
    cxt = lambda i: mod[b:b + 1, i][:, None, :]
    row = lambda v: v[0][None, :]

    x1, ctx1 = _ffn_call(x.reshape(b * t, d), (lat(0), lat(1), lat(2)), row(ffn1_norm), ffn1_w_up[0],
                         ffn1_w_down[0], row(ffn1_norm), rows_per_mod=t, final_norm=False, name="ffn_pre",
                         extra=(ctx.reshape(b * tc, d), (cxt(0), cxt(1), cxt(2))))
    x1 = x1.reshape(b, t, d)
    ctx1 = ctx1.reshape(b, tc, d)

    n_gate = 4 * M_HEADS
    w_a, w_b = _split_w_in(w_in[0].T, 2 * d_lru)
    group_a = (w_a, w_a.shape[1], 0, (d_lru, d_lru), (F32, F32))
    group_b = (w_b, w_b.shape[1], 0, (d_mqk, d_mqk, d_mv, d_mv, LANES), (BF16, BF16, BF16, F32, F32))
    mixn = row(mix_norm)
    lx, lg = _proj_call(x1, mixn, lat(3), lat(4), [group_a], tm=PROJ_TM, colmajor=False, name="proj_lru")
    q, k, v, o, gt = _proj_call(x1.reshape(b, rows, GRID_W, d), mixn, lat(3), lat(4), [group_b],
                                tm=PROJ_TM, colmajor=True, name="proj_mlstm")
    d_qkv = 2 * d_mqk + d_mv
    gate_col = d_qkv + d_mv
    assert gate_col % LANES == 0
    ctx_groups = [(w_a, d_lru, 0, (d_lru,), (F32,)),
                  (w_b, d_qkv, 0, (d_mqk, d_mqk, d_mv), (BF16, BF16, BF16)),
                  (w_b, LANES, gate_col // LANES, (LANES,), (F32,))]
    ctx_mod = lambda i: jnp.broadcast_to(cxt(i), (b, 1, d))
    lxc, qc, kc, vc, gtc = _proj_call(ctx1, mixn, ctx_mod(3), ctx_mod(4), ctx_groups, tm=tc, colmajor=False,
                                      name="proj_ctx")

    lru_lat = _lru_call(lx, lg, lxc, lru_conv_w[0], row(lru_conv_b), lru_w_r[0], lru_b_r[0], lru_w_i[0],
                        lru_b_i[0], lru_lam[0])
    bias = jnp.concatenate([b_mgate[0], jnp.zeros((LANES - n_gate,), F32)])[None, :]
    mls_cm = _mlstm_call(q, k, v, o, gt, qc, kc, vc, gtc, bias, row(mlstm_norm))

    w_out0 = w_out[0].astype(BF16)
    x2 = _outproj_call(x1, lru_lat, mls_cm, w_out0[:d_lru], w_out0[d_lru:], lat(5))

    out, = _ffn_call(x2.reshape(b * t, d), (lat(6), lat(7), lat(8)), row(ffn2_norm), ffn2_w_up[0],
                     ffn2_w_down[0], final_norm[None, :], rows_per_mod=t, final_norm=True, name="ffn_post")
    return out.reshape(b, t, d)
```

```python
import functools

import jax
import jax.numpy as jnp
from jax import lax
from jax.experimental import pallas as pl
from jax.experimental.pallas import tpu as pltpu

F32 = jnp.float32
BF16 = jnp.bfloat16

GRID_W = 64
LRU_BLOCKS = 8
CONV_W = 4
LRU_C = 8.0
M_HEADS = 8
CHUNK = 128
N_MOD = 9
EPS = 1e-6
HALF = 0.5

LANES = 128
SUBLANES = 8
BF16_MANTISSA_BITS = 7
VMEM_LIMIT_BYTES = 58 * 1024 * 1024

FFN_TM = 1024
FFN_TF = 256
FFN_EDGE_ROWS = 512
PROJ_ROWS = 256
PROJ_TM = 512
LRU_GATE_ROWS = 512
LRU_SEQ_SCAN_MAX = 64
ADA_TK = 256
WSPLIT_TN = 512


def _cparams(sem):
    return pltpu.CompilerParams(dimension_semantics=sem, vmem_limit_bytes=VMEM_LIMIT_BYTES)


def _sigmoid(z):
    return 0.5 * jnp.tanh(0.5 * z) + 0.5


def _rms(x, g):
    return x * lax.rsqrt(jnp.mean(x * x, axis=-1, keepdims=True) + EPS) * g


def _ada_body(c_ref, w_ref, b_ref, o_ref):
    k = pl.program_id(0)
    s = c_ref[...]
    part = jnp.dot(s * jax.nn.sigmoid(s), w_ref[...], preferred_element_type=F32)

    @pl.when(k == 0)
    def _():
        o_ref[...] = part + b_ref[...]

    @pl.when(k > 0)
    def _():
        o_ref[...] += part


def _ada_call(cc, w, b):
    rows, d = cc.shape
    n = w.shape[1]
    return pl.pallas_call(
        _ada_body,
        grid=(d // ADA_TK,),
        in_specs=[pl.BlockSpec((rows, ADA_TK), lambda k: (0, k)),
                  pl.BlockSpec((ADA_TK, n), lambda k: (k, 0)),
                  pl.BlockSpec((1, n), lambda k: (0, 0))],
        out_specs=pl.BlockSpec((rows, n), lambda k: (0, 0)),
        out_shape=jax.ShapeDtypeStruct((rows, n), F32),
        compiler_params=_cparams(("arbitrary",)),
        name="ada_mod",
    )(cc, w, b)


def _ffn_body(*refs, final_norm, n_groups):
    groups_in = [refs[4 * k:4 * k + 4] for k in range(n_groups)]
    g_ref, wg_ref, wu_ref, wd_ref, fin_ref = refs[4 * n_groups:4 * n_groups + 5]
    out_refs = refs[4 * n_groups + 5:5 * n_groups + 5]
    h_scr = refs[-1]
    j = pl.program_id(1)
    last = pl.num_programs(1) - 1
    groups, off = [], 0
    for (x_ref, sh_ref, sc_ref, gate_ref), o_ref in zip(groups_in, out_refs):
        n = x_ref.shape[0]
        blocks = lambda rb: [slice(r0, r0 + min(rb, n)) for r0 in range(0, n, min(rb, n))]
        groups.append((x_ref, sh_ref, sc_ref, gate_ref, o_ref, off, blocks(FFN_EDGE_ROWS),
                       blocks(FFN_EDGE_ROWS // 2 if final_norm else FFN_EDGE_ROWS)))
        off += n

    def contrib(h):
        g = jnp.dot(h, wg_ref[...].astype(BF16), preferred_element_type=F32)
        u = jnp.dot(h, wu_ref[...].astype(BF16), preferred_element_type=F32)
        a = (g * jax.nn.sigmoid(g) * u).astype(BF16)
        return jnp.dot(a, wd_ref[...].astype(BF16), preferred_element_type=F32)

    @pl.when(j == 0)
    def _():
        for x_ref, sh_ref, sc_ref, _, o_ref, off, blocks, _ in groups:
            for rows in blocks:
                h = (_rms(x_ref[rows, :], g_ref[...]) * (1.0 + sc_ref[0]) + sh_ref[0]).astype(BF16)
                h_scr[off + rows.start:off + rows.stop, :] = h
                o_ref[rows, :] = contrib(h)

    @pl.when((j > 0) & (j < last))
    def _():
        res = contrib(h_scr[...])
        for _, _, _, _, o_ref, off, _, _ in groups:
            o_ref[...] += res[off:off + o_ref.shape[0]]

    @pl.when(j == last)
    def _():
        for x_ref, _, _, gate_ref, o_ref, off, _, blocks in groups:
            for rows in blocks:
                acc = o_ref[rows, :] + contrib(h_scr[off + rows.start:off + rows.stop, :])
                y = x_ref[rows, :] + HALF * gate_ref[0] * acc
                if final_norm:
                    y = _rms(y, fin_ref[...])
                o_ref[rows, :] = y


def _ffn_call(x2d, mods, norm_g, w_up, w_down, fin_g, *, rows_per_mod, final_norm, name, extra=None):
    m, d = x2d.shape
    f = w_down.shape[0]
    tm = min(FFN_TM, m)
    n_tiles = m // tm
    tiles_per_mod = rows_per_mod // tm
    nf = f // FFN_TF
    vec_spec = pl.BlockSpec((1, d), lambda i, j: (0, 0))
    row_spec = lambda rows: pl.BlockSpec((rows, d), lambda i, j: (i, 0))
    mod_spec = pl.BlockSpec((1, 1, d), lambda i, j: (i // tiles_per_mod, 0, 0))
    const_mod_spec = pl.BlockSpec((1, 1, d), lambda i, j: (0, 0, 0))
    in_specs = [row_spec(tm), mod_spec, mod_spec, mod_spec]
    args = [x2d, *mods]
    out_rows = [tm]
    if extra is not None:
        x_e, mods_e = extra
        te = x_e.shape[0] // n_tiles
        assert te * n_tiles == x_e.shape[0] and te % (2 * SUBLANES) == 0
        in_specs += [row_spec(te), const_mod_spec, const_mod_spec, const_mod_spec]
        args += [x_e, *mods_e]
        out_rows.append(te)
    in_specs += [vec_spec,
                 pl.BlockSpec((d, FFN_TF), lambda i, j: (0, j)),
                 pl.BlockSpec((d, FFN_TF), lambda i, j: (0, j + nf)),
                 pl.BlockSpec((FFN_TF, d), lambda i, j: (j, 0)),
                 vec_spec]
    args += [norm_g, w_up, w_up, w_down, fin_g]
    return pl.pallas_call(
        functools.partial(_ffn_body, final_norm=final_norm, n_groups=len(out_rows)),
        grid=(n_tiles, nf),
        in_specs=in_specs,
        out_specs=[row_spec(r) for r in out_rows],
        out_shape=[jax.ShapeDtypeStruct((r * n_tiles, d), F32) for r in out_rows],
        scratch_shapes=[pltpu.VMEM((sum(out_rows), d), BF16)],
        compiler_params=_cparams(("parallel", "arbitrary")),
        name=name,
    )(*args)


def _wsplit_body(wt_ref, a_ref, b_ref, *, n_valid, n_a_blocks):
    j = pl.program_id(0)
    feat = j * WSPLIT_TN + lax.broadcasted_iota(jnp.int32, wt_ref.shape, 0)
    w = jnp.where(feat < n_valid, wt_ref[...], 0.0).T.astype(BF16)

    @pl.when(j < n_a_blocks)
    def _():
        a_ref[...] = w

    @pl.when(j >= n_a_blocks)
    def _():
        b_ref[...] = w


def _split_w_in(wt, n_a):
    n, d = wt.shape
    assert n_a % WSPLIT_TN == 0
    nb = pl.cdiv(n, WSPLIT_TN)
    na = n_a // WSPLIT_TN
    return pl.pallas_call(
        functools.partial(_wsplit_body, n_valid=n, n_a_blocks=na),
        grid=(nb,),
        in_specs=[pl.BlockSpec((WSPLIT_TN, d), lambda j: (j, 0))],
        out_specs=[pl.BlockSpec((d, WSPLIT_TN), lambda j: (0, jnp.minimum(j, na - 1))),
                   pl.BlockSpec((d, WSPLIT_TN), lambda j: (0, jnp.maximum(j - na, 0)))],
        out_shape=[jax.ShapeDtypeStruct((d, n_a), BF16),
                   jax.ShapeDtypeStruct((d, (nb - na) * WSPLIT_TN), BF16)],
        compiler_params=_cparams(("arbitrary",)),
        name="w_in_split",
    )(wt)


def _proj_body(x_ref, g_ref, sh_ref, sc_ref, *refs, group_widths, colmajor):
    w_refs, out_refs = refs[:len(group_widths)], refs[len(group_widths):]
    tm = out_refs[0].shape[1]
    rb = min(PROJ_ROWS // 2 if colmajor else PROJ_ROWS, tm)
    norm = lambda x: (_rms(x, g_ref[...]) * (1.0 + sc_ref[0]) + sh_ref[0]).astype(BF16)
    if colmajor:
        r, c, d = x_ref.shape[1:]
        assert rb % r == 0
        xs = jnp.swapaxes(x_ref[0], 0, 1)
    for r0 in range(0, tm, rb):
        rows = slice(r0, r0 + rb)
        if colmajor:
            h = norm(xs[r0 // r:(r0 + rb) // r].reshape(rb, d))
        else:
            h = norm(x_ref[0, rows, :])
        outs = iter(out_refs)
        for w_ref, widths in zip(w_refs, group_widths):
            off = 0
            for wdt in widths:
                o_ref = next(outs)
                o_ref[0, rows, :] = jnp.dot(h, w_ref[:, off:off + wdt],
                                            preferred_element_type=F32).astype(o_ref.dtype)
                off += wdt


def _proj_call(x, norm_g, shift, scale, groups, *, tm, colmajor, name):
    b = x.shape[0]
    d = norm_g.shape[1]
    if colmajor:
        rows, gw = x.shape[1:3]
        cols = tm // rows
        assert cols == SUBLANES and gw % cols == 0
        n_tiles = gw // cols
        x_spec = pl.BlockSpec((1, rows, cols, d), lambda bi, i: (bi, 0, i, 0))
    else:
        n_tiles = x.shape[1] // tm
        x_spec = pl.BlockSpec((1, tm, d), lambda bi, i: (bi, i, 0))
    mod_spec = pl.BlockSpec((1, 1, d), lambda bi, i: (bi, 0, 0))
    w_specs = [pl.BlockSpec((d, cols_), functools.partial(lambda bi, i, idx: (0, idx), idx=idx))
               for _, cols_, idx, _, _ in groups]
    assert all(sum(widths) <= cols_ for _, cols_, _, widths, _ in groups)
    outs = [(wdt, dt) for _, _, _, widths, dtypes in groups for wdt, dt in zip(widths, dtypes)]
    return pl.pallas_call(
        functools.partial(_proj_body, group_widths=tuple(tuple(g[3]) for g in groups), colmajor=colmajor),
        grid=(b, n_tiles),
        in_specs=[x_spec, pl.BlockSpec((1, d), lambda bi, i: (0, 0)), mod_spec, mod_spec] + w_specs,
        out_specs=[pl.BlockSpec((1, tm, wdt), lambda bi, i: (bi, i, 0)) for wdt, _ in outs],
        out_shape=[jax.ShapeDtypeStruct((b, n_tiles * tm, wdt), dt) for wdt, dt in outs],
        compiler_params=_cparams(("parallel", "parallel")),
        name=name,
    )(x, norm_g, shift, scale, *[g[0] for g in groups])


def _lru_conv(x_ref, cw_ref, cb_ref, pad_scr, xc_scr, t):
    zeros = jnp.zeros((SUBLANES, LANES), F32)
    pad_scr[0:SUBLANES, :] = zeros
    pad_scr[SUBLANES:SUBLANES + t, :] = x_ref[0]
    pad_scr[SUBLANES + t:2 * SUBLANES + t, :] = zeros
    ch = min(LRU_GATE_ROWS, t)
    win_rows = ch + 2 * SUBLANES

    def body(c, carry):
        off = pl.multiple_of(c * ch, SUBLANES)
        win = pad_scr[pl.ds(off, win_rows), :]
        acc = cb_ref[...] + cw_ref[2:3, :] * win[SUBLANES:SUBLANES + ch]
        for k, shift in ((0, 2), (1, 1), (3, win_rows - 1)):
            acc = acc + cw_ref[k:k + 1, :] * pltpu.roll(win, shift, 0)[SUBLANES:SUBLANES + ch]
        xc_scr[pl.ds(off, ch), :] = acc
        return carry

    lax.fori_loop(0, t // ch, body, 0)


def _lru_gates(xc_scr, t, d, wr_ref, br_ref, wi_ref, bi_ref, lam_ref, a_scr, b_scr):
    ch = min(LRU_GATE_ROWS, t)
    half_unit = (-0.5 * LRU_C) * jax.nn.softplus(-lam_ref[d:d + 1, :])
    wr = (0.5 * wr_ref[d, 0]).astype(BF16)
    wi = (0.5 * wi_ref[d, 0]).astype(BF16)
    br = 0.5 * br_ref[d:d + 1, :]
    bi = 0.5 * bi_ref[d:d + 1, :]

    def body(c, carry):
        off = pl.multiple_of(c * ch, SUBLANES)
        xc = xc_scr[pl.ds(off, ch), :]
        xb = xc.astype(BF16)
        tr = jnp.tanh(jnp.dot(xb, wr, preferred_element_type=F32) + br)
        ti = jnp.tanh(jnp.dot(xb, wi, preferred_element_type=F32) + bi)
        log_a = half_unit * tr + half_unit
        a = jnp.exp(log_a)
        a_scr[pl.ds(off, ch), :] = a
        y = -jnp.tanh(log_a) * (a * a + 1.0)
        mult = jnp.where(y > 0.0, y * lax.rsqrt(y), 0.0)
        b_scr[pl.ds(off, ch), :] = (mult * xc) * (0.5 * ti + 0.5)
        return carry

    lax.fori_loop(0, t // ch, body, 0, unroll=min(4, t // ch))


def _lru_scan(a_ref, b_ref, t, d, h0, out_ref, accumulate=False):
    groups = t // SUBLANES
    row = lax.broadcasted_iota(jnp.int32, (SUBLANES, LANES), 0)

    def body(g, carry):
        gi = g if d == 0 else groups - 1 - g
        off = pl.multiple_of(gi * SUBLANES, SUBLANES)
        a = a_ref[pl.ds(off, SUBLANES), :]
        bv = b_ref[pl.ds(off, SUBLANES), :]
        for k in (1, 2, 4):
            shift = k if d == 0 else SUBLANES - k
            valid = (row >= k) if d == 0 else (row < SUBLANES - k)
            a_prev = pltpu.roll(a, shift, 0)
            b_prev = pltpu.roll(bv, shift, 0)
            bv = jnp.where(valid, a * b_prev + bv, bv)
            a = jnp.where(valid, a * a_prev, a)
        h = a * carry + bv
        if out_ref is not None:
            if accumulate:
                out_ref[pl.ds(off, SUBLANES), :] += h
            else:
                out_ref[pl.ds(off, SUBLANES), :] = h
        return h[SUBLANES - 1:SUBLANES, :] if d == 0 else h[0:1, :]

    return lax.fori_loop(0, groups, body, h0, unroll=min(4, groups))


def _lru_scan_planes(a_ref, b_ref, n, d, h0, out_ref, accumulate, levels):
    if n <= LRU_SEQ_SCAN_MAX:
        _lru_scan(a_ref, b_ref, n, d, h0, out_ref, accumulate)
        return
    g = n // SUBLANES
    p_scr, q_scr, s_scr = levels[0]
    order = list(range(SUBLANES)) if d == 0 else list(range(SUBLANES - 1, -1, -1))
    group_rows = SUBLANES * SUBLANES

    def up(v, carry):
        base = pl.multiple_of(v * group_rows, group_rows)
        rows = pl.ds(pl.multiple_of(v * SUBLANES, SUBLANES), SUBLANES)
        p = h = None
        for j in order:
            a = a_ref[pl.ds(base + j, SUBLANES, stride=SUBLANES), :]
            bv = b_ref[pl.ds(base + j, SUBLANES, stride=SUBLANES), :]
            if p is None:
                p, h = a, bv
            else:
                h = a * h + bv
                p = a * p
            p_scr[j, rows, :] = p
            q_scr[j, rows, :] = h
        return carry

    lax.fori_loop(0, g // SUBLANES, up, 0, unroll=4)

    last = order[-1]
    _lru_scan_planes(p_scr.at[last], q_scr.at[last], g, d, h0, s_scr, False, levels[1:])
    s = s_scr[...]
    row = lax.broadcasted_iota(jnp.int32, (g, LANES), 0)
    if d == 0:
        s_scr[...] = jnp.where(row == 0, h0, pltpu.roll(s, 1, 0))
    else:
        s_scr[...] = jnp.where(row == g - 1, h0, pltpu.roll(s, g - 1, 0))

    def down(v, carry):
        base = pl.multiple_of(v * group_rows, group_rows)
        rows = pl.ds(pl.multiple_of(v * SUBLANES, SUBLANES), SUBLANES)
        x = s_scr[rows, :]
        for j in range(SUBLANES):
            dst = pl.ds(base + j, SUBLANES, stride=SUBLANES)
            val = q_scr[j, rows, :] + p_scr[j, rows, :] * x
            if accumulate:
                val = val + out_ref[dst, :]
            out_ref[dst, :] = val
        return carry

    lax.fori_loop(0, g // SUBLANES, down, 0, unroll=4)


def _lru_body(lx_ref, lg_ref, lxc_ref, cw_ref, cb_ref, wr_ref, br_ref, wi_ref, bi_ref, lam_ref, o_ref,
              pad_scr, xc_scr, xcc_scr, a_scr, b_scr, h_scr, *level_scr):
    levels = [level_scr[i:i + 3] for i in range(0, len(level_scr), 3)]
    t = lx_ref.shape[1]
    tc = lxc_ref.shape[1]
    _lru_conv(lxc_ref, cw_ref, cb_ref, pad_scr, xcc_scr, tc)
    _lru_conv(lx_ref, cw_ref, cb_ref, pad_scr, xc_scr, t)
    gate_refs = (wr_ref, br_ref, wi_ref, bi_ref, lam_ref)
    for d in (0, 1):
        _lru_gates(xcc_scr, tc, d, *gate_refs, a_scr, b_scr)
        h0 = _lru_scan(a_scr, b_scr, tc, d, jnp.zeros((1, LANES), F32), None)
        _lru_gates(xc_scr, t, d, *gate_refs, a_scr, b_scr)
        _lru_scan_planes(a_scr, b_scr, t, d, h0, h_scr, d == 1, levels)
    o_ref[0] = (jax.nn.gelu(lg_ref[0]) * h_scr[...]).astype(o_ref.dtype)


def _lru_call(lx, lg, lxc, conv_w, conv_b, w_r, b_r, w_i, b_i, lam):
    b, t, d_lru = lx.shape
    tc = lxc.shape[1]
    nb = d_lru // LANES
    assert w_r.shape == (2, nb, LANES, LANES)
    seq = lambda n: pl.BlockSpec((1, n, LANES), lambda bi, j: (bi, 0, j))
    vec = lambda n: pl.BlockSpec((n, LANES), lambda bi, j: (0, j))
    wblk = pl.BlockSpec((2, 1, LANES, LANES), lambda bi, j: (0, j, 0, 0))
    level_scr = []
    n = t
    while n > LRU_SEQ_SCAN_MAX:
        assert n % (SUBLANES * SUBLANES) == 0
        n //= SUBLANES
        level_scr += [pltpu.VMEM((SUBLANES, n, LANES), F32), pltpu.VMEM((SUBLANES, n, LANES), F32),
                      pltpu.VMEM((n, LANES), F32)]
    return pl.pallas_call(
        _lru_body,
        grid=(b, nb),
        in_specs=[seq(t), seq(t), seq(tc), vec(CONV_W), vec(1), wblk, vec(2), wblk, vec(2), vec(2)],
        out_specs=seq(t),
        out_shape=jax.ShapeDtypeStruct((b, t, d_lru), BF16),
        scratch_shapes=[pltpu.VMEM((t + 2 * SUBLANES, LANES), F32),
                        pltpu.VMEM((t, LANES), F32), pltpu.VMEM((tc, LANES), F32),
                        pltpu.VMEM((t, LANES), F32), pltpu.VMEM((t, LANES), F32),
                        pltpu.VMEM((t, LANES), F32)] + level_scr,
        compiler_params=_cparams(("parallel", "parallel")),
        name="rglru",
    )(lx, lg, lxc, conv_w, conv_b, w_r, b_r, w_i, b_i, lam)


FWD_GATE_LANES = 2 * M_HEADS


def _split3_dot(tri, x):
    hi = x.astype(BF16)
    r1 = x - hi.astype(F32)
    mid = r1.astype(BF16)
    lo = (r1 - mid.astype(F32)).astype(BF16)
    return (jnp.dot(tri, hi, preferred_element_type=F32) + jnp.dot(tri, mid, preferred_element_type=F32)
            + jnp.dot(tri, lo, preferred_element_type=F32))


def _cummax_rows(x, row, reverse):
    n = x.shape[0]
    k = 1
    while k < n:
        if reverse:
            x = jnp.where(row < n - k, jnp.maximum(x, pltpu.roll(x, n - k, 0)), x)
        else:
            x = jnp.where(row >= k, jnp.maximum(x, pltpu.roll(x, k, 0)), x)
        k *= 2
    return x


def _round_up_bf16(x):
    return (x + jnp.abs(x) * (2.0 ** -BF16_MANTISSA_BITS)).astype(BF16)


def _mlstm_gate_prep(gates, kidx, bc_scr, cm_scr, xt_scr, tot_scr, gmax_scr):
    L = gates.shape[0]
    ti = lax.broadcasted_iota(jnp.int32, (L, L), 0)
    si = lax.broadcasted_iota(jnp.int32, (L, L), 1)
    row = lax.broadcasted_iota(jnp.int32, (L, LANES), 0)
    fwd = lax.broadcasted_iota(jnp.int32, (L, LANES), 1) < FWD_GATE_LANES
    lf = jax.nn.log_sigmoid(gates)
    bc_f = _split3_dot((si <= ti).astype(BF16), lf)
    bc_b = _split3_dot((si >= ti).astype(BF16), lf)
    bc = jnp.where(fwd, bc_f, bc_b)
    tot = jnp.where(fwd[0:1], bc_f[L - 1:L], bc_b[0:1])
    x = pltpu.roll(gates, SUBLANES, 1) - bc
    cm = jnp.where(fwd, _cummax_rows(x, row, False), _cummax_rows(x, row, True))
    gmax = jnp.max(tot + x, axis=0, keepdims=True)
    xt = x.T
    bc_scr[kidx] = bc
    cm_scr[kidx] = _round_up_bf16(cm)
    xt_scr[kidx, 0:SUBLANES, :] = xt[SUBLANES:2 * SUBLANES]
    xt_scr[kidx, SUBLANES:2 * SUBLANES, :] = xt[3 * SUBLANES:4 * SUBLANES]
    tot_scr[kidx] = jnp.broadcast_to(tot, (SUBLANES, LANES))
    gmax_scr[kidx] = jnp.broadcast_to(gmax, (SUBLANES, LANES))


def _mlstm_chunk(q, v, kidx, step, d, scr, need_h, dk):
    bc_scr, cm_scr, xt_scr, kt_scr, tot_scr, m_in_scr, m_out_scr, s_scr = scr
    L = q.shape[0]
    ti = lax.broadcasted_iota(jnp.int32, (L, L), 0)
    si = lax.broadcasted_iota(jnp.int32, (L, L), 1)
    causal = (si <= ti) if d == 0 else (si >= ti)
    lane = lax.broadcasted_iota(jnp.int32, (L, LANES), 1)
    head0_rows = lax.broadcasted_iota(jnp.int32, (LANES, L), 0) < dk
    srow = lax.broadcasted_iota(jnp.int32, (LANES, 2 * LANES), 0)
    k_t = kt_scr[kidx]
    tot_v, m_in_v, m_out_v = tot_scr[kidx], m_in_scr[step], m_out_scr[step]
    ones = jnp.ones((L, LANES), BF16)
    v1 = jnp.concatenate([v[:, :LANES], ones, v[:, LANES:], ones], axis=1)
    cf0 = FWD_GATE_LANES * d + SUBLANES
    pick = lambda vals, e: vals[0:1, cf0 + e:cf0 + e + 1]
    x_rows = [xt_scr[kidx, SUBLANES * d + e:SUBLANES * d + e + 1, :] for e in (0, 1)]
    s_old = s_scr[d]
    hs = []
    if need_h:
        bc = bc_scr[kidx]
        sel_lane = lax.broadcasted_iota(jnp.int32, (LANES, 2 * LANES), 0)
        sel_col = lax.broadcasted_iota(jnp.int32, (LANES, 2 * LANES), 1)
        sel = (sel_lane == jnp.where(sel_col < LANES, cf0, cf0 + 1)).astype(BF16)
        mx_all = jnp.maximum(_round_up_bf16(m_in_v[0:1, :]), cm_scr[kidx])
        mx_tiles = jnp.dot(mx_all, sel, preferred_element_type=F32)
        q_heads = [jnp.where((lane >= e * dk) & (lane < (e + 1) * dk), q, jnp.zeros_like(q)) for e in (0, 1)]
        q2 = jnp.concatenate(q_heads, axis=0)
        qk = jnp.dot(q2, k_t, preferred_element_type=F32)
        qs = jnp.dot(q2, s_old.astype(BF16), preferred_element_type=F32)
        for e in (0, 1):
            cf = cf0 + e
            mx = mx_tiles[:, e * LANES:(e + 1) * LANES]
            w = qk[e * L:(e + 1) * L] * jnp.exp(jnp.where(causal, x_rows[e] - mx, -jnp.inf))
            s_inter = jnp.exp(pick(m_in_v, e) - mx)
            wv = jnp.dot(w.astype(BF16), v1[:, 2 * e * LANES:2 * (e + 1) * LANES],
                         preferred_element_type=F32)
            qs_e = qs[e * L:(e + 1) * L]
            num = s_inter * qs_e[:, :LANES] + wv[:, :LANES]
            den = s_inter * qs_e[:, LANES:] + wv[:, LANES:]
            m_row = bc[:, cf:cf + 1] + mx[:, cf:cf + 1]
            hs.append(num * (1.0 / jnp.maximum(jnp.abs(den[:, cf:cf + 1]), jnp.exp(-m_row))))
    wg_rows = [jnp.exp(pick(tot_v, e) + x_rows[e] - pick(m_out_v, e)) for e in (0, 1)]
    decays = [jnp.exp(pick(tot_v, e) + pick(m_in_v, e) - pick(m_out_v, e)) for e in (0, 1)]
    kw = (k_t.astype(F32) * jnp.where(head0_rows, wg_rows[0], wg_rows[1])).astype(BF16)
    own = jnp.concatenate([jnp.dot(kw[:dk], v1[:, :2 * LANES], preferred_element_type=F32),
                           jnp.dot(kw[dk:], v1[:, 2 * LANES:], preferred_element_type=F32)], axis=0)
    s_scr[d] = jnp.where(srow < dk, decays[0], decays[1]) * s_old + own
    return hs


def _mlstm_body(q_ref, k_ref, v_ref, o_ref, gt_ref, qc_ref, kc_ref, vc_ref, gtc_ref, bias_ref, gain_ref,
                out_ref, hs_scr, bc_scr, cm_scr, xt_scr, kt_scr, tot_scr, gmax_scr, m_in_scr, m_out_scr,
                s_scr, *, dk):
    t = q_ref.shape[1]
    tc = qc_ref.shape[1]
    L = CHUNK
    nc, ncc = t // L, tc // L
    n_steps = nc + ncc
    shift = (LANES - 2 * pl.program_id(1)) % LANES
    bias = pltpu.roll(jnp.broadcast_to(bias_ref[...], (SUBLANES, LANES)), shift, 1)[0:1, :]
    prep_scr = (bc_scr, cm_scr, xt_scr, tot_scr, gmax_scr)
    state_scr = (bc_scr, cm_scr, xt_scr, kt_scr, tot_scr, m_in_scr, m_out_scr, s_scr)
    scaled_t = lambda k: (k.astype(F32).T * (dk ** -0.5)).astype(BF16)

    for c in range(ncc):
        rows = slice(c * L, (c + 1) * L)
        _mlstm_gate_prep(pltpu.roll(gtc_ref[0, rows, :], shift, 1) + bias, c, *prep_scr)
        kt_scr[c] = scaled_t(kc_ref[0, rows, :])

    def prep(c, carry):
        rows = pl.ds(pl.multiple_of(c * L, L), L)
        _mlstm_gate_prep(pltpu.roll(gt_ref[0, rows, :], shift, 1) + bias, c + ncc, *prep_scr)
        kt_scr[c + ncc] = scaled_t(k_ref[0, rows, :])
        return carry

    lax.fori_loop(0, nc, prep, 0, unroll=8)

    fwd_id = lambda i: i
    bwd_id = lambda i: (ncc - 1 - i) if i < ncc else (n_steps - 1 - (i - ncc))
    fwd8 = lax.broadcasted_iota(jnp.int32, (SUBLANES, LANES), 1) < FWD_GATE_LANES
    m = jnp.zeros((SUBLANES, LANES), F32)
    for i in range(n_steps):
        tot = jnp.where(fwd8, tot_scr[fwd_id(i)], tot_scr[bwd_id(i)])
        gmax = jnp.where(fwd8, gmax_scr[fwd_id(i)], gmax_scr[bwd_id(i)])
        m_in_scr[i] = m
        m = jnp.maximum(tot + m, gmax)
        m_out_scr[i] = m

    s_scr[...] = jnp.zeros_like(s_scr)
    hs_scr[...] = jnp.zeros_like(hs_scr)
    for i in range(ncc):
        for d, cid in ((0, fwd_id(i)), (1, bwd_id(i))):
            rows = slice(cid * L, (cid + 1) * L)
            _mlstm_chunk(qc_ref[0, rows, :], vc_ref[0, rows, :], cid, i, d, state_scr, False, dk)

    def body(j, carry):
        for d in (0, 1):
            cj = j if d == 0 else nc - 1 - j
            rows = pl.ds(pl.multiple_of(cj * L, L), L)
            hs = _mlstm_chunk(q_ref[0, rows, :], v_ref[0, rows, :], cj + ncc, j + ncc, d, state_scr, True, dk)
            hs_scr[rows, :] += jnp.concatenate(hs, axis=1)
        return carry

    lax.fori_loop(0, nc, body, 0, unroll=8)

    def fin(c, carry):
        rows = pl.ds(pl.multiple_of(c * L, L), L)
        hh = hs_scr[rows, :]
        outs = []
        for e in (0, 1):
            x = hh[:, e * LANES:(e + 1) * LANES]
            outs.append(x * lax.rsqrt(jnp.mean(x * x, axis=-1, keepdims=True) + EPS))
        y = jnp.concatenate(outs, axis=1) * gain_ref[...] * _sigmoid(o_ref[0, rows, :])
        out_ref[0, rows, :] = y.astype(out_ref.dtype)
        return carry

    lax.fori_loop(0, nc, fin, 0, unroll=4)


def _mlstm_call(q, k, v, o, gt, qc, kc, vc, gtc, bias, gain):
    b, t, d_qk = q.shape
    tc = qc.shape[1]
    d_v = v.shape[2]
    pairs = M_HEADS // 2
    dk = d_qk // M_HEADS
    assert d_qk // pairs == LANES and d_v // pairs == 2 * LANES and CHUNK == LANES
    n_chunks = (t + tc) // CHUNK
    seq = lambda n, w: pl.BlockSpec((1, n, w), lambda bi, p: (bi, 0, p))
    allg = lambda n: pl.BlockSpec((1, n, LANES), lambda bi, p: (bi, 0, 0))
    chunk_f32 = lambda rows, cols: pltpu.VMEM((n_chunks, rows, cols), F32)
    return pl.pallas_call(
        functools.partial(_mlstm_body, dk=dk),
        grid=(b, pairs),
        in_specs=[seq(t, LANES), seq(t, LANES), seq(t, 2 * LANES), seq(t, 2 * LANES), allg(t),
                  seq(tc, LANES), seq(tc, LANES), seq(tc, 2 * LANES), allg(tc),
                  pl.BlockSpec((1, LANES), lambda bi, p: (0, 0)),
                  pl.BlockSpec((1, 2 * LANES), lambda bi, p: (0, p))],
        out_specs=seq(t, 2 * LANES),
        out_shape=jax.ShapeDtypeStruct((b, t, d_v), F32),
        scratch_shapes=[pltpu.VMEM((t, 2 * LANES), F32),
                        chunk_f32(CHUNK, LANES), pltpu.VMEM((n_chunks, CHUNK, LANES), BF16),
                        chunk_f32(2 * SUBLANES, CHUNK),
                        pltpu.VMEM((n_chunks, LANES, CHUNK), BF16),
                        chunk_f32(SUBLANES, LANES), chunk_f32(SUBLANES, LANES),
                        chunk_f32(SUBLANES, LANES), chunk_f32(SUBLANES, LANES),
                        pltpu.VMEM((2, LANES, 2 * LANES), F32)],
        compiler_params=_cparams(("parallel", "parallel")),
        name="mlstm",
    )(q, k, v, o, gt, qc, kc, vc, gtc, bias, gain)


def _outproj_body(x_ref, lru_ref, mls_ref, wa_ref, wb_ref, g_ref, o_ref):
    w_cols, r, dm = mls_ref.shape[1:]
    m = jnp.swapaxes(mls_ref[0], 0, 1).reshape(w_cols * r, dm).astype(BF16)
    y = (jnp.dot(lru_ref[0], wa_ref[...], preferred_element_type=F32)
         + jnp.dot(m, wb_ref[...], preferred_element_type=F32))
    o_ref[0] = x_ref[0] + g_ref[0] * y


def _outproj_call(x3d, lru, mls_cm, w_a, w_b, gate):
    b, t, d = x3d.shape
    da, dm = lru.shape[2], mls_cm.shape[2]
    rows = t // GRID_W
    r_tile = PROJ_TM // GRID_W
    assert r_tile == SUBLANES
    mls_view = mls_cm.reshape(b, GRID_W, rows, dm)
    return pl.pallas_call(
        _outproj_body,
        grid=(b, t // PROJ_TM),
        in_specs=[pl.BlockSpec((1, PROJ_TM, d), lambda bi, i: (bi, i, 0)),
                  pl.BlockSpec((1, PROJ_TM, da), lambda bi, i: (bi, i, 0)),
                  pl.BlockSpec((1, GRID_W, r_tile, dm), lambda bi, i: (bi, 0, i, 0)),
                  pl.BlockSpec((da, d), lambda bi, i: (0, 0)),
                  pl.BlockSpec((dm, d), lambda bi, i: (0, 0)),
                  pl.BlockSpec((1, 1, d), lambda bi, i: (bi, 0, 0))],
        out_specs=pl.BlockSpec((1, PROJ_TM, d), lambda bi, i: (bi, i, 0)),
        out_shape=jax.ShapeDtypeStruct((b, t, d), F32),
        compiler_params=_cparams(("parallel", "parallel")),
        name="outproj",
    )(x3d, lru, mls_view, w_a, w_b, gate)


def kernel(x, c, ctx, c_ctx, ada_w, ada_b, ffn1_norm, ffn1_w_up, ffn1_w_down, mix_norm, w_in, b_mgate, lru_conv_w, lru_conv_b, lru_w_r, lru_b_r, lru_w_i, lru_b_i, lru_lam, mlstm_norm, w_out, ffn2_norm, ffn2_w_up, ffn2_w_down, final_norm):
    b, t, d = x.shape
    tc = ctx.shape[1]
    assert ada_w.shape[0] == 1, "single-layer block only"
    assert t % GRID_W == 0 and t % PROJ_TM == 0
    rows = t // GRID_W
    d_lru = lru_conv_w.shape[2]
    d_mv = mlstm_norm.shape[1]
    d_mqk = (w_in.shape[2] - 2 * d_lru - 2 * d_mv - 4 * M_HEADS) // 2

    pad = SUBLANES - b - 1
    cc = jnp.concatenate([c, c_ctx[None, :], jnp.zeros((pad, d), F32)], axis=0)
    mod = _ada_call(cc, ada_w[0], ada_b[0][None, :]).reshape(SUBLANES, N_MOD, d)
    lat = lambda i: mod[:b, i][:, None, :]
    cxt = lambda i: mod[b:b + 1, i][:, None, :]
    row = lambda v: v[0][None, :]

    x1, ctx1 = _ffn_call(x.reshape(b * t, d), (lat(0), lat(1), lat(2)), row(ffn1_norm), ffn1_w_up[0],
                         ffn1_w_down[0], row(ffn1_norm), rows_per_mod=t, final_norm=False, name="ffn_pre",
                         extra=(ctx.reshape(b * tc, d), (cxt(0), cxt(1), cxt(2))))
    x1 = x1.reshape(b, t, d)
    ctx1 = ctx1.reshape(b, tc, d)

    n_gate = 4 * M_HEADS
    w_a, w_b = _split_w_in(w_in[0].T, 2 * d_lru)
    group_a = (w_a, w_a.shape[1], 0, (d_lru, d_lru), (F32, F32))
    group_b = (w_b, w_b.shape[1], 0, (d_mqk, d_mqk, d_mv, d_mv, LANES), (BF16, BF16, BF16, F32, F32))
    mixn = row(mix_norm)
    lx, lg = _proj_call(x1, mixn, lat(3), lat(4), [group_a], tm=PROJ_TM, colmajor=False, name="proj_lru")
    q, k, v, o, gt = _proj_call(x1.reshape(b, rows, GRID_W, d), mixn, lat(3), lat(4), [group_b],
                                tm=PROJ_TM, colmajor=True, name="proj_mlstm")
    d_qkv = 2 * d_mqk + d_mv
    gate_col = d_qkv + d_mv
    assert gate_col % LANES == 0
    ctx_groups = [(w_a, d_lru, 0, (d_lru,), (F32,)),
                  (w_b, d_qkv, 0, (d_mqk, d_mqk, d_mv), (BF16, BF16, BF16)),
                  (w_b, LANES, gate_col // LANES, (LANES,), (F32,))]
    ctx_mod = lambda i: jnp.broadcast_to(cxt(i), (b, 1, d))
    lxc, qc, kc, vc, gtc = _proj_call(ctx1, mixn, ctx_mod(3), ctx_mod(4), ctx_groups, tm=tc, colmajor=False,
                                      name="proj_ctx")

    lru_lat = _lru_call(lx, lg, lxc, lru_conv_w[0], row(lru_conv_b), lru_w_r[0], lru_b_r[0], lru_w_i[0],
                        lru_b_i[0], lru_lam[0])
    bias = jnp.concatenate([b_mgate[0], jnp.zeros((LANES - n_gate,), F32)])[None, :]
    mls_cm = _mlstm_call(q, k, v, o, gt, qc, kc, vc, gtc, bias, row(mlstm_norm))

    w_out0 = w_out[0].astype(BF16)
    x2 = _outproj_call(x1, lru_lat, mls_cm, w_out0[:d_lru], w_out0[d_lru:], lat(5))

    out, = _ffn_call(x2.reshape(b * t, d), (lat(6), lat(7), lat(8)), row(ffn2_norm), ffn2_w_up[0],
                     ffn2_w_down[0], final_norm[None, :], rows_per_mod=t, final_norm=True, name="ffn_post")
    return out.reshape(b, t, d)
```

```python
import functools

import jax
import jax.numpy as jnp
from jax import lax
from jax.experimental import pallas as pl
from jax.experimental.pallas import tpu as pltpu

F32 = jnp.float32
BF16 = jnp.bfloat16

GRID_W = 64
LRU_BLOCKS = 8
CONV_W = 4
LRU_C = 8.0
M_HEADS = 8
CHUNK = 128
N_MOD = 9
EPS = 1e-6
HALF = 0.5

LANES = 128
SUBLANES = 8
BF16_MANTISSA_BITS = 7
VMEM_LIMIT_BYTES = 58 * 1024 * 1024

FFN_TM = 1024
FFN_TF = 256
FFN_EDGE_ROWS = 512
PROJ_ROWS = 256
PROJ_TM = 512
LRU_GATE_ROWS = 512
LRU_SEQ_SCAN_MAX = 64
ADA_TK = 256
WSPLIT_TN = 512


def _cparams(sem):
    return pltpu.CompilerParams(dimension_semantics=sem, vmem_limit_bytes=VMEM_LIMIT_BYTES)


def _sigmoid(z):
    return 0.5 * jnp.tanh(0.5 * z) + 0.5


def _rms(x, g):
    return x * lax.rsqrt(jnp.mean(x * x, axis=-1, keepdims=True) + EPS) * g


def _ada_body(c_ref, w_ref, b_ref, o_ref):
    k = pl.program_id(0)
    s = c_ref[...]
    part = jnp.dot(s * jax.nn.sigmoid(s), w_ref[...], preferred_element_type=F32)

    @pl.when(k == 0)
    def _():
        o_ref[...] = part + b_ref[...]

    @pl.when(k > 0)
    def _():
        o_ref[...] += part


def _ada_call(cc, w, b):
    rows, d = cc.shape
    n = w.shape[1]
    return pl.pallas_call(
        _ada_body,
        grid=(d // ADA_TK,),
        in_specs=[pl.BlockSpec((rows, ADA_TK), lambda k: (0, k)),
                  pl.BlockSpec((ADA_TK, n), lambda k: (k, 0)),
                  pl.BlockSpec((1, n), lambda k: (0, 0))],
        out_specs=pl.BlockSpec((rows, n), lambda k: (0, 0)),
        out_shape=jax.ShapeDtypeStruct((rows, n), F32),
        compiler_params=_cparams(("arbitrary",)),
        name="ada_mod",
    )(cc, w, b)


def _ffn_body(*refs, final_norm, n_groups):
    groups_in = [refs[4 * k:4 * k + 4] for k in range(n_groups)]
    g_ref, wg_ref, wu_ref, wd_ref, fin_ref = refs[4 * n_groups:4 * n_groups + 5]
    out_refs = refs[4 * n_groups + 5:5 * n_groups + 5]
    h_scr = refs[-1]
    j = pl.program_id(1)
    last = pl.num_programs(1) - 1
    groups, off = [], 0
    for (x_ref, sh_ref, sc_ref, gate_ref), o_ref in zip(groups_in, out_refs):
        n = x_ref.shape[0]
        blocks = lambda rb: [slice(r0, r0 + min(rb, n)) for r0 in range(0, n, min(rb, n))]
        groups.append((x_ref, sh_ref, sc_ref, gate_ref, o_ref, off, blocks(FFN_EDGE_ROWS),
                       blocks(FFN_EDGE_ROWS // 2 if final_norm else FFN_EDGE_ROWS)))
        off += n

    def contrib(h):
        g = jnp.dot(h, wg_ref[...].astype(BF16), preferred_element_type=F32)
        u = jnp.dot(h, wu_ref[...].astype(BF16), preferred_element_type=F32)
        a = (g * jax.nn.sigmoid(g) * u).astype(BF16)
        return jnp.dot(a, wd_ref[...].astype(BF16), preferred_element_type=F32)

    @pl.when(j == 0)
    def _():
        for x_ref, sh_ref, sc_ref, _, o_ref, off, blocks, _ in groups:
            for rows in blocks:
                h = (_rms(x_ref[rows, :], g_ref[...]) * (1.0 + sc_ref[0]) + sh_ref[0]).astype(BF16)
                h_scr[off + rows.start:off + rows.stop, :] = h
                o_ref[rows, :] = contrib(h)

    @pl.when((j > 0) & (j < last))
    def _():
        res = contrib(h_scr[...])
        for _, _, _, _, o_ref, off, _, _ in groups:
            o_ref[...] += res[off:off + o_ref.shape[0]]

    @pl.when(j == last)
    def _():
        for x_ref, _, _, gate_ref, o_ref, off, _, blocks in groups:
            for rows in blocks:
                acc = o_ref[rows, :] + contrib(h_scr[off + rows.start:off + rows.stop, :])
                y = x_ref[rows, :] + HALF * gate_ref[0] * acc
                if final_norm:
                    y = _rms(y, fin_ref[...])
                o_ref[rows, :] = y


def _ffn_call(x2d, mods, norm_g, w_up, w_down, fin_g, *, rows_per_mod, final_norm, name, extra=None):
    m, d = x2d.shape
    f = w_down.shape[0]
    tm = min(FFN_TM, m)
    n_tiles = m // tm
    tiles_per_mod = rows_per_mod // tm
    nf = f // FFN_TF
    vec_spec = pl.BlockSpec((1, d), lambda i, j: (0, 0))
    row_spec = lambda rows: pl.BlockSpec((rows, d), lambda i, j: (i, 0))
    mod_spec = pl.BlockSpec((1, 1, d), lambda i, j: (i // tiles_per_mod, 0, 0))
    const_mod_spec = pl.BlockSpec((1, 1, d), lambda i, j: (0, 0, 0))
    in_specs = [row_spec(tm), mod_spec, mod_spec, mod_spec]
    args = [x2d, *mods]
    out_rows = [tm]
    if extra is not None:
        x_e, mods_e = extra
        te = x_e.shape[0] // n_tiles
        assert te * n_tiles == x_e.shape[0] and te % (2 * SUBLANES) == 0
        in_specs += [row_spec(te), const_mod_spec, const_mod_spec, const_mod_spec]
        args += [x_e, *mods_e]
        out_rows.append(te)
    in_specs += [vec_spec,
                 pl.BlockSpec((d, FFN_TF), lambda i, j: (0, j)),
                 pl.BlockSpec((d, FFN_TF), lambda i, j: (0, j + nf)),
                 pl.BlockSpec((FFN_TF, d), lambda i, j: (j, 0)),
                 vec_spec]
    args += [norm_g, w_up, w_up, w_down, fin_g]
    return pl.pallas_call(
        functools.partial(_ffn_body, final_norm=final_norm, n_groups=len(out_rows)),
        grid=(n_tiles, nf),
        in_specs=in_specs,
        out_specs=[row_spec(r) for r in out_rows],
        out_shape=[jax.ShapeDtypeStruct((r * n_tiles, d), F32) for r in out_rows],
        scratch_shapes=[pltpu.VMEM((sum(out_rows), d), BF16)],
        compiler_params=_cparams(("parallel", "arbitrary")),
        name=name,
    )(*args)


def _wsplit_body(wt_ref, a_ref, b_ref, *, n_valid, n_a_blocks):
    j = pl.program_id(0)
    feat = j * WSPLIT_TN + lax.broadcasted_iota(jnp.int32, wt_ref.shape, 0)
    w = jnp.where(feat < n_valid, wt_ref[...], 0.0).T.astype(BF16)

    @pl.when(j < n_a_blocks)
    def _():
        a_ref[...] = w

    @pl.when(j >= n_a_blocks)
    def _():
        b_ref[...] = w


def _split_w_in(wt, n_a):
    n, d = wt.shape
    assert n_a % WSPLIT_TN == 0
    nb = pl.cdiv(n, WSPLIT_TN)
    na = n_a // WSPLIT_TN
    return pl.pallas_call(
        functools.partial(_wsplit_body, n_valid=n, n_a_blocks=na),
        grid=(nb,),
        in_specs=[pl.BlockSpec((WSPLIT_TN, d), lambda j: (j, 0))],
        out_specs=[pl.BlockSpec((d, WSPLIT_TN), lambda j: (0, jnp.minimum(j, na - 1))),
                   pl.BlockSpec((d, WSPLIT_TN), lambda j: (0, jnp.maximum(j - na, 0)))],
        out_shape=[jax.ShapeDtypeStruct((d, n_a), BF16),
                   jax.ShapeDtypeStruct((d, (nb - na) * WSPLIT_TN), BF16)],
        compiler_params=_cparams(("arbitrary",)),
        name="w_in_split",
    )(wt)


def _proj_body(x_ref, g_ref, sh_ref, sc_ref, *refs, group_widths, colmajor):
    w_refs, out_refs = refs[:len(group_widths)], refs[len(group_widths):]
    tm = out_refs[0].shape[1]
    rb = min(PROJ_ROWS, tm)
    norm = lambda x: (_rms(x, g_ref[...]) * (1.0 + sc_ref[0]) + sh_ref[0]).astype(BF16)
    if colmajor:
        r, c, d = x_ref.shape[1:]
        assert rb % r == 0
        xs = jnp.swapaxes(x_ref[0], 0, 1)
    for r0 in range(0, tm, rb):
        rows = slice(r0, r0 + rb)
        if colmajor:
            h = norm(xs[r0 // r:(r0 + rb) // r].reshape(rb, d))
        else:
            h = norm(x_ref[0, rows, :])
        outs = iter(out_refs)
        for w_ref, widths in zip(w_refs, group_widths):
            off = 0
            for wdt in widths:
                o_ref = next(outs)
                o_ref[0, rows, :] = jnp.dot(h, w_ref[:, off:off + wdt],
                                            preferred_element_type=F32).astype(o_ref.dtype)
                off += wdt


def _proj_call(x, norm_g, shift, scale, groups, *, tm, colmajor, name):
    b = x.shape[0]
    d = norm_g.shape[1]
    if colmajor:
        rows, gw = x.shape[1:3]
        cols = tm // rows
        assert cols == SUBLANES and gw % cols == 0
        n_tiles = gw // cols
        x_spec = pl.BlockSpec((1, rows, cols, d), lambda bi, i: (bi, 0, i, 0))
    else:
        n_tiles = x.shape[1] // tm
        x_spec = pl.BlockSpec((1, tm, d), lambda bi, i: (bi, i, 0))
    mod_spec = pl.BlockSpec((1, 1, d), lambda bi, i: (bi, 0, 0))
    w_specs = [pl.BlockSpec((d, cols_), functools.partial(lambda bi, i, idx: (0, idx), idx=idx))
               for _, cols_, idx, _, _ in groups]
    assert all(sum(widths) <= cols_ for _, cols_, _, widths, _ in groups)
    outs = [(wdt, dt) for _, _, _, widths, dtypes in groups for wdt, dt in zip(widths, dtypes)]
    return pl.pallas_call(
        functools.partial(_proj_body, group_widths=tuple(tuple(g[3]) for g in groups), colmajor=colmajor),
        grid=(b, n_tiles),
        in_specs=[x_spec, pl.BlockSpec((1, d), lambda bi, i: (0, 0)), mod_spec, mod_spec] + w_specs,
        out_specs=[pl.BlockSpec((1, tm, wdt), lambda bi, i: (bi, i, 0)) for wdt, _ in outs],
        out_shape=[jax.ShapeDtypeStruct((b, n_tiles * tm, wdt), dt) for wdt, dt in outs],
        compiler_params=_cparams(("parallel", "parallel")),
        name=name,
    )(x, norm_g, shift, scale, *[g[0] for g in groups])


def _lru_conv(x_ref, cw_ref, cb_ref, pad_scr, xc_scr, t):
    zeros = jnp.zeros((SUBLANES, LANES), F32)
    pad_scr[0:SUBLANES, :] = zeros
    pad_scr[SUBLANES:SUBLANES + t, :] = x_ref[0]
    pad_scr[SUBLANES + t:2 * SUBLANES + t, :] = zeros
    ch = min(LRU_GATE_ROWS, t)
    win_rows = ch + 2 * SUBLANES

    def body(c, carry):
        off = pl.multiple_of(c * ch, SUBLANES)
        win = pad_scr[pl.ds(off, win_rows), :]
        acc = cb_ref[...] + cw_ref[2:3, :] * win[SUBLANES:SUBLANES + ch]
        for k, shift in ((0, 2), (1, 1), (3, win_rows - 1)):
            acc = acc + cw_ref[k:k + 1, :] * pltpu.roll(win, shift, 0)[SUBLANES:SUBLANES + ch]
        xc_scr[pl.ds(off, ch), :] = acc
        return carry

    lax.fori_loop(0, t // ch, body, 0)


def _lru_gates(xc_scr, t, d, wr_ref, br_ref, wi_ref, bi_ref, lam_ref, a_scr, b_scr):
    ch = min(LRU_GATE_ROWS, t)
    half_unit = (-0.5 * LRU_C) * jax.nn.softplus(-lam_ref[d:d + 1, :])
    wr = (0.5 * wr_ref[d, 0]).astype(BF16)
    wi = (0.5 * wi_ref[d, 0]).astype(BF16)
    br = 0.5 * br_ref[d:d + 1, :]
    bi = 0.5 * bi_ref[d:d + 1, :]

    def body(c, carry):
        off = pl.multiple_of(c * ch, SUBLANES)
        xc = xc_scr[pl.ds(off, ch), :]
        xb = xc.astype(BF16)
        tr = jnp.tanh(jnp.dot(xb, wr, preferred_element_type=F32) + br)
        ti = jnp.tanh(jnp.dot(xb, wi, preferred_element_type=F32) + bi)
        log_a = half_unit * tr + half_unit
        a = jnp.exp(log_a)
        a_scr[pl.ds(off, ch), :] = a
        y = -jnp.tanh(log_a) * (a * a + 1.0)
        mult = jnp.where(y > 0.0, y * lax.rsqrt(y), 0.0)
        b_scr[pl.ds(off, ch), :] = (mult * xc) * (0.5 * ti + 0.5)
        return carry

    lax.fori_loop(0, t // ch, body, 0, unroll=min(4, t // ch))


def _lru_scan(a_ref, b_ref, t, d, h0, out_ref, accumulate=False):
    groups = t // SUBLANES
    row = lax.broadcasted_iota(jnp.int32, (SUBLANES, LANES), 0)

    def body(g, carry):
        gi = g if d == 0 else groups - 1 - g
        off = pl.multiple_of(gi * SUBLANES, SUBLANES)
        a = a_ref[pl.ds(off, SUBLANES), :]
        bv = b_ref[pl.ds(off, SUBLANES), :]
        for k in (1, 2, 4):
            shift = k if d == 0 else SUBLANES - k
            valid = (row >= k) if d == 0 else (row < SUBLANES - k)
            a_prev = pltpu.roll(a, shift, 0)
            b_prev = pltpu.roll(bv, shift, 0)
            bv = jnp.where(valid, a * b_prev + bv, bv)
            a = jnp.where(valid, a * a_prev, a)
        h = a * carry + bv
        if out_ref is not None:
            if accumulate:
                out_ref[pl.ds(off, SUBLANES), :] += h
            else:
                out_ref[pl.ds(off, SUBLANES), :] = h
        return h[SUBLANES - 1:SUBLANES, :] if d == 0 else h[0:1, :]

    return lax.fori_loop(0, groups, body, h0, unroll=min(4, groups))


def _lru_scan_planes(a_ref, b_ref, n, d, h0, out_ref, accumulate, levels):
    if n <= LRU_SEQ_SCAN_MAX:
        _lru_scan(a_ref, b_ref, n, d, h0, out_ref, accumulate)
        return
    g = n // SUBLANES
    p_scr, q_scr, s_scr = levels[0]
    order = list(range(SUBLANES)) if d == 0 else list(range(SUBLANES - 1, -1, -1))
    group_rows = SUBLANES * SUBLANES

    def up(v, carry):
        base = pl.multiple_of(v * group_rows, group_rows)
        rows = pl.ds(pl.multiple_of(v * SUBLANES, SUBLANES), SUBLANES)
        p = h = None
        for j in order:
            a = a_ref[pl.ds(base + j, SUBLANES, stride=SUBLANES), :]
            bv = b_ref[pl.ds(base + j, SUBLANES, stride=SUBLANES), :]
            if p is None:
                p, h = a, bv
            else:
                h = a * h + bv
                p = a * p
            p_scr[j, rows, :] = p
            q_scr[j, rows, :] = h
        return carry

    lax.fori_loop(0, g // SUBLANES, up, 0, unroll=4)

    last = order[-1]
    _lru_scan_planes(p_scr.at[last], q_scr.at[last], g, d, h0, s_scr, False, levels[1:])
    s = s_scr[...]
    row = lax.broadcasted_iota(jnp.int32, (g, LANES), 0)
    if d == 0:
        s_scr[...] = jnp.where(row == 0, h0, pltpu.roll(s, 1, 0))
    else:
        s_scr[...] = jnp.where(row == g - 1, h0, pltpu.roll(s, g - 1, 0))

    def down(v, carry):
        base = pl.multiple_of(v * group_rows, group_rows)
        rows = pl.ds(pl.multiple_of(v * SUBLANES, SUBLANES), SUBLANES)
        x = s_scr[rows, :]
        for j in range(SUBLANES):
            dst = pl.ds(base + j, SUBLANES, stride=SUBLANES)
            val = q_scr[j, rows, :] + p_scr[j, rows, :] * x
            if accumulate:
                val = val + out_ref[dst, :]
            out_ref[dst, :] = val
        return carry

    lax.fori_loop(0, g // SUBLANES, down, 0, unroll=4)


def _lru_body(lx_ref, lg_ref, lxc_ref, cw_ref, cb_ref, wr_ref, br_ref, wi_ref, bi_ref, lam_ref, o_ref,
              pad_scr, xc_scr, xcc_scr, a_scr, b_scr, h_scr, *level_scr):
    levels = [level_scr[i:i + 3] for i in range(0, len(level_scr), 3)]
    t = lx_ref.shape[1]
    tc = lxc_ref.shape[1]
    _lru_conv(lxc_ref, cw_ref, cb_ref, pad_scr, xcc_scr, tc)
    _lru_conv(lx_ref, cw_ref, cb_ref, pad_scr, xc_scr, t)
    gate_refs = (wr_ref, br_ref, wi_ref, bi_ref, lam_ref)
    for d in (0, 1):
        _lru_gates(xcc_scr, tc, d, *gate_refs, a_scr, b_scr)
        h0 = _lru_scan(a_scr, b_scr, tc, d, jnp.zeros((1, LANES), F32), None)
        _lru_gates(xc_scr, t, d, *gate_refs, a_scr, b_scr)
        _lru_scan_planes(a_scr, b_scr, t, d, h0, h_scr, d == 1, levels)
    o_ref[0] = (jax.nn.gelu(lg_ref[0]) * h_scr[...]).astype(o_ref.dtype)


def _lru_call(lx, lg, lxc, conv_w, conv_b, w_r, b_r, w_i, b_i, lam):
    b, t, d_lru = lx.shape
    tc = lxc.shape[1]
    nb = d_lru // LANES
    assert w_r.shape == (2, nb, LANES, LANES)
    seq = lambda n: pl.BlockSpec((1, n, LANES), lambda bi, j: (bi, 0, j))
    vec = lambda n: pl.BlockSpec((n, LANES), lambda bi, j: (0, j))
    wblk = pl.BlockSpec((2, 1, LANES, LANES), lambda bi, j: (0, j, 0, 0))
    level_scr = []
    n = t
    while n > LRU_SEQ_SCAN_MAX:
        assert n % (SUBLANES * SUBLANES) == 0
        n //= SUBLANES
        level_scr += [pltpu.VMEM((SUBLANES, n, LANES), F32), pltpu.VMEM((SUBLANES, n, LANES), F32),
                      pltpu.VMEM((n, LANES), F32)]
    return pl.pallas_call(
        _lru_body,
        grid=(b, nb),
        in_specs=[seq(t), seq(t), seq(tc), vec(CONV_W), vec(1), wblk, vec(2), wblk, vec(2), vec(2)],
        out_specs=seq(t),
        out_shape=jax.ShapeDtypeStruct((b, t, d_lru), BF16),
        scratch_shapes=[pltpu.VMEM((t + 2 * SUBLANES, LANES), F32),
                        pltpu.VMEM((t, LANES), F32), pltpu.VMEM((tc, LANES), F32),
                        pltpu.VMEM((t, LANES), F32), pltpu.VMEM((t, LANES), F32),
                        pltpu.VMEM((t, LANES), F32)] + level_scr,
        compiler_params=_cparams(("parallel", "parallel")),
        name="rglru",
    )(lx, lg, lxc, conv_w, conv_b, w_r, b_r, w_i, b_i, lam)


FWD_GATE_LANES = 2 * M_HEADS


def _split3_dot(tri, x):
    hi = x.astype(BF16)
    r1 = x - hi.astype(F32)
    mid = r1.astype(BF16)
    lo = (r1 - mid.astype(F32)).astype(BF16)
    return (jnp.dot(tri, hi, preferred_element_type=F32) + jnp.dot(tri, mid, preferred_element_type=F32)
            + jnp.dot(tri, lo, preferred_element_type=F32))


def _cummax_rows(x, row, reverse):
    n = x.shape[0]
    k = 1
    while k < n:
        if reverse:
            x = jnp.where(row < n - k, jnp.maximum(x, pltpu.roll(x, n - k, 0)), x)
        else:
            x = jnp.where(row >= k, jnp.maximum(x, pltpu.roll(x, k, 0)), x)
        k *= 2
    return x


def _round_up_bf16(x):
    return (x + jnp.abs(x) * (2.0 ** -BF16_MANTISSA_BITS)).astype(BF16)


def _mlstm_gate_prep(gates, kidx, bc_scr, cm_scr, xt_scr, tot_scr, gmax_scr):
    L = gates.shape[0]
    ti = lax.broadcasted_iota(jnp.int32, (L, L), 0)
    si = lax.broadcasted_iota(jnp.int32, (L, L), 1)
    row = lax.broadcasted_iota(jnp.int32, (L, LANES), 0)
    fwd = lax.broadcasted_iota(jnp.int32, (L, LANES), 1) < FWD_GATE_LANES
    lf = jax.nn.log_sigmoid(gates)
    bc_f = _split3_dot((si <= ti).astype(BF16), lf)
    bc_b = _split3_dot((si >= ti).astype(BF16), lf)
    bc = jnp.where(fwd, bc_f, bc_b)
    tot = jnp.where(fwd[0:1], bc_f[L - 1:L], bc_b[0:1])
    x = pltpu.roll(gates, SUBLANES, 1) - bc
    cm = jnp.where(fwd, _cummax_rows(x, row, False), _cummax_rows(x, row, True))
    gmax = jnp.max(tot + x, axis=0, keepdims=True)
    xt = x.T
    bc_scr[kidx] = bc
    cm_scr[kidx] = _round_up_bf16(cm)
    xt_scr[kidx, 0:SUBLANES, :] = xt[SUBLANES:2 * SUBLANES]
    xt_scr[kidx, SUBLANES:2 * SUBLANES, :] = xt[3 * SUBLANES:4 * SUBLANES]
    tot_scr[kidx] = jnp.broadcast_to(tot, (SUBLANES, LANES))
    gmax_scr[kidx] = jnp.broadcast_to(gmax, (SUBLANES, LANES))


def _mlstm_chunk(q, v, kidx, step, d, scr, need_h, dk):
    bc_scr, cm_scr, xt_scr, kt_scr, tot_scr, m_in_scr, m_out_scr, s_scr = scr
    L = q.shape[0]
    ti = lax.broadcasted_iota(jnp.int32, (L, L), 0)
    si = lax.broadcasted_iota(jnp.int32, (L, L), 1)
    causal = (si <= ti) if d == 0 else (si >= ti)
    lane = lax.broadcasted_iota(jnp.int32, (L, LANES), 1)
    head0_rows = lax.broadcasted_iota(jnp.int32, (LANES, L), 0) < dk
    srow = lax.broadcasted_iota(jnp.int32, (LANES, 2 * LANES), 0)
    k_t = kt_scr[kidx]
    tot_v, m_in_v, m_out_v = tot_scr[kidx], m_in_scr[step], m_out_scr[step]
    ones = jnp.ones((L, LANES), BF16)
    v1 = jnp.concatenate([v[:, :LANES], ones, v[:, LANES:], ones], axis=1)
    cf0 = FWD_GATE_LANES * d + SUBLANES
    pick = lambda vals, e: vals[0:1, cf0 + e:cf0 + e + 1]
    x_rows = [xt_scr[kidx, SUBLANES * d + e:SUBLANES * d + e + 1, :] for e in (0, 1)]
    s_old = s_scr[d]
    hs = []
    if need_h:
        bc = bc_scr[kidx]
        sel_lane = lax.broadcasted_iota(jnp.int32, (LANES, 2 * LANES), 0)
        sel_col = lax.broadcasted_iota(jnp.int32, (LANES, 2 * LANES), 1)
        sel = (sel_lane == jnp.where(sel_col < LANES, cf0, cf0 + 1)).astype(BF16)
        mx_all = jnp.maximum(_round_up_bf16(m_in_v[0:1, :]), cm_scr[kidx])
        mx_tiles = jnp.dot(mx_all, sel, preferred_element_type=F32)
        q_heads = [jnp.where((lane >= e * dk) & (lane < (e + 1) * dk), q, jnp.zeros_like(q)) for e in (0, 1)]
        q2 = jnp.concatenate(q_heads, axis=0)
        qk = jnp.dot(q2, k_t, preferred_element_type=F32)
        qs = jnp.dot(q2, s_old.astype(BF16), preferred_element_type=F32)
        for e in (0, 1):
            cf = cf0 + e
            mx = mx_tiles[:, e * LANES:(e + 1) * LANES]
            w = qk[e * L:(e + 1) * L] * jnp.exp(jnp.where(causal, x_rows[e] - mx, -jnp.inf))
            s_inter = jnp.exp(pick(m_in_v, e) - mx)
            wv = jnp.dot(w.astype(BF16), v1[:, 2 * e * LANES:2 * (e + 1) * LANES],
                         preferred_element_type=F32)
            qs_e = qs[e * L:(e + 1) * L]
            num = s_inter * qs_e[:, :LANES] + wv[:, :LANES]
            den = s_inter * qs_e[:, LANES:] + wv[:, LANES:]
            m_row = bc[:, cf:cf + 1] + mx[:, cf:cf + 1]
            hs.append(num * (1.0 / jnp.maximum(jnp.abs(den[:, cf:cf + 1]), jnp.exp(-m_row))))
    wg_rows = [jnp.exp(pick(tot_v, e) + x_rows[e] - pick(m_out_v, e)) for e in (0, 1)]
    decays = [jnp.exp(pick(tot_v, e) + pick(m_in_v, e) - pick(m_out_v, e)) for e in (0, 1)]
    kw = (k_t.astype(F32) * jnp.where(head0_rows, wg_rows[0], wg_rows[1])).astype(BF16)
    own = jnp.concatenate([jnp.dot(kw[:dk], v1[:, :2 * LANES], preferred_element_type=F32),
                           jnp.dot(kw[dk:], v1[:, 2 * LANES:], preferred_element_type=F32)], axis=0)
    s_scr[d] = jnp.where(srow < dk, decays[0], decays[1]) * s_old + own
    return hs


def _mlstm_body(q_ref, k_ref, v_ref, o_ref, gt_ref, qc_ref, kc_ref, vc_ref, gtc_ref, bias_ref, gain_ref,
                out_ref, hs_scr, bc_scr, cm_scr, xt_scr, kt_scr, tot_scr, gmax_scr, m_in_scr, m_out_scr,
                s_scr, *, dk):
    t = q_ref.shape[1]
    tc = qc_ref.shape[1]
    L = CHUNK
    nc, ncc = t // L, tc // L
    n_steps = nc + ncc
    shift = (LANES - 2 * pl.program_id(1)) % LANES
    bias = pltpu.roll(jnp.broadcast_to(bias_ref[...], (SUBLANES, LANES)), shift, 1)[0:1, :]
    prep_scr = (bc_scr, cm_scr, xt_scr, tot_scr, gmax_scr)
    state_scr = (bc_scr, cm_scr, xt_scr, kt_scr, tot_scr, m_in_scr, m_out_scr, s_scr)
    scaled_t = lambda k: (k.astype(F32).T * (dk ** -0.5)).astype(BF16)

    for c in range(ncc):
        rows = slice(c * L, (c + 1) * L)
        _mlstm_gate_prep(pltpu.roll(gtc_ref[0, rows, :], shift, 1) + bias, c, *prep_scr)
        kt_scr[c] = scaled_t(kc_ref[0, rows, :])

    def prep(c, carry):
        rows = pl.ds(pl.multiple_of(c * L, L), L)
        _mlstm_gate_prep(pltpu.roll(gt_ref[0, rows, :], shift, 1) + bias, c + ncc, *prep_scr)
        kt_scr[c + ncc] = scaled_t(k_ref[0, rows, :])
        return carry

    lax.fori_loop(0, nc, prep, 0, unroll=8)

    fwd_id = lambda i: i
    bwd_id = lambda i: (ncc - 1 - i) if i < ncc else (n_steps - 1 - (i - ncc))
    fwd8 = lax.broadcasted_iota(jnp.int32, (SUBLANES, LANES), 1) < FWD_GATE_LANES
    m = jnp.zeros((SUBLANES, LANES), F32)
    for i in range(n_steps):
        tot = jnp.where(fwd8, tot_scr[fwd_id(i)], tot_scr[bwd_id(i)])
        gmax = jnp.where(fwd8, gmax_scr[fwd_id(i)], gmax_scr[bwd_id(i)])
        m_in_scr[i] = m
        m = jnp.maximum(tot + m, gmax)
        m_out_scr[i] = m

    s_scr[...] = jnp.zeros_like(s_scr)
    for i in range(ncc):
        for d, cid in ((0, fwd_id(i)), (1, bwd_id(i))):
            rows = slice(cid * L, (cid + 1) * L)
            _mlstm_chunk(qc_ref[0, rows, :], vc_ref[0, rows, :], cid, i, d, state_scr, False, dk)

    def body(j, carry):
        for d in (0, 1):
            cj = j if d == 0 else nc - 1 - j
            rows = pl.ds(pl.multiple_of(cj * L, L), L)
            hs = _mlstm_chunk(q_ref[0, rows, :], v_ref[0, rows, :], cj + ncc, j + ncc, d, state_scr, True, dk)
            hs_scr[d, rows, :] = jnp.concatenate(hs, axis=1)
        return carry

    lax.fori_loop(0, nc, body, 0, unroll=8)

    def fin(c, carry):
        rows = pl.ds(pl.multiple_of(c * L, L), L)
        hh = hs_scr[0, rows, :] + hs_scr[1, rows, :]
        outs = []
        for e in (0, 1):
            x = hh[:, e * LANES:(e + 1) * LANES]
            outs.append(x * lax.rsqrt(jnp.mean(x * x, axis=-1, keepdims=True) + EPS))
        y = jnp.concatenate(outs, axis=1) * gain_ref[...] * _sigmoid(o_ref[0, rows, :])
        out_ref[0, rows, :] = y.astype(out_ref.dtype)
        return carry

    lax.fori_loop(0, nc, fin, 0, unroll=4)


def _mlstm_call(q, k, v, o, gt, qc, kc, vc, gtc, bias, gain):
    b, t, d_qk = q.shape
    tc = qc.shape[1]
    d_v = v.shape[2]
    pairs = M_HEADS // 2
    dk = d_qk // M_HEADS
    assert d_qk // pairs == LANES and d_v // pairs == 2 * LANES and CHUNK == LANES
    n_chunks = (t + tc) // CHUNK
    seq = lambda n, w: pl.BlockSpec((1, n, w), lambda bi, p: (bi, 0, p))
    allg = lambda n: pl.BlockSpec((1, n, LANES), lambda bi, p: (bi, 0, 0))
    chunk_f32 = lambda rows, cols: pltpu.VMEM((n_chunks, rows, cols), F32)
    return pl.pallas_call(
        functools.partial(_mlstm_body, dk=dk),
        grid=(b, pairs),
        in_specs=[seq(t, LANES), seq(t, LANES), seq(t, 2 * LANES), seq(t, 2 * LANES), allg(t),
                  seq(tc, LANES), seq(tc, LANES), seq(tc, 2 * LANES), allg(tc),
                  pl.BlockSpec((1, LANES), lambda bi, p: (0, 0)),
                  pl.BlockSpec((1, 2 * LANES), lambda bi, p: (0, p))],
        out_specs=seq(t, 2 * LANES),
        out_shape=jax.ShapeDtypeStruct((b, t, d_v), F32),
        scratch_shapes=[pltpu.VMEM((2, t, 2 * LANES), F32),
                        chunk_f32(CHUNK, LANES), pltpu.VMEM((n_chunks, CHUNK, LANES), BF16),
                        chunk_f32(2 * SUBLANES, CHUNK),
                        pltpu.VMEM((n_chunks, LANES, CHUNK), BF16),
                        chunk_f32(SUBLANES, LANES), chunk_f32(SUBLANES, LANES),
                        chunk_f32(SUBLANES, LANES), chunk_f32(SUBLANES, LANES),
                        pltpu.VMEM((2, LANES, 2 * LANES), F32)],
        compiler_params=_cparams(("parallel", "parallel")),
        name="mlstm",
    )(q, k, v, o, gt, qc, kc, vc, gtc, bias, gain)


def _outproj_body(x_ref, lru_ref, mls_ref, wa_ref, wb_ref, g_ref, o_ref):
    w_cols, r, dm = mls_ref.shape[1:]
    m = jnp.swapaxes(mls_ref[0], 0, 1).reshape(w_cols * r, dm).astype(BF16)
    y = (jnp.dot(lru_ref[0], wa_ref[...], preferred_element_type=F32)
         + jnp.dot(m, wb_ref[...], preferred_element_type=F32))
    o_ref[0] = x_ref[0] + g_ref[0] * y


def _outproj_call(x3d, lru, mls_cm, w_a, w_b, gate):
    b, t, d = x3d.shape
    da, dm = lru.shape[2], mls_cm.shape[2]
    rows = t // GRID_W
    r_tile = PROJ_TM // GRID_W
    assert r_tile == SUBLANES
    mls_view = mls_cm.reshape(b, GRID_W, rows, dm)
    return pl.pallas_call(
        _outproj_body,
        grid=(b, t // PROJ_TM),
        in_specs=[pl.BlockSpec((1, PROJ_TM, d), lambda bi, i: (bi, i, 0)),
                  pl.BlockSpec((1, PROJ_TM, da), lambda bi, i: (bi, i, 0)),
                  pl.BlockSpec((1, GRID_W, r_tile, dm), lambda bi, i: (bi, 0, i, 0)),
                  pl.BlockSpec((da, d), lambda bi, i: (0, 0)),
                  pl.BlockSpec((dm, d), lambda bi, i: (0, 0)),
                  pl.BlockSpec((1, 1, d), lambda bi, i: (bi, 0, 0))],
        out_specs=pl.BlockSpec((1, PROJ_TM, d), lambda bi, i: (bi, i, 0)),
        out_shape=jax.ShapeDtypeStruct((b, t, d), F32),
        compiler_params=_cparams(("parallel", "parallel")),
        name="outproj",
    )(x3d, lru, mls_view, w_a, w_b, gate)


def kernel(x, c, ctx, c_ctx, ada_w, ada_b, ffn1_norm, ffn1_w_up, ffn1_w_down, mix_norm, w_in, b_mgate, lru_conv_w, lru_conv_b, lru_w_r, lru_b_r, lru_w_i, lru_b_i, lru_lam, mlstm_norm, w_out, ffn2_norm, ffn2_w_up, ffn2_w_down, final_norm):
    b, t, d = x.shape
    tc = ctx.shape[1]
    assert ada_w.shape[0] == 1, "single-layer block only"
    assert t % GRID_W == 0 and t % PROJ_TM == 0
    rows = t // GRID_W
    d_lru = lru_conv_w.shape[2]
    d_mv = mlstm_norm.shape[1]
    d_mqk = (w_in.shape[2] - 2 * d_lru - 2 * d_mv - 4 * M_HEADS) // 2

    pad = SUBLANES - b - 1
    cc = jnp.concatenate([c, c_ctx[None, :], jnp.zeros((pad, d), F32)], axis=0)
    mod = _ada_call(cc, ada_w[0], ada_b[0][None, :]).reshape(SUBLANES, N_MOD, d)
    lat = lambda i: mod[:b, i][:, None, :]
    cxt = lambda i: mod[b:b + 1, i][:, None, :]
    row = lambda v: v[0][None, :]

    x1, ctx1 = _ffn_call(x.reshape(b * t, d), (lat(0), lat(1), lat(2)), row(ffn1_norm), ffn1_w_up[0],
                         ffn1_w_down[0], row(ffn1_norm), rows_per_mod=t, final_norm=False, name="ffn_pre",
                         extra=(ctx.reshape(b * tc, d), (cxt(0), cxt(1), cxt(2))))
    x1 = x1.reshape(b, t, d)
    ctx1 = ctx1.reshape(b, tc, d)

    n_gate = 4 * M_HEADS
    w_a, w_b = _split_w_in(w_in[0].T, 2 * d_lru)
    group_a = (w_a, w_a.shape[1], 0, (d_lru, d_lru), (F32, F32))
    group_b = (w_b, w_b.shape[1], 0, (d_mqk, d_mqk, d_mv, d_mv, LANES), (BF16, BF16, BF16, F32, F32))
    mixn = row(mix_norm)
    lx, lg = _proj_call(x1, mixn, lat(3), lat(4), [group_a], tm=PROJ_TM, colmajor=False, name="proj_lru")
    q, k, v, o, gt = _proj_call(x1.reshape(b, rows, GRID_W, d), mixn, lat(3), lat(4), [group_b],
                                tm=PROJ_TM, colmajor=True, name="proj_mlstm")
    d_qkv = 2 * d_mqk + d_mv
    gate_col = d_qkv + d_mv
    assert gate_col % LANES == 0
    ctx_groups = [(w_a, d_lru, 0, (d_lru,), (F32,)),
                  (w_b, d_qkv, 0, (d_mqk, d_mqk, d_mv), (BF16, BF16, BF16)),
                  (w_b, LANES, gate_col // LANES, (LANES,), (F32,))]
    ctx_mod = lambda i: jnp.broadcast_to(cxt(i), (b, 1, d))
    lxc, qc, kc, vc, gtc = _proj_call(ctx1, mixn, ctx_mod(3), ctx_mod(4), ctx_groups, tm=tc, colmajor=False,
                                      name="proj_ctx")

    lru_lat = _lru_call(lx, lg, lxc, lru_conv_w[0], row(lru_conv_b), lru_w_r[0], lru_b_r[0], lru_w_i[0],
                        lru_b_i[0], lru_lam[0])
    bias = jnp.concatenate([b_mgate[0], jnp.zeros((LANES - n_gate,), F32)])[None, :]
    mls_cm = _mlstm_call(q, k, v, o, gt, qc, kc, vc, gtc, bias, row(mlstm_norm))

    w_out0 = w_out[0].astype(BF16)
    x2 = _outproj_call(x1, lru_lat, mls_cm, w_out0[:d_lru], w_out0[d_lru:], lat(5))

    out, = _ffn_call(x2.reshape(b * t, d), (lat(6), lat(7), lat(8)), row(ffn2_norm), ffn2_w_up[0],
                     ffn2_w_down[0], final_norm[None, :], rows_per_mod=t, final_norm=True, name="ffn_post")
    return out.reshape(b, t, d)
```

```python
import functools

import jax
import jax.numpy as jnp
from jax import lax
from jax.experimental import pallas as pl
from jax.experimental.pallas import tpu as pltpu

F32 = jnp.float32
BF16 = jnp.bfloat16

GRID_W = 64
LRU_BLOCKS = 8
CONV_W = 4
LRU_C = 8.0
M_HEADS = 8
CHUNK = 128
N_MOD = 9
EPS = 1e-6
HALF = 0.5

LANES = 128
SUBLANES = 8
BF16_MANTISSA_BITS = 7
VMEM_LIMIT_BYTES = 58 * 1024 * 1024

FFN_TM = 1024
FFN_TF = 256
FFN_EDGE_ROWS = 512
PROJ_ROWS = 256
PROJ_TM = 512
LRU_GATE_ROWS = 512
LRU_SEQ_SCAN_MAX = 64
LRU_SUB_BLOCKS = 2
ADA_TN = 2048
WSPLIT_TN = 512


def _cparams(sem):
    return pltpu.CompilerParams(dimension_semantics=sem, vmem_limit_bytes=VMEM_LIMIT_BYTES)


def _sigmoid(z):
    return 0.5 * jnp.tanh(0.5 * z) + 0.5


def _rms(x, g):
    return x * lax.rsqrt(jnp.mean(x * x, axis=-1, keepdims=True) + EPS) * g


def _ada_body(c_ref, w_ref, b_ref, o_ref):
    s = c_ref[...]
    s = s * jax.nn.sigmoid(s)
    o_ref[...] = jnp.dot(s, w_ref[...], preferred_element_type=F32) + b_ref[...]


def _ada_call(cc, w, b):
    rows, d = cc.shape
    n = w.shape[1]
    return pl.pallas_call(
        _ada_body,
        grid=(n // ADA_TN,),
        in_specs=[pl.BlockSpec((rows, d), lambda j: (0, 0)),
                  pl.BlockSpec((d, ADA_TN), lambda j: (0, j)),
                  pl.BlockSpec((1, ADA_TN), lambda j: (0, j))],
        out_specs=pl.BlockSpec((rows, ADA_TN), lambda j: (0, j)),
        out_shape=jax.ShapeDtypeStruct((rows, n), F32),
        compiler_params=_cparams(("arbitrary",)),
        name="ada_mod",
    )(cc, w, b)


def _ffn_body(*refs, final_norm, n_groups):
    groups_in = [refs[4 * k:4 * k + 4] for k in range(n_groups)]
    g_ref, wg_ref, wu_ref, wd_ref, fin_ref = refs[4 * n_groups:4 * n_groups + 5]
    out_refs = refs[4 * n_groups + 5:5 * n_groups + 5]
    h_scr = refs[-1]
    j = pl.program_id(1)
    last = pl.num_programs(1) - 1
    groups, off = [], 0
    for (x_ref, sh_ref, sc_ref, gate_ref), o_ref in zip(groups_in, out_refs):
        n = x_ref.shape[0]
        blocks = lambda rb: [slice(r0, r0 + min(rb, n)) for r0 in range(0, n, min(rb, n))]
        groups.append((x_ref, sh_ref, sc_ref, gate_ref, o_ref, off, blocks(FFN_EDGE_ROWS),
                       blocks(FFN_EDGE_ROWS // 2 if final_norm else FFN_EDGE_ROWS)))
        off += n

    def contrib(h):
        g = jnp.dot(h, wg_ref[...].astype(BF16), preferred_element_type=F32)
        u = jnp.dot(h, wu_ref[...].astype(BF16), preferred_element_type=F32)
        a = (g * jax.nn.sigmoid(g) * u).astype(BF16)
        return jnp.dot(a, wd_ref[...].astype(BF16), preferred_element_type=F32)

    @pl.when(j == 0)
    def _():
        for x_ref, sh_ref, sc_ref, _, o_ref, off, blocks, _ in groups:
            for rows in blocks:
                h = (_rms(x_ref[rows, :], g_ref[...]) * (1.0 + sc_ref[0]) + sh_ref[0]).astype(BF16)
                h_scr[off + rows.start:off + rows.stop, :] = h
                o_ref[rows, :] = contrib(h)

    @pl.when((j > 0) & (j < last))
    def _():
        res = contrib(h_scr[...])
        for _, _, _, _, o_ref, off, _, _ in groups:
            o_ref[...] += res[off:off + o_ref.shape[0]]

    @pl.when(j == last)
    def _():
        for x_ref, _, _, gate_ref, o_ref, off, _, blocks in groups:
            for rows in blocks:
                acc = o_ref[rows, :] + contrib(h_scr[off + rows.start:off + rows.stop, :])
                y = x_ref[rows, :] + HALF * gate_ref[0] * acc
                if final_norm:
                    y = _rms(y, fin_ref[...])
                o_ref[rows, :] = y


def _ffn_call(x2d, mods, norm_g, w_up, w_down, fin_g, *, rows_per_mod, final_norm, name, extra=None):
    m, d = x2d.shape
    f = w_down.shape[0]
    tm = min(FFN_TM, m)
    n_tiles = m // tm
    tiles_per_mod = rows_per_mod // tm
    nf = f // FFN_TF
    vec_spec = pl.BlockSpec((1, d), lambda i, j: (0, 0))
    row_spec = lambda rows: pl.BlockSpec((rows, d), lambda i, j: (i, 0))
    mod_spec = pl.BlockSpec((1, 1, d), lambda i, j: (i // tiles_per_mod, 0, 0))
    const_mod_spec = pl.BlockSpec((1, 1, d), lambda i, j: (0, 0, 0))
    in_specs = [row_spec(tm), mod_spec, mod_spec, mod_spec]
    args = [x2d, *mods]
    out_rows = [tm]
    if extra is not None:
        x_e, mods_e = extra
        te = x_e.shape[0] // n_tiles
        assert te * n_tiles == x_e.shape[0] and te % (2 * SUBLANES) == 0
        in_specs += [row_spec(te), const_mod_spec, const_mod_spec, const_mod_spec]
        args += [x_e, *mods_e]
        out_rows.append(te)
    in_specs += [vec_spec,
                 pl.BlockSpec((d, FFN_TF), lambda i, j: (0, j)),
                 pl.BlockSpec((d, FFN_TF), lambda i, j: (0, j + nf)),
                 pl.BlockSpec((FFN_TF, d), lambda i, j: (j, 0)),
                 vec_spec]
    args += [norm_g, w_up, w_up, w_down, fin_g]
    return pl.pallas_call(
        functools.partial(_ffn_body, final_norm=final_norm, n_groups=len(out_rows)),
        grid=(n_tiles, nf),
        in_specs=in_specs,
        out_specs=[row_spec(r) for r in out_rows],
        out_shape=[jax.ShapeDtypeStruct((r * n_tiles, d), F32) for r in out_rows],
        scratch_shapes=[pltpu.VMEM((sum(out_rows), d), BF16)],
        compiler_params=_cparams(("parallel", "arbitrary")),
        name=name,
    )(*args)


def _wsplit_body(wt_ref, a_ref, b_ref, *, n_valid, n_a_blocks):
    j = pl.program_id(0)
    feat = j * WSPLIT_TN + lax.broadcasted_iota(jnp.int32, wt_ref.shape, 0)
    w = jnp.where(feat < n_valid, wt_ref[...], 0.0).T.astype(BF16)

    @pl.when(j < n_a_blocks)
    def _():
        a_ref[...] = w

    @pl.when(j >= n_a_blocks)
    def _():
        b_ref[...] = w


def _split_w_in(wt, n_a):
    n, d = wt.shape
    assert n_a % WSPLIT_TN == 0
    nb = pl.cdiv(n, WSPLIT_TN)
    na = n_a // WSPLIT_TN
    return pl.pallas_call(
        functools.partial(_wsplit_body, n_valid=n, n_a_blocks=na),
        grid=(nb,),
        in_specs=[pl.BlockSpec((WSPLIT_TN, d), lambda j: (j, 0))],
        out_specs=[pl.BlockSpec((d, WSPLIT_TN), lambda j: (0, jnp.minimum(j, na - 1))),
                   pl.BlockSpec((d, WSPLIT_TN), lambda j: (0, jnp.maximum(j - na, 0)))],
        out_shape=[jax.ShapeDtypeStruct((d, n_a), BF16),
                   jax.ShapeDtypeStruct((d, (nb - na) * WSPLIT_TN), BF16)],
        compiler_params=_cparams(("arbitrary",)),
        name="w_in_split",
    )(wt)


def _proj_body(x_ref, g_ref, sh_ref, sc_ref, *refs, group_widths, colmajor):
    w_refs, out_refs = refs[:len(group_widths)], refs[len(group_widths):]
    tm = out_refs[0].shape[1]
    rb = min(PROJ_ROWS, tm)
    norm = lambda x: (_rms(x, g_ref[...]) * (1.0 + sc_ref[0]) + sh_ref[0]).astype(BF16)
    if colmajor:
        r, c, d = x_ref.shape[1:]
        assert rb % r == 0
        xs = jnp.swapaxes(x_ref[0], 0, 1)
    for r0 in range(0, tm, rb):
        rows = slice(r0, r0 + rb)
        if colmajor:
            h = norm(xs[r0 // r:(r0 + rb) // r].reshape(rb, d))
        else:
            h = norm(x_ref[0, rows, :])
        outs = iter(out_refs)
        for w_ref, widths in zip(w_refs, group_widths):
            off = 0
            for wdt in widths:
                o_ref = next(outs)
                o_ref[0, rows, :] = jnp.dot(h, w_ref[:, off:off + wdt],
                                            preferred_element_type=F32).astype(o_ref.dtype)
                off += wdt


def _proj_call(x, norm_g, shift, scale, groups, *, tm, colmajor, name):
    b = x.shape[0]
    d = norm_g.shape[1]
    if colmajor:
        rows, gw = x.shape[1:3]
        cols = tm // rows
        assert cols == SUBLANES and gw % cols == 0
        n_tiles = gw // cols
        x_spec = pl.BlockSpec((1, rows, cols, d), lambda bi, i: (bi, 0, i, 0))
    else:
        n_tiles = x.shape[1] // tm
        x_spec = pl.BlockSpec((1, tm, d), lambda bi, i: (bi, i, 0))
    mod_spec = pl.BlockSpec((1, 1, d), lambda bi, i: (bi, 0, 0))
    w_specs = [pl.BlockSpec((d, cols_), functools.partial(lambda bi, i, idx: (0, idx), idx=idx))
               for _, cols_, idx, _, _ in groups]
    assert all(sum(widths) <= cols_ for _, cols_, _, widths, _ in groups)
    outs = [(wdt, dt) for _, _, _, widths, dtypes in groups for wdt, dt in zip(widths, dtypes)]
    return pl.pallas_call(
        functools.partial(_proj_body, group_widths=tuple(tuple(g[3]) for g in groups), colmajor=colmajor),
        grid=(b, n_tiles),
        in_specs=[x_spec, pl.BlockSpec((1, d), lambda bi, i: (0, 0)), mod_spec, mod_spec] + w_specs,
        out_specs=[pl.BlockSpec((1, tm, wdt), lambda bi, i: (bi, i, 0)) for wdt, _ in outs],
        out_shape=[jax.ShapeDtypeStruct((b, n_tiles * tm, wdt), dt) for wdt, dt in outs],
        compiler_params=_cparams(("parallel", "parallel")),
        name=name,
    )(x, norm_g, shift, scale, *[g[0] for g in groups])


def _lru_conv(x_ref, ls, cw_ref, cb_ref, pad_scr, xc_scr, t):
    zeros = jnp.zeros((SUBLANES, LANES), F32)
    pad_scr[0:SUBLANES, :] = zeros
    pad_scr[SUBLANES:SUBLANES + t, :] = x_ref[0, :, ls]
    pad_scr[SUBLANES + t:2 * SUBLANES + t, :] = zeros
    ch = min(LRU_GATE_ROWS, t)
    win_rows = ch + 2 * SUBLANES

    def body(c, carry):
        off = pl.multiple_of(c * ch, SUBLANES)
        win = pad_scr[pl.ds(off, win_rows), :]
        acc = cb_ref[:, ls] + cw_ref[2:3, ls] * win[SUBLANES:SUBLANES + ch]
        for k, shift in ((0, 2), (1, 1), (3, win_rows - 1)):
            acc = acc + cw_ref[k:k + 1, ls] * pltpu.roll(win, shift, 0)[SUBLANES:SUBLANES + ch]
        xc_scr[pl.ds(off, ch), :] = acc
        return carry

    lax.fori_loop(0, t // ch, body, 0)


def _lru_gates(xc_scr, t, d, sub, wr_ref, br_ref, wi_ref, bi_ref, lam_ref, a_scr, b_scr):
    ls = slice(sub * LANES, (sub + 1) * LANES)
    ch = min(LRU_GATE_ROWS, t)
    half_unit = (-0.5 * LRU_C) * jax.nn.softplus(-lam_ref[d:d + 1, ls])
    wr = (0.5 * wr_ref[d, sub]).astype(BF16)
    wi = (0.5 * wi_ref[d, sub]).astype(BF16)
    br = 0.5 * br_ref[d:d + 1, ls]
    bi = 0.5 * bi_ref[d:d + 1, ls]

    def body(c, carry):
        off = pl.multiple_of(c * ch, SUBLANES)
        xc = xc_scr[pl.ds(off, ch), :]
        xb = xc.astype(BF16)
        tr = jnp.tanh(jnp.dot(xb, wr, preferred_element_type=F32) + br)
        ti = jnp.tanh(jnp.dot(xb, wi, preferred_element_type=F32) + bi)
        log_a = half_unit * tr + half_unit
        a = jnp.exp(log_a)
        a_scr[pl.ds(off, ch), :] = a
        y = -jnp.tanh(log_a) * (a * a + 1.0)
        mult = jnp.where(y > 0.0, y * lax.rsqrt(y), 0.0)
        b_scr[pl.ds(off, ch), :] = (mult * xc) * (0.5 * ti + 0.5)
        return carry

    lax.fori_loop(0, t // ch, body, 0, unroll=min(4, t // ch))


def _lru_scan(a_ref, b_ref, t, d, h0, out_ref, accumulate=False):
    groups = t // SUBLANES
    row = lax.broadcasted_iota(jnp.int32, (SUBLANES, LANES), 0)

    def body(g, carry):
        gi = g if d == 0 else groups - 1 - g
        off = pl.multiple_of(gi * SUBLANES, SUBLANES)
        a = a_ref[pl.ds(off, SUBLANES), :]
        bv = b_ref[pl.ds(off, SUBLANES), :]
        for k in (1, 2, 4):
            shift = k if d == 0 else SUBLANES - k
            valid = (row >= k) if d == 0 else (row < SUBLANES - k)
            a_prev = pltpu.roll(a, shift, 0)
            b_prev = pltpu.roll(bv, shift, 0)
            bv = jnp.where(valid, a * b_prev + bv, bv)
            a = jnp.where(valid, a * a_prev, a)
        h = a * carry + bv
        if out_ref is not None:
            if accumulate:
                out_ref[pl.ds(off, SUBLANES), :] += h
            else:
                out_ref[pl.ds(off, SUBLANES), :] = h
        return h[SUBLANES - 1:SUBLANES, :] if d == 0 else h[0:1, :]

    return lax.fori_loop(0, groups, body, h0, unroll=min(4, groups))


def _lru_scan_planes(a_ref, b_ref, n, d, h0, out_ref, accumulate, levels):
    if n <= LRU_SEQ_SCAN_MAX:
        _lru_scan(a_ref, b_ref, n, d, h0, out_ref, accumulate)
        return
    g = n // SUBLANES
    p_scr, q_scr, s_scr = levels[0]
    order = list(range(SUBLANES)) if d == 0 else list(range(SUBLANES - 1, -1, -1))
    group_rows = SUBLANES * SUBLANES

    def up(v, carry):
        base = pl.multiple_of(v * group_rows, group_rows)
        rows = pl.ds(pl.multiple_of(v * SUBLANES, SUBLANES), SUBLANES)
        p = h = None
        for j in order:
            a = a_ref[pl.ds(base + j, SUBLANES, stride=SUBLANES), :]
            bv = b_ref[pl.ds(base + j, SUBLANES, stride=SUBLANES), :]
            if p is None:
                p, h = a, bv
            else:
                h = a * h + bv
                p = a * p
            p_scr[j, rows, :] = p
            q_scr[j, rows, :] = h
        return carry

    lax.fori_loop(0, g // SUBLANES, up, 0, unroll=4)

    last = order[-1]
    _lru_scan_planes(p_scr.at[last], q_scr.at[last], g, d, h0, s_scr, False, levels[1:])
    s = s_scr[...]
    row = lax.broadcasted_iota(jnp.int32, (g, LANES), 0)
    if d == 0:
        s_scr[...] = jnp.where(row == 0, h0, pltpu.roll(s, 1, 0))
    else:
        s_scr[...] = jnp.where(row == g - 1, h0, pltpu.roll(s, g - 1, 0))

    def down(v, carry):
        base = pl.multiple_of(v * group_rows, group_rows)
        rows = pl.ds(pl.multiple_of(v * SUBLANES, SUBLANES), SUBLANES)
        x = s_scr[rows, :]
        for j in range(SUBLANES):
            dst = pl.ds(base + j, SUBLANES, stride=SUBLANES)
            val = q_scr[j, rows, :] + p_scr[j, rows, :] * x
            if accumulate:
                val = val + out_ref[dst, :]
            out_ref[dst, :] = val
        return carry

    lax.fori_loop(0, g // SUBLANES, down, 0, unroll=4)


def _lru_body(lx_ref, lg_ref, lxc_ref, cw_ref, cb_ref, wr_ref, br_ref, wi_ref, bi_ref, lam_ref, o_ref,
              pad_scr, xc_scr, xcc_scr, a_scr, b_scr, h_scr, *level_scr):
    levels = [level_scr[i:i + 3] for i in range(0, len(level_scr), 3)]
    t = lx_ref.shape[1]
    tc = lxc_ref.shape[1]
    gate_refs = (wr_ref, br_ref, wi_ref, bi_ref, lam_ref)
    for sub in range(LRU_SUB_BLOCKS):
        ls = slice(sub * LANES, (sub + 1) * LANES)
        _lru_conv(lxc_ref, ls, cw_ref, cb_ref, pad_scr, xcc_scr, tc)
        _lru_conv(lx_ref, ls, cw_ref, cb_ref, pad_scr, xc_scr, t)
        for d in (0, 1):
            _lru_gates(xcc_scr, tc, d, sub, *gate_refs, a_scr, b_scr)
            h0 = _lru_scan(a_scr, b_scr, tc, d, jnp.zeros((1, LANES), F32), None)
            _lru_gates(xc_scr, t, d, sub, *gate_refs, a_scr, b_scr)
            _lru_scan_planes(a_scr, b_scr, t, d, h0, h_scr, d == 1, levels)
        o_ref[0, :, ls] = (jax.nn.gelu(lg_ref[0, :, ls]) * h_scr[...]).astype(o_ref.dtype)


def _lru_call(lx, lg, lxc, conv_w, conv_b, w_r, b_r, w_i, b_i, lam):
    b, t, d_lru = lx.shape
    tc = lxc.shape[1]
    width = LRU_SUB_BLOCKS * LANES
    nb = d_lru // width
    assert w_r.shape == (2, d_lru // LANES, LANES, LANES) and nb * width == d_lru
    seq = lambda n: pl.BlockSpec((1, n, width), lambda bi, j: (bi, 0, j))
    vec = lambda n: pl.BlockSpec((n, width), lambda bi, j: (0, j))
    wblk = pl.BlockSpec((2, LRU_SUB_BLOCKS, LANES, LANES), lambda bi, j: (0, j, 0, 0))
    level_scr = []
    n = t
    while n > LRU_SEQ_SCAN_MAX:
        assert n % (SUBLANES * SUBLANES) == 0
        n //= SUBLANES
        level_scr += [pltpu.VMEM((SUBLANES, n, LANES), F32), pltpu.VMEM((SUBLANES, n, LANES), F32),
                      pltpu.VMEM((n, LANES), F32)]
    return pl.pallas_call(
        _lru_body,
        grid=(b, nb),
        in_specs=[seq(t), seq(t), seq(tc), vec(CONV_W), vec(1), wblk, vec(2), wblk, vec(2), vec(2)],
        out_specs=seq(t),
        out_shape=jax.ShapeDtypeStruct((b, t, d_lru), BF16),
        scratch_shapes=[pltpu.VMEM((t + 2 * SUBLANES, LANES), F32),
                        pltpu.VMEM((t, LANES), F32), pltpu.VMEM((tc, LANES), F32),
                        pltpu.VMEM((t, LANES), F32), pltpu.VMEM((t, LANES), F32),
                        pltpu.VMEM((t, LANES), F32)] + level_scr,
        compiler_params=_cparams(("parallel", "parallel")),
        name="rglru",
    )(lx, lg, lxc, conv_w, conv_b, w_r, b_r, w_i, b_i, lam)


FWD_GATE_LANES = 2 * M_HEADS


def _split3_dot(tri, x):
    hi = x.astype(BF16)
    r1 = x - hi.astype(F32)
    mid = r1.astype(BF16)
    lo = (r1 - mid.astype(F32)).astype(BF16)
    return (jnp.dot(tri, hi, preferred_element_type=F32) + jnp.dot(tri, mid, preferred_element_type=F32)
            + jnp.dot(tri, lo, preferred_element_type=F32))


def _cummax_rows(x, row, reverse):
    n = x.shape[0]
    k = 1
    while k < n:
        if reverse:
            x = jnp.where(row < n - k, jnp.maximum(x, pltpu.roll(x, n - k, 0)), x)
        else:
            x = jnp.where(row >= k, jnp.maximum(x, pltpu.roll(x, k, 0)), x)
        k *= 2
    return x


def _round_up_bf16(x):
    return (x + jnp.abs(x) * (2.0 ** -BF16_MANTISSA_BITS)).astype(BF16)


def _mlstm_gate_prep(gates, kidx, bc_scr, cm_scr, xt_scr, tot_scr, gmax_scr):
    L = gates.shape[0]
    ti = lax.broadcasted_iota(jnp.int32, (L, L), 0)
    si = lax.broadcasted_iota(jnp.int32, (L, L), 1)
    row = lax.broadcasted_iota(jnp.int32, (L, LANES), 0)
    fwd = lax.broadcasted_iota(jnp.int32, (L, LANES), 1) < FWD_GATE_LANES
    lf = jax.nn.log_sigmoid(gates)
    bc_f = _split3_dot((si <= ti).astype(BF16), lf)
    bc_b = _split3_dot((si >= ti).astype(BF16), lf)
    bc = jnp.where(fwd, bc_f, bc_b)
    tot = jnp.where(fwd[0:1], bc_f[L - 1:L], bc_b[0:1])
    x = pltpu.roll(gates, SUBLANES, 1) - bc
    cm = jnp.where(fwd, _cummax_rows(x, row, False), _cummax_rows(x, row, True))
    gmax = jnp.max(tot + x, axis=0, keepdims=True)
    xt = x.T
    bc_scr[kidx] = bc
    cm_scr[kidx] = _round_up_bf16(cm)
    xt_scr[kidx, 0:SUBLANES, :] = xt[SUBLANES:2 * SUBLANES]
    xt_scr[kidx, SUBLANES:2 * SUBLANES, :] = xt[3 * SUBLANES:4 * SUBLANES]
    tot_scr[kidx] = jnp.broadcast_to(tot, (SUBLANES, LANES))
    gmax_scr[kidx] = jnp.broadcast_to(gmax, (SUBLANES, LANES))


def _mlstm_chunk(q, v, kidx, step, d, scr, need_h, dk):
    bc_scr, cm_scr, xt_scr, kt_scr, tot_scr, m_in_scr, m_out_scr, s_scr = scr
    L = q.shape[0]
    ti = lax.broadcasted_iota(jnp.int32, (L, L), 0)
    si = lax.broadcasted_iota(jnp.int32, (L, L), 1)
    causal = (si <= ti) if d == 0 else (si >= ti)
    lane = lax.broadcasted_iota(jnp.int32, (L, LANES), 1)
    head0_rows = lax.broadcasted_iota(jnp.int32, (LANES, L), 0) < dk
    srow = lax.broadcasted_iota(jnp.int32, (LANES, 2 * LANES), 0)
    k_t = kt_scr[kidx]
    tot_v, m_in_v, m_out_v = tot_scr[kidx], m_in_scr[step], m_out_scr[step]
    ones = jnp.ones((L, LANES), BF16)
    v1 = jnp.concatenate([v[:, :LANES], ones, v[:, LANES:], ones], axis=1)
    cf0 = FWD_GATE_LANES * d + SUBLANES
    pick = lambda vals, e: vals[0:1, cf0 + e:cf0 + e + 1]
    x_rows = [xt_scr[kidx, SUBLANES * d + e:SUBLANES * d + e + 1, :] for e in (0, 1)]
    s_old = s_scr[d]
    hs = []
    if need_h:
        bc = bc_scr[kidx]
        sel_lane = lax.broadcasted_iota(jnp.int32, (LANES, 2 * LANES), 0)
        sel_col = lax.broadcasted_iota(jnp.int32, (LANES, 2 * LANES), 1)
        sel = (sel_lane == jnp.where(sel_col < LANES, cf0, cf0 + 1)).astype(BF16)
        mx_all = jnp.maximum(_round_up_bf16(m_in_v[0:1, :]), cm_scr[kidx])
        mx_tiles = jnp.dot(mx_all, sel, preferred_element_type=F32)
        q_heads = [jnp.where((lane >= e * dk) & (lane < (e + 1) * dk), q, jnp.zeros_like(q)) for e in (0, 1)]
        q2 = jnp.concatenate(q_heads, axis=0)
        qk = jnp.dot(q2, k_t, preferred_element_type=F32)
        qs = jnp.dot(q2, s_old.astype(BF16), preferred_element_type=F32)
        for e in (0, 1):
            cf = cf0 + e
            mx = mx_tiles[:, e * LANES:(e + 1) * LANES]
            w = qk[e * L:(e + 1) * L] * jnp.exp(jnp.where(causal, x_rows[e] - mx, -jnp.inf))
            s_inter = jnp.exp(pick(m_in_v, e) - mx)
            wv = jnp.dot(w.astype(BF16), v1[:, 2 * e * LANES:2 * (e + 1) * LANES],
                         preferred_element_type=F32)
            qs_e = qs[e * L:(e + 1) * L]
            num = s_inter * qs_e[:, :LANES] + wv[:, :LANES]
            den = s_inter * qs_e[:, LANES:] + wv[:, LANES:]
            m_row = bc[:, cf:cf + 1] + mx[:, cf:cf + 1]
            hs.append(num * (1.0 / jnp.maximum(jnp.abs(den[:, cf:cf + 1]), jnp.exp(-m_row))))
    wg_rows = [jnp.exp(pick(tot_v, e) + x_rows[e] - pick(m_out_v, e)) for e in (0, 1)]
    decays = [jnp.exp(pick(tot_v, e) + pick(m_in_v, e) - pick(m_out_v, e)) for e in (0, 1)]
    kw = (k_t.astype(F32) * jnp.where(head0_rows, wg_rows[0], wg_rows[1])).astype(BF16)
    own = jnp.concatenate([jnp.dot(kw[:dk], v1[:, :2 * LANES], preferred_element_type=F32),
                           jnp.dot(kw[dk:], v1[:, 2 * LANES:], preferred_element_type=F32)], axis=0)
    s_scr[d] = jnp.where(srow < dk, decays[0], decays[1]) * s_old + own
    return hs


def _mlstm_body(q_ref, k_ref, v_ref, o_ref, gt_ref, qc_ref, kc_ref, vc_ref, gtc_ref, bias_ref, gain_ref,
                out_ref, hs_scr, bc_scr, cm_scr, xt_scr, kt_scr, tot_scr, gmax_scr, m_in_scr, m_out_scr,
                s_scr, *, dk):
    t = q_ref.shape[1]
    tc = qc_ref.shape[1]
    L = CHUNK
    nc, ncc = t // L, tc // L
    n_steps = nc + ncc
    shift = (LANES - 2 * pl.program_id(1)) % LANES
    bias = pltpu.roll(jnp.broadcast_to(bias_ref[...], (SUBLANES, LANES)), shift, 1)[0:1, :]
    prep_scr = (bc_scr, cm_scr, xt_scr, tot_scr, gmax_scr)
    state_scr = (bc_scr, cm_scr, xt_scr, kt_scr, tot_scr, m_in_scr, m_out_scr, s_scr)
    scaled_t = lambda k: (k.astype(F32).T * (dk ** -0.5)).astype(BF16)

    for c in range(ncc):
        rows = slice(c * L, (c + 1) * L)
        _mlstm_gate_prep(pltpu.roll(gtc_ref[0, rows, :], shift, 1) + bias, c, *prep_scr)
        kt_scr[c] = scaled_t(kc_ref[0, rows, :])

    def prep(c, carry):
        rows = pl.ds(pl.multiple_of(c * L, L), L)
        _mlstm_gate_prep(pltpu.roll(gt_ref[0, rows, :], shift, 1) + bias, c + ncc, *prep_scr)
        kt_scr[c + ncc] = scaled_t(k_ref[0, rows, :])
        return carry

    lax.fori_loop(0, nc, prep, 0, unroll=8)

    fwd_id = lambda i: i
    bwd_id = lambda i: (ncc - 1 - i) if i < ncc else (n_steps - 1 - (i - ncc))
    fwd8 = lax.broadcasted_iota(jnp.int32, (SUBLANES, LANES), 1) < FWD_GATE_LANES
    m = jnp.zeros((SUBLANES, LANES), F32)
    for i in range(n_steps):
        tot = jnp.where(fwd8, tot_scr[fwd_id(i)], tot_scr[bwd_id(i)])
        gmax = jnp.where(fwd8, gmax_scr[fwd_id(i)], gmax_scr[bwd_id(i)])
        m_in_scr[i] = m
        m = jnp.maximum(tot + m, gmax)
        m_out_scr[i] = m

    s_scr[...] = jnp.zeros_like(s_scr)
    hs_scr[...] = jnp.zeros_like(hs_scr)
    for i in range(ncc):
        for d, cid in ((0, fwd_id(i)), (1, bwd_id(i))):
            rows = slice(cid * L, (cid + 1) * L)
            _mlstm_chunk(qc_ref[0, rows, :], vc_ref[0, rows, :], cid, i, d, state_scr, False, dk)

    def body(j, carry):
        for d in (0, 1):
            cj = j if d == 0 else nc - 1 - j
            rows = pl.ds(pl.multiple_of(cj * L, L), L)
            hs = _mlstm_chunk(q_ref[0, rows, :], v_ref[0, rows, :], cj + ncc, j + ncc, d, state_scr, True, dk)
            hs_scr[rows, :] += jnp.concatenate(hs, axis=1)
        return carry

    lax.fori_loop(0, nc, body, 0, unroll=8)

    def fin(c, carry):
        rows = pl.ds(pl.multiple_of(c * L, L), L)
        hh = hs_scr[rows, :]
        outs = []
        for e in (0, 1):
            x = hh[:, e * LANES:(e + 1) * LANES]
            outs.append(x * lax.rsqrt(jnp.mean(x * x, axis=-1, keepdims=True) + EPS))
        y = jnp.concatenate(outs, axis=1) * gain_ref[...] * _sigmoid(o_ref[0, rows, :])
        out_ref[0, rows, :] = y.astype(out_ref.dtype)
        return carry

    lax.fori_loop(0, nc, fin, 0, unroll=4)


def _mlstm_call(q, k, v, o, gt, qc, kc, vc, gtc, bias, gain):
    b, t, d_qk = q.shape
    tc = qc.shape[1]
    d_v = v.shape[2]
    pairs = M_HEADS // 2
    dk = d_qk // M_HEADS
    assert d_qk // pairs == LANES and d_v // pairs == 2 * LANES and CHUNK == LANES
    n_chunks = (t + tc) // CHUNK
    seq = lambda n, w: pl.BlockSpec((1, n, w), lambda bi, p: (bi, 0, p))
    allg = lambda n: pl.BlockSpec((1, n, LANES), lambda bi, p: (bi, 0, 0))
    chunk_f32 = lambda rows, cols: pltpu.VMEM((n_chunks, rows, cols), F32)
    return pl.pallas_call(
        functools.partial(_mlstm_body, dk=dk),
        grid=(b, pairs),
        in_specs=[seq(t, LANES), seq(t, LANES), seq(t, 2 * LANES), seq(t, 2 * LANES), allg(t),
                  seq(tc, LANES), seq(tc, LANES), seq(tc, 2 * LANES), allg(tc),
                  pl.BlockSpec((1, LANES), lambda bi, p: (0, 0)),
                  pl.BlockSpec((1, 2 * LANES), lambda bi, p: (0, p))],
        out_specs=seq(t, 2 * LANES),
        out_shape=jax.ShapeDtypeStruct((b, t, d_v), F32),
        scratch_shapes=[pltpu.VMEM((t, 2 * LANES), F32),
                        chunk_f32(CHUNK, LANES), pltpu.VMEM((n_chunks, CHUNK, LANES), BF16),
                        chunk_f32(2 * SUBLANES, CHUNK),
                        pltpu.VMEM((n_chunks, LANES, CHUNK), BF16),
                        chunk_f32(SUBLANES, LANES), chunk_f32(SUBLANES, LANES),
                        chunk_f32(SUBLANES, LANES), chunk_f32(SUBLANES, LANES),
                        pltpu.VMEM((2, LANES, 2 * LANES), F32)],
        compiler_params=_cparams(("parallel", "parallel")),
        name="mlstm",
    )(q, k, v, o, gt, qc, kc, vc, gtc, bias, gain)


def _outproj_body(x_ref, lru_ref, mls_ref, wa_ref, wb_ref, g_ref, o_ref):
    w_cols, r, dm = mls_ref.shape[1:]
    m = jnp.swapaxes(mls_ref[0], 0, 1).reshape(w_cols * r, dm).astype(BF16)
    y = (jnp.dot(lru_ref[0], wa_ref[...], preferred_element_type=F32)
         + jnp.dot(m, wb_ref[...], preferred_element_type=F32))
    o_ref[0] = x_ref[0] + g_ref[0] * y


def _outproj_call(x3d, lru, mls_cm, w_a, w_b, gate):
    b, t, d = x3d.shape
    da, dm = lru.shape[2], mls_cm.shape[2]
    rows = t // GRID_W
    r_tile = PROJ_TM // GRID_W
    assert r_tile == SUBLANES
    mls_view = mls_cm.reshape(b, GRID_W, rows, dm)
    return pl.pallas_call(
        _outproj_body,
        grid=(b, t // PROJ_TM),
        in_specs=[pl.BlockSpec((1, PROJ_TM, d), lambda bi, i: (bi, i, 0)),
                  pl.BlockSpec((1, PROJ_TM, da), lambda bi, i: (bi, i, 0)),
                  pl.BlockSpec((1, GRID_W, r_tile, dm), lambda bi, i: (bi, 0, i, 0)),
                  pl.BlockSpec((da, d), lambda bi, i: (0, 0)),
                  pl.BlockSpec((dm, d), lambda bi, i: (0, 0)),
                  pl.BlockSpec((1, 1, d), lambda bi, i: (bi, 0, 0))],
        out_specs=pl.BlockSpec((1, PROJ_TM, d), lambda bi, i: (bi, i, 0)),
        out_shape=jax.ShapeDtypeStruct((b, t, d), F32),
        compiler_params=_cparams(("parallel", "parallel")),
        name="outproj",
    )(x3d, lru, mls_view, w_a, w_b, gate)


def kernel(x, c, ctx, c_ctx, ada_w, ada_b, ffn1_norm, ffn1_w_up, ffn1_w_down, mix_norm, w_in, b_mgate, lru_conv_w, lru_conv_b, lru_w_r, lru_b_r, lru_w_i, lru_b_i, lru_lam, mlstm_norm, w_out, ffn2_norm, ffn2_w_up, ffn2_w_down, final_norm):
    b, t, d = x.shape
    tc = ctx.shape[1]
    assert ada_w.shape[0] == 1, "single-layer block only"
    assert t % GRID_W == 0 and t % PROJ_TM == 0
    rows = t // GRID_W
    d_lru = lru_conv_w.shape[2]
    d_mv = mlstm_norm.shape[1]
    d_mqk = (w_in.shape[2] - 2 * d_lru - 2 * d_mv - 4 * M_HEADS) // 2

    pad = SUBLANES - b - 1
    cc = jnp.concatenate([c, c_ctx[None, :], jnp.zeros((pad, d), F32)], axis=0)
    mod = _ada_call(cc, ada_w[0], ada_b[0][None, :]).reshape(SUBLANES, N_MOD, d)
    lat = lambda i: mod[:b, i][:, None, :]
    cxt = lambda i: mod[b:b + 1, i][:, None, :]
    row = lambda v: v[0][None, :]

    x1, ctx1 = _ffn_call(x.reshape(b * t, d), (lat(0), lat(1), lat(2)), row(ffn1_norm), ffn1_w_up[0],
                         ffn1_w_down[0], row(ffn1_norm), rows_per_mod=t, final_norm=False, name="ffn_pre",
                         extra=(ctx.reshape(b * tc, d), (cxt(0), cxt(1), cxt(2))))
    x1 = x1.reshape(b, t, d)
    ctx1 = ctx1.reshape(b, tc, d)

    n_gate = 4 * M_HEADS
    w_a, w_b = _split_w_in(w_in[0].T, 2 * d_lru)
    group_a = (w_a, w_a.shape[1], 0, (d_lru, d_lru), (F32, F32))
    group_b = (w_b, w_b.shape[1], 0, (d_mqk, d_mqk, d_mv, d_mv, LANES), (BF16, BF16, BF16, F32, F32))
    mixn = row(mix_norm)
    lx, lg = _proj_call(x1, mixn, lat(3), lat(4), [group_a], tm=PROJ_TM, colmajor=False, name="proj_lru")
    q, k, v, o, gt = _proj_call(x1.reshape(b, rows, GRID_W, d), mixn, lat(3), lat(4), [group_b],
                                tm=PROJ_TM, colmajor=True, name="proj_mlstm")
    d_qkv = 2 * d_mqk + d_mv
    gate_col = d_qkv + d_mv
    assert gate_col % LANES == 0
    ctx_groups = [(w_a, d_lru, 0, (d_lru,), (F32,)),
                  (w_b, d_qkv, 0, (d_mqk, d_mqk, d_mv), (BF16, BF16, BF16)),
                  (w_b, LANES, gate_col // LANES, (LANES,), (F32,))]
    ctx_mod = lambda i: jnp.broadcast_to(cxt(i), (b, 1, d))
    lxc, qc, kc, vc, gtc = _proj_call(ctx1, mixn, ctx_mod(3), ctx_mod(4), ctx_groups, tm=tc, colmajor=False,
                                      name="proj_ctx")

    lru_lat = _lru_call(lx, lg, lxc, lru_conv_w[0], row(lru_conv_b), lru_w_r[0], lru_b_r[0], lru_w_i[0],
                        lru_b_i[0], lru_lam[0])
    bias = jnp.concatenate([b_mgate[0], jnp.zeros((LANES - n_gate,), F32)])[None, :]
    mls_cm = _mlstm_call(q, k, v, o, gt, qc, kc, vc, gtc, bias, row(mlstm_norm))

    w_out0 = w_out[0].astype(BF16)
    x2 = _outproj_call(x1, lru_lat, mls_cm, w_out0[:d_lru], w_out0[d_lru:], lat(5))

    out, = _ffn_call(x2.reshape(b * t, d), (lat(6), lat(7), lat(8)), row(ffn2_norm), ffn2_w_up[0],
                     ffn2_w_down[0], final_norm[None, :], rows_per_mod=t, final_norm=True, name="ffn_post")
    return out.reshape(b, t, d)
```
